```python
import math
import jax, jax.numpy as jnp
from jax import lax
import numpy as np

D_MODEL = 2048
BATCH = 2
SEQ = 4096
DEPTH = 1
DEC_BATCH = 128
DEC_SEQ = 4
PAST_LEN = 2048
PAGE_SIZE = 128

FOX_HEADS = 16
FOX_HEAD_DIM = D_MODEL // 32
FOX_WIDTH = FOX_HEADS * FOX_HEAD_DIM
Q_BLOCK = 128
S5_WIDTH = D_MODEL // 2
S5_GROUP = 16
S5_GROUPS = S5_WIDTH // S5_GROUP
S5_STATE = 64
DT_MIN = 1e-3
DT_MAX = 1e-1
N_EXPERTS = 32
TOP_K = 4
D_EXPERT = D_MODEL
SWIGLU_LIMIT = 7.0
SWIGLU_ALPHA = 1.702
MOE_BLOCK = 128
DN_ALPHA = (2 * DEPTH) ** 0.25
DN_BETA = (8 * DEPTH) ** -0.25
LN_EPS = 1e-5
ADA_INIT = 0.5
IN_SIZES = (FOX_WIDTH, FOX_WIDTH, FOX_WIDTH, FOX_HEADS, S5_WIDTH, D_MODEL, D_MODEL)
D_IN = sum(IN_SIZES)

kernel_name = 'fox_s5_gated_moe_deepnorm_step'


def _layernorm(x, g, b):
    xf = x.astype(jnp.float32)
    mu = jnp.mean(xf, axis=-1, keepdims=True)
    var = jnp.mean(jnp.square(xf - mu), axis=-1, keepdims=True)
    return (xf - mu) * lax.rsqrt(var + LN_EPS) * g + b


def _post(x, gate, y, g, b):
    return _layernorm(DN_ALPHA * x + gate * y, g, b).astype(x.dtype)


def _adaln(c, w_ada, b_ada):
    mod = jax.nn.silu(c) @ w_ada + b_ada
    return tuple(jnp.split(mod[:, None, :], 6, axis=-1))


def _modulate(x, shift, scale):
    return x * (1 + scale) + shift


def _mixer_inputs(u, w_in, fox_b_f):
    h = u @ w_in
    cuts = [int(c) for c in np.cumsum(IN_SIZES)[:-1]]
    q, k, v, f, s_in, ga, gb = jnp.split(h, cuts, axis=-1)
    hd = u.shape[:2] + (FOX_HEADS, FOX_HEAD_DIM)
    logf = jax.nn.log_sigmoid((f + fox_b_f).astype(jnp.float32))
    return q.reshape(hd), k.reshape(hd), v.reshape(hd), logf, s_in, ga, gb


def _fox_attend(q, cq, qpos, segs):
    scale = FOX_HEAD_DIM ** -0.5
    cq_t = jnp.swapaxes(cq, 1, 2)[..., None]
    logits = []
    for k, _, ck, kpos in segs:
        s = jnp.einsum('bqhd,bshd->bhqs', q, k, preferred_element_type=jnp.float32) * scale
        s = s + cq_t - jnp.swapaxes(ck, 1, 2)[:, :, None, :]
        logits.append(jnp.where(kpos[None, :] <= qpos[:, None], s, -jnp.inf))
    p = jax.nn.softmax(jnp.concatenate(logits, axis=-1), axis=-1)
    outs = []
    off = 0
    for _, v, _, kpos in segs:
        n = kpos.shape[0]
        outs.append(jnp.einsum('bhqs,bshd->bqhd', p[..., off:off + n].astype(v.dtype), v,
                               preferred_element_type=jnp.float32))
        off += n
    return sum(outs)


def _fox_prompt(q, k, v, logf):
    B, L = q.shape[:2]
    c = jnp.cumsum(logf, axis=1)
    pos = jnp.arange(L)
    nb = L // Q_BLOCK
    qb = q.reshape(B, nb, Q_BLOCK, FOX_HEADS, FOX_HEAD_DIM).swapaxes(0, 1)
    cb = c.reshape(B, nb, Q_BLOCK, FOX_HEADS).swapaxes(0, 1)
    pb = pos.reshape(nb, Q_BLOCK)
    seg = [(k, v, c, pos)]
    ob = lax.map(lambda a: _fox_attend(a[0], a[1], a[2], seg), (qb, cb, pb))
    return ob.swapaxes(0, 1).reshape(B, L, FOX_WIDTH)


def _fox_sample(q, k, v, logf, l, cache_k, cache_v, cache_logf, page_table):
    Bd, Ld = q.shape[:2]
    P = page_table.shape[1] * PAGE_SIZE
    k_past = cache_k[l, page_table].reshape(Bd, P, FOX_HEADS, FOX_HEAD_DIM)
    v_past = cache_v[l, page_table].reshape(Bd, P, FOX_HEADS, FOX_HEAD_DIM)
    c_past = jnp.cumsum(cache_logf[l, page_table].reshape(Bd, P, FOX_HEADS).astype(jnp.float32), axis=1)
    c_new = c_past[:, -1:] + jnp.cumsum(logf, axis=1)
    pos_past = jnp.arange(P)
    pos_new = P + jnp.arange(Ld)
    out = _fox_attend(q, c_new, pos_new,
                      [(k_past, v_past, c_past, pos_past), (k, v, c_new, pos_new)])
    return out.reshape(Bd, Ld, FOX_WIDTH)


def _ssm_combine(e1, e2):
    a1r, a1i, b1r, b1i = e1
    a2r, a2i, b2r, b2i = e2
    return (a2r * a1r - a2i * a1i, a2r * a1i + a2i * a1r,
            a2r * b1r - a2i * b1i + b2r, a2r * b1i + a2i * b1r + b2i)


def _s5(u, h0_re, h0_im, a_re, a_im, log_dt, b_re, b_im, c_re, c_im, d_skip, w_glu, b_glu):
    f32 = jnp.float32
    B, L, _ = u.shape
    ug = u.reshape(B, L, S5_GROUPS, S5_GROUP).astype(f32)
    dt = jnp.exp(log_dt.astype(f32))[:, None]
    ar, ai = a_re.astype(f32), a_im.astype(f32)
    mag = jnp.exp(ar * dt)
    lr, li = mag * jnp.cos(ai * dt), mag * jnp.sin(ai * dt)
    den = ar * ar + ai * ai
    zr = ((lr - 1) * ar + li * ai) / den
    zi = (li * ar - (lr - 1) * ai) / den
    br, bi = b_re.astype(f32), b_im.astype(f32)
    bbr = zr[..., None] * br - zi[..., None] * bi
    bbi = zr[..., None] * bi + zi[..., None] * br
    bu_r = jnp.einsum('blgc,gnc->blgn', ug, bbr)
    bu_i = jnp.einsum('blgc,gnc->blgn', ug, bbi)
    A_r, A_i, hr, hi = lax.associative_scan(
        _ssm_combine, (jnp.broadcast_to(lr, bu_r.shape), jnp.broadcast_to(li, bu_r.shape), bu_r, bu_i), axis=1)
    h0r, h0i = h0_re[:, None].astype(f32), h0_im[:, None].astype(f32)
    hr = hr + A_r * h0r - A_i * h0i
    hi = hi + A_r * h0i + A_i * h0r
    y = jnp.einsum('blgn,gcn->blgc', hr, c_re) - jnp.einsum('blgn,gcn->blgc', hi, c_im)
    y = y.reshape(B, L, S5_WIDTH) + d_skip * u
    z = jax.nn.gelu(y)
    return z * jax.nn.sigmoid(z @ w_glu + b_glu), hr[:, -1], hi[:, -1]


def _mixer_out(attn, ssm, ga, gb, p_a, p_b, w_o):
    m = jax.nn.sigmoid(ga) * (attn @ p_a) + jax.nn.sigmoid(gb) * (ssm @ p_b)
    return m @ w_o


def _moe(u, l, w_router, b_router, w_gu, b_gu, w_dn, b_dn):
    shp = u.shape
    t = u.reshape(-1, D_MODEL)
    T = t.shape[0]
    logits = jnp.dot(t, w_router[l], preferred_element_type=jnp.float32) + b_router[l]
    top_v, top_i = lax.top_k(logits, TOP_K)
    wts = jax.nn.softmax(top_v, axis=-1)
    A = T * TOP_K
    flat_e = top_i.reshape(A).astype(jnp.int32)
    flat_tok = jnp.arange(A, dtype=jnp.int32) // TOP_K
    order = jnp.argsort(flat_e)
    sorted_e = flat_e[order]
    counts = jnp.bincount(flat_e, length=N_EXPERTS)
    padded = (counts + MOE_BLOCK - 1) // MOE_BLOCK * MOE_BLOCK
    pad_end = jnp.cumsum(padded)
    pad_start = pad_end - padded
    start = jnp.cumsum(counts) - counts
    dest = pad_start[sorted_e] + jnp.arange(A, dtype=jnp.int32) - start[sorted_e]
    n_blocks = -(-A // MOE_BLOCK) + N_EXPERTS
    buf_tok = jnp.full((n_blocks * MOE_BLOCK,), T, jnp.int32).at[dest].set(flat_tok[order])
    blk_e = jnp.minimum(jnp.searchsorted(pad_end, jnp.arange(n_blocks) * MOE_BLOCK, side='right'), N_EXPERTS - 1)
    xb = jnp.concatenate([t, jnp.zeros((1, D_MODEL), t.dtype)])[buf_tok].reshape(n_blocks, MOE_BLOCK, D_MODEL)

    def expert_block(args):
        xe, e = args
        h = xe @ w_gu[l, e] + b_gu[l, e]
        gate = jnp.minimum(h[:, :D_EXPERT], SWIGLU_LIMIT)
        up = jnp.clip(h[:, D_EXPERT:], -SWIGLU_LIMIT, SWIGLU_LIMIT)
        return ((up + 1) * (gate * jax.nn.sigmoid(SWIGLU_ALPHA * gate))) @ w_dn[l, e] + b_dn[l, e]

    yb = lax.map(expert_block, (xb, blk_e)).reshape(-1, D_MODEL)
    dest_orig = jnp.zeros((A,), dest.dtype).at[order].set(dest)
    y = jnp.einsum('tk,tkd->td', wts, yb[dest_orig].reshape(T, TOP_K, D_MODEL))
    return y.reshape(shp)


def setup_inputs(seed: int = 0) -> dict:
    key = jax.random.key(seed)
    ks = iter(jax.random.split(key, 40))
    f32 = jnp.float32

    def nrm(shape, scale=1.0):
        return jax.random.normal(next(ks), shape, f32) * scale

    n_pages = PAST_LEN // PAGE_SIZE
    n_used = DEC_BATCH * n_pages
    n_phys = n_used + max(1, n_used // 4)
    L, D, H, Dh = DEPTH, D_MODEL, FOX_HEADS, FOX_HEAD_DIM
    G, N, C = S5_GROUPS, S5_STATE, S5_GROUP
    E, F = N_EXPERTS, D_EXPERT
    x_prompt = nrm((BATCH, SEQ, D))
    x_sample = nrm((DEC_BATCH, DEC_SEQ, D))
    c_prompt = nrm((BATCH, D))
    c_sample = nrm((DEC_BATCH, D))
    cache_k = nrm((L, n_phys, PAGE_SIZE, H, Dh))
    cache_v = nrm((L, n_phys, PAGE_SIZE, H, Dh))
    cache_logf = jax.nn.log_sigmoid(4.0 + nrm((L, n_phys, PAGE_SIZE, H)))
    state_s5_re = nrm((L, DEC_BATCH, G, N))
    state_s5_im = nrm((L, DEC_BATCH, G, N))
    page_table = jax.random.permutation(next(ks), n_phys)[:n_used].reshape(DEC_BATCH, n_pages).astype(jnp.int32)
    w_ada = nrm((L, D, 6 * D), ADA_INIT * D ** -0.5)
    b_ada = nrm((L, 6 * D), 0.01)
    v_lo = 2 * FOX_WIDTH
    col_scale = jnp.ones((D_IN,), f32).at[v_lo:v_lo + FOX_WIDTH].set(DN_BETA)
    w_in = nrm((L, D, D_IN), D ** -0.5) * col_scale
    fox_b_f = 2.0 + 4.0 * jax.random.uniform(next(ks), (L, H), f32)
    s5_a_re = -0.5 * jnp.exp(nrm((L, G, N), 0.01))
    s5_a_im = math.pi * jnp.arange(N, dtype=f32) + nrm((L, G, N), 0.01)
    s5_log_dt = jax.random.uniform(next(ks), (L, G), f32, math.log(DT_MIN), math.log(DT_MAX))
    s5_b_re = nrm((L, G, N, C), (2 * C) ** -0.5)
    s5_b_im = nrm((L, G, N, C), (2 * C) ** -0.5)
    s5_c_re = nrm((L, G, C, N), (2 * N) ** -0.5)
    s5_c_im = nrm((L, G, C, N), (2 * N) ** -0.5)
    s5_d = nrm((L, S5_WIDTH))
    w_glu = nrm((L, S5_WIDTH, S5_WIDTH), S5_WIDTH ** -0.5)
    b_glu = nrm((L, S5_WIDTH), 0.01)
    p_a = nrm((L, FOX_WIDTH, D), FOX_WIDTH ** -0.5)
    p_b = nrm((L, S5_WIDTH, D), S5_WIDTH ** -0.5)
    w_o = nrm((L, D, D), DN_BETA * D ** -0.5)
    ln1_g = 1.0 + nrm((L, D), 0.01)
    ln1_b = nrm((L, D), 0.01)
    w_router = nrm((L, D, E), D ** -0.5)
    b_router = nrm((L, E), 0.01)
    w_gu = nrm((L, E, D, 2 * F), DN_BETA * D ** -0.5)
    b_gu = nrm((L, E, 2 * F), 0.01)
    w_dn = nrm((L, E, F, D), DN_BETA * F ** -0.5)
    b_dn = nrm((L, E, D), 0.01)
    ln2_g = 1.0 + nrm((L, D), 0.01)
    ln2_b = nrm((L, D), 0.01)
    return {'x_prompt': x_prompt, 'x_sample': x_sample, 'c_prompt': c_prompt, 'c_sample': c_sample,
            'cache_k': cache_k, 'cache_v': cache_v, 'cache_logf': cache_logf,
            'state_s5_re': state_s5_re, 'state_s5_im': state_s5_im, 'page_table': page_table,
            'w_ada': w_ada, 'b_ada': b_ada, 'w_in': w_in, 'fox_b_f': fox_b_f,
            's5_a_re': s5_a_re, 's5_a_im': s5_a_im, 's5_log_dt': s5_log_dt,
            's5_b_re': s5_b_re, 's5_b_im': s5_b_im, 's5_c_re': s5_c_re, 's5_c_im': s5_c_im,
            's5_d': s5_d, 'w_glu': w_glu, 'b_glu': b_glu, 'p_a': p_a, 'p_b': p_b, 'w_o': w_o,
            'ln1_g': ln1_g, 'ln1_b': ln1_b, 'w_router': w_router, 'b_router': b_router,
            'w_gu': w_gu, 'b_gu': b_gu, 'w_dn': w_dn, 'b_dn': b_dn, 'ln2_g': ln2_g, 'ln2_b': ln2_b}


def reference(x_prompt, x_sample, c_prompt, c_sample, cache_k, cache_v, cache_logf,
              state_s5_re, state_s5_im, page_table, w_ada, b_ada, w_in, fox_b_f,
              s5_a_re, s5_a_im, s5_log_dt, s5_b_re, s5_b_im, s5_c_re, s5_c_im, s5_d,
              w_glu, b_glu, p_a, p_b, w_o, ln1_g, ln1_b, w_router, b_router,
              w_gu, b_gu, w_dn, b_dn, ln2_g, ln2_b):
    xp, xs = x_prompt, x_sample
    kp_l, vp_l, fp_l, srp_l, sip_l = [], [], [], [], []
    ks_l, vs_l, fs_l, srs_l, sis_l = [], [], [], [], []
    for l in range(DEPTH):
        s5 = (s5_a_re[l], s5_a_im[l], s5_log_dt[l], s5_b_re[l], s5_b_im[l],
              s5_c_re[l], s5_c_im[l], s5_d[l], w_glu[l], b_glu[l])
        mp = _adaln(c_prompt, w_ada[l], b_ada[l])
        ms = _adaln(c_sample, w_ada[l], b_ada[l])

        qp, kp, vp, fp, sp, gap, gbp = _mixer_inputs(_modulate(xp, mp[0], mp[1]), w_in[l], fox_b_f[l])
        att_p = _fox_prompt(qp, kp, vp, fp)
        h0 = jnp.zeros((xp.shape[0], S5_GROUPS, S5_STATE), jnp.float32)
        ssm_p, hr_p, hi_p = _s5(sp, h0, h0, *s5)
        xp = _post(xp, mp[2], _mixer_out(att_p, ssm_p, gap, gbp, p_a[l], p_b[l], w_o[l]), ln1_g[l], ln1_b[l])

        qs, ks_, vs, fs, ss, gas, gbs = _mixer_inputs(_modulate(xs, ms[0], ms[1]), w_in[l], fox_b_f[l])
        att_s = _fox_sample(qs, ks_, vs, fs, l, cache_k, cache_v, cache_logf, page_table)
        ssm_s, hr_s, hi_s = _s5(ss, state_s5_re[l], state_s5_im[l], *s5)
        xs = _post(xs, ms[2], _mixer_out(att_s, ssm_s, gas, gbs, p_a[l], p_b[l], w_o[l]), ln1_g[l], ln1_b[l])

        xp = _post(xp, mp[5], _moe(_modulate(xp, mp[3], mp[4]), l, w_router, b_router, w_gu, b_gu, w_dn, b_dn),
                   ln2_g[l], ln2_b[l])
        xs = _post(xs, ms[5], _moe(_modulate(xs, ms[3], ms[4]), l, w_router, b_router, w_gu, b_gu, w_dn, b_dn),
                   ln2_g[l], ln2_b[l])

        kp_l.append(kp); vp_l.append(vp); fp_l.append(fp); srp_l.append(hr_p); sip_l.append(hi_p)
        ks_l.append(ks_); vs_l.append(vs); fs_l.append(fs); srs_l.append(hr_s); sis_l.append(hi_s)

    return (xp, xs,
            jnp.stack(kp_l), jnp.stack(vp_l), jnp.stack(fp_l), jnp.stack(srp_l), jnp.stack(sip_l),
            jnp.stack(ks_l), jnp.stack(vs_l), jnp.stack(fs_l), jnp.stack(srs_l), jnp.stack(sis_l))
```

```python
import functools
import math

import jax
import jax.numpy as jnp
from jax import lax
from jax.experimental import pallas as pl
from jax.experimental.pallas import tpu as pltpu

F32 = jnp.float32
BF16 = jnp.bfloat16

HEADS = 16
HEAD_DIM = 64
PAGE = 128
S5_GROUPS = 64
S5_GROUP = 16
S5_STATE = 64
N_EXPERTS = 32
TOP_K = 4
SWIGLU_LIMIT = 7.0
SWIGLU_ALPHA = 1.702
DN_ALPHA = 2.0 ** 0.25
LN_EPS = 1e-5
NEG = -1e30

LANES = 128
S5_CHUNKS = 8
S5_CHUNK_STATE = S5_GROUPS * S5_STATE // S5_CHUNKS
S5_CHUNK_IN = S5_GROUPS * S5_GROUP // S5_CHUNKS
S5_LANE_BLOCKS = S5_CHUNK_STATE // LANES
MOE_BLOCK = 1152
MOE_SUB = 384
MOE_FT = 256
VMEM_LIMIT = 56 * 1024 * 1024


def _cp(sem, vmem=VMEM_LIMIT):
    return pltpu.CompilerParams(dimension_semantics=sem, vmem_limit_bytes=vmem)


def _dot(a, b):
    return jnp.dot(a, b, preferred_element_type=F32)


def _dot_nt(a, b):
    return lax.dot_general(a, b, (((1,), (1,)), ((), ())), preferred_element_type=F32)


def _log_sigmoid(x):
    return jnp.minimum(x, 0.0) - jnp.log1p(jnp.exp(-jnp.abs(x)))


def _split3(v):
    hi = v.astype(BF16)
    r = v - hi.astype(F32)
    mid = r.astype(BF16)
    lo = (r - mid.astype(F32)).astype(BF16)
    return hi, mid, lo


def _layernorm(v, g, b):
    mu = jnp.mean(v, axis=-1, keepdims=True)
    d = v - mu
    var = jnp.mean(d * d, axis=-1, keepdims=True)
    return d * lax.rsqrt(var + LN_EPS) * g + b


def _ada_kernel(c_ref, w_ref, b_ref, o_ref):
    c = c_ref[...]
    a = (c * jax.nn.sigmoid(c)).astype(BF16)
    o_ref[...] = _dot(a, w_ref[...].astype(BF16)) + b_ref[...]


def _adaln(c_all, w_ada, b_ada):
    R, D = c_all.shape
    N = w_ada.shape[1]
    tn = 1024
    return pl.pallas_call(
        _ada_kernel,
        grid=(N // tn,),
        in_specs=[pl.BlockSpec((R, D), lambda j: (0, 0)),
                  pl.BlockSpec((D, tn), lambda j: (0, j)),
                  pl.BlockSpec((1, tn), lambda j: (0, j))],
        out_specs=pl.BlockSpec((R, tn), lambda j: (0, j)),
        out_shape=jax.ShapeDtypeStruct((R, N), F32),
        compiler_params=_cp(("arbitrary",)),
        name="adaln",
    )(c_all, w_ada, b_ada)


def _in_kernel(x_ref, sh_ref, sc_ref, wm_ref, wf_ref, wft_ref, bf_ref, bft_ref,
               h_ref, logf_ref, cum_ref, u_scr, carry_scr, *, seq_tiles, seg):
    i = pl.program_id(0)
    j = pl.program_id(1)
    tm = x_ref.shape[0]

    @pl.when(j == 0)
    def _():
        u = x_ref[...] * (1.0 + sc_ref[...]) + sh_ref[...]
        ub = u.astype(BF16)
        u_scr[...] = ub
        lf = _log_sigmoid(_dot(ub, wf_ref[...]) + bf_ref[...])
        logf_ref[...] = lf
        r = lax.broadcasted_iota(jnp.int32, (tm, tm), 0)
        c = lax.broadcasted_iota(jnp.int32, (tm, tm), 1)
        if seq_tiles > 0:
            lft = _log_sigmoid(_dot_nt(wft_ref[...], ub) + bft_ref[...])
            tri = (r <= c).astype(BF16)
            hi, mid, lo = _split3(lft)
            cs = _dot(hi, tri) + _dot(mid, tri) + _dot(lo, tri)

            @pl.when(i % seq_tiles == 0)
            def _():
                carry_scr[...] = jnp.zeros_like(carry_scr)

            cs = cs + carry_scr[...]
            cum_ref[...] = cs
            carry_scr[...] = cs[:, tm - 1:tm]
        else:
            tri = ((c <= r) & (c // seg == r // seg)).astype(BF16)
            hi, mid, lo = _split3(lf)
            cum_ref[...] = _dot(tri, hi) + _dot(tri, mid) + _dot(tri, lo)

    h_ref[...] = _dot(u_scr[...], wm_ref[...])


def _in_proj(x, shift, scale, wm, wf, wft, bf, bft, *, seq_len, seg):
    T, D = x.shape
    N = wm.shape[1]
    tm, tn = min(512, T), 1024
    nt = T // tm
    if seq_len > 0:
        seq_tiles = seq_len // tm
        mod_spec = pl.BlockSpec((None, 1, D), lambda i, j: (i // seq_tiles, 0, 0))
        cum_shape, cum_spec = (LANES, T), pl.BlockSpec((LANES, tm), lambda i, j: (0, i))
    else:
        seq_tiles = 0
        mod_spec = pl.BlockSpec((tm, D), lambda i, j: (i, 0))
        cum_shape, cum_spec = (T, LANES), pl.BlockSpec((tm, LANES), lambda i, j: (i, 0))
    kern = functools.partial(_in_kernel, seq_tiles=seq_tiles, seg=seg)
    return pl.pallas_call(
        kern,
        grid=(nt, N // tn),
        in_specs=[pl.BlockSpec((tm, D), lambda i, j: (i, 0)), mod_spec, mod_spec,
                  pl.BlockSpec((D, tn), lambda i, j: (0, j)),
                  pl.BlockSpec((D, LANES), lambda i, j: (0, 0)),
                  pl.BlockSpec((LANES, D), lambda i, j: (0, 0)),
                  pl.BlockSpec((1, LANES), lambda i, j: (0, 0)),
                  pl.BlockSpec((LANES, 1), lambda i, j: (0, 0))],
        out_specs=[pl.BlockSpec((tm, tn), lambda i, j: (i, j)),
                   pl.BlockSpec((tm, LANES), lambda i, j: (i, 0)),
                   cum_spec],
        out_shape=[jax.ShapeDtypeStruct((T, N), F32),
                   jax.ShapeDtypeStruct((T, LANES), F32),
                   jax.ShapeDtypeStruct(cum_shape, F32)],
        scratch_shapes=[pltpu.VMEM((tm, D), BF16), pltpu.VMEM((LANES, 1), F32)],
        compiler_params=_cp(("arbitrary", "arbitrary")),
        name="in_proj",
    )(x, shift, scale, wm, wf, wft, bf, bft)


def _fox_prompt_kernel(q_ref, k_ref, v_ref, c_ref, o_ref, qs, m_scr, l_scr, acc):
    hp = pl.program_id(1)
    qi = pl.program_id(2)
    kv = pl.program_id(3)
    tq, tk = q_ref.shape[0], k_ref.shape[0]
    lane = lax.broadcasted_iota(jnp.int32, (tq, LANES), 1)

    @pl.when(kv == 0)
    def _():
        q2 = q_ref[...] * (HEAD_DIM ** -0.5)
        qs[0] = jnp.where(lane < HEAD_DIM, q2, 0.0).astype(BF16)
        qs[1] = jnp.where(lane >= HEAD_DIM, q2, 0.0).astype(BF16)
        m_scr[...] = jnp.full_like(m_scr, NEG)
        l_scr[...] = jnp.zeros_like(l_scr)
        acc[...] = jnp.zeros_like(acc)

    @pl.when(kv <= qi)
    def _():
        kb = k_ref[...].astype(BF16)
        vb = v_ref[...].astype(BF16)
        row = lax.broadcasted_iota(jnp.int32, (tq, tk), 0) + qi * tq
        col = lax.broadcasted_iota(jnp.int32, (tq, tk), 1) + kv * tk
        keep = col <= row
        r0 = (2 * hp) % 8
        for a in range(2):
            crow = c_ref[pl.ds(r0 + a, 1), :]
            s = _dot_nt(qs[a], kb) - crow
            s = jnp.where(keep, s, NEG)
            m_prev = m_scr[a]
            m_new = jnp.maximum(m_prev, jnp.max(s, axis=1, keepdims=True))
            alpha = jnp.exp(m_prev - m_new)
            p = jnp.exp(s - m_new)
            l_scr[a] = alpha * l_scr[a] + jnp.sum(p, axis=1, keepdims=True)
            acc[a] = alpha * acc[a] + _dot(p.astype(BF16), vb)
            m_scr[a] = m_new

    @pl.when(kv == qi)
    def _():
        o = jnp.where(lane < HEAD_DIM, acc[0] / l_scr[0], acc[1] / l_scr[1])
        o_ref[...] = o.astype(o_ref.dtype)


def _fox_prompt(h, cT, B, L):
    tq = tk = 512
    nq, nk = L // tq, L // tk
    npairs = HEADS // 2
    return pl.pallas_call(
        _fox_prompt_kernel,
        grid=(B, npairs, nq, nk),
        in_specs=[
            pl.BlockSpec((tq, LANES), lambda b, hp, qi, kv: (b * nq + qi, hp)),
            pl.BlockSpec((tk, LANES), lambda b, hp, qi, kv: (b * nk + jnp.minimum(kv, qi), npairs + hp)),
            pl.BlockSpec((tk, LANES), lambda b, hp, qi, kv: (b * nk + jnp.minimum(kv, qi), 2 * npairs + hp)),
            pl.BlockSpec((8, tk), lambda b, hp, qi, kv: (hp // 4, b * nk + jnp.minimum(kv, qi))),
        ],
        out_specs=pl.BlockSpec((tq, LANES), lambda b, hp, qi, kv: (b * nq + qi, hp)),
        out_shape=jax.ShapeDtypeStruct((B * L, HEADS * HEAD_DIM), BF16),
        scratch_shapes=[pltpu.VMEM((2, tq, LANES), BF16), pltpu.VMEM((2, tq, 1), F32),
                        pltpu.VMEM((2, tq, 1), F32), pltpu.VMEM((2, tq, LANES), F32)],
        compiler_params=_cp(("parallel", "parallel", "arbitrary", "arbitrary")),
        name="fox_prompt",
    )(h, h, h, cT)


def _fox_sample_kernel(pt_ref, k_ref, v_ref, lf_ref, qe_ref, kn_ref, vn_ref, bn_ref,
                       o_ref, m_scr, l_scr, acc, carry):
    p_id = pl.program_id(1)
    n_pages = pl.num_programs(1)
    nrow = HEADS * 4
    ncol = PAGE * HEADS

    @pl.when(p_id == 0)
    def _():
        m_scr[...] = jnp.full_like(m_scr, NEG)
        l_scr[...] = jnp.zeros_like(l_scr)
        acc[...] = jnp.zeros_like(acc)
        carry[...] = jnp.zeros_like(carry)

    lane = lax.broadcasted_iota(jnp.int32, (8, ncol), 1)
    x = jnp.broadcast_to(lf_ref[...], (8, ncol))
    suf = x
    tot = x
    sh = HEADS
    while sh < ncol:
        suf = suf + jnp.where(lane + sh < ncol, pltpu.roll(suf, ncol - sh, 1), 0.0)
        tot = tot + pltpu.roll(tot, sh, 1)
        sh *= 2
    bias = (suf - x + carry[...])[0:1, :]
    carry[...] += tot

    row_head = lax.broadcasted_iota(jnp.int32, (nrow, ncol), 0) // 4
    col_head = lax.broadcasted_iota(jnp.int32, (nrow, ncol), 1) % HEADS
    s = _dot_nt(qe_ref[...].astype(BF16), k_ref[...].astype(BF16)) + bias
    s = jnp.where(row_head == col_head, s, NEG)
    m_prev = m_scr[...]
    m_new = jnp.maximum(m_prev, jnp.max(s, axis=1, keepdims=True))
    alpha = jnp.exp(m_prev - m_new)
    p = jnp.exp(s - m_new)
    l_scr[...] = alpha * l_scr[...] + jnp.sum(p, axis=1, keepdims=True)
    acc[...] = alpha * acc[...] + _dot(p.astype(BF16), v_ref[...].astype(BF16))
    m_scr[...] = m_new

    @pl.when(p_id == n_pages - 1)
    def _():
        q_of_row = lax.broadcasted_iota(jnp.int32, (nrow, 1), 0) % 4
        qe = qe_ref[...]
        bn = bn_ref[...]
        sn = []
        for j in range(4):
            sj = jnp.sum(qe * kn_ref[j], axis=1, keepdims=True) - bn[:, j:j + 1]
            sn.append(jnp.where(q_of_row >= j, sj, NEG))
        m_old = m_scr[...]
        m_fin = jnp.maximum(jnp.maximum(jnp.maximum(sn[0], sn[1]), jnp.maximum(sn[2], sn[3])), m_old)
        a_fin = jnp.exp(m_old - m_fin)
        l_fin = a_fin * l_scr[...]
        o_fin = a_fin * acc[...]
        for j in range(4):
            pj = jnp.exp(sn[j] - m_fin)
            l_fin = l_fin + pj
            o_fin = o_fin + pj * vn_ref[j]
        o_ref[...] = o_fin / l_fin


def _fox_sample(page_table, ck, cv, clf, qe, kn, vn, bn):
    Bd, n_pages = page_table.shape
    nrow = HEADS * 4
    ncol = PAGE * HEADS

    def page_map(b, p, pt):
        return (pt[b * n_pages + (n_pages - 1 - p)], 0, 0)

    grid_spec = pltpu.PrefetchScalarGridSpec(
        num_scalar_prefetch=1,
        grid=(Bd, n_pages),
        in_specs=[
            pl.BlockSpec((None, ncol, HEAD_DIM), page_map),
            pl.BlockSpec((None, ncol, HEAD_DIM), page_map),
            pl.BlockSpec((None, 1, ncol), page_map),
            pl.BlockSpec((None, nrow, HEAD_DIM), lambda b, p, pt: (b, 0, 0)),
            pl.BlockSpec((None, 4, nrow, HEAD_DIM), lambda b, p, pt: (b, 0, 0, 0)),
            pl.BlockSpec((None, 4, nrow, HEAD_DIM), lambda b, p, pt: (b, 0, 0, 0)),
            pl.BlockSpec((None, nrow, LANES), lambda b, p, pt: (b, 0, 0)),
        ],
        out_specs=pl.BlockSpec((None, nrow, HEAD_DIM), lambda b, p, pt: (b, 0, 0)),
        scratch_shapes=[pltpu.VMEM((nrow, 1), F32), pltpu.VMEM((nrow, 1), F32),
                        pltpu.VMEM((nrow, HEAD_DIM), F32), pltpu.VMEM((8, ncol), F32)],
    )
    return pl.pallas_call(
        _fox_sample_kernel,
        grid_spec=grid_spec,
        out_shape=jax.ShapeDtypeStruct((Bd, nrow, HEAD_DIM), F32),
        compiler_params=_cp(("parallel", "arbitrary")),
        name="fox_sample",
    )(page_table.reshape(-1), ck, cv, clf, qe, kn, vn, bn)


def _s5_param_kernel(ar_ref, ai_ref, ldt_ref, br_ref, bi_ref, lr_ref, li_ref, bbr_ref, bbi_ref):
    ar, ai = ar_ref[...], ai_ref[...]
    dt = jnp.exp(ldt_ref[...])
    mag = jnp.exp(ar * dt)
    lr = mag * jnp.cos(ai * dt)
    li = mag * jnp.sin(ai * dt)
    den = ar * ar + ai * ai
    zr = ((lr - 1.0) * ar + li * ai) / den
    zi = (li * ar - (lr - 1.0) * ai) / den
    lr_ref[...] = lr
    li_ref[...] = li
    br, bi = br_ref[...], bi_ref[...]
    zr3, zi3 = zr[:, None, :], zi[:, None, :]
    bbr_ref[...] = zr3 * br - zi3 * bi
    bbi_ref[...] = zr3 * bi + zi3 * br


def _s5_params(a_re, a_im, log_dt, bt_re, bt_im):
    G, N = a_re.shape
    C = bt_re.shape[1]
    return pl.pallas_call(
        _s5_param_kernel,
        out_shape=[jax.ShapeDtypeStruct((G, N), F32), jax.ShapeDtypeStruct((G, N), F32),
                   jax.ShapeDtypeStruct((G, C, N), F32), jax.ShapeDtypeStruct((G, C, N), F32)],
        name="s5_params",
    )(a_re, a_im, log_dt, bt_re, bt_im)


def _s5_kernel(u_ref, h0r_ref, h0i_ref, lr_ref, li_ref, wb_ref, wcr_ref, wci_ref, d_ref, wg_ref, bg_ref,
               o_ref, sr_ref, si_ref, bur, bui, y_scr, cr, ci, *, nseq, slen):
    c_id = pl.program_id(1)
    R = u_ref.shape[0]
    u = u_ref[...]
    ub = u.astype(BF16)
    for gc in range(S5_CHUNKS):
        r = _dot(ub[:, gc * S5_CHUNK_IN:(gc + 1) * S5_CHUNK_IN], wb_ref[gc])
        for lb in range(S5_LANE_BLOCKS):
            bur[lb, gc * R:(gc + 1) * R, :] = r[:, lb * LANES:(lb + 1) * LANES]
            bui[lb, gc * R:(gc + 1) * R, :] = r[:, S5_CHUNK_STATE + lb * LANES:S5_CHUNK_STATE + (lb + 1) * LANES]

    @pl.when(c_id == 0)
    def _():
        cr[...] = h0r_ref[...]
        ci[...] = h0i_ref[...]

    lr = lr_ref[...]
    li = li_ref[...]

    def seq_body(q, _):
        base = q * slen

        def step(t, hc):
            hr, hi = hc
            rows = pl.ds(base + t, S5_CHUNKS, stride=R)
            br = jnp.concatenate([bur[lb, rows, :] for lb in range(S5_LANE_BLOCKS)], axis=1)
            bi = jnp.concatenate([bui[lb, rows, :] for lb in range(S5_LANE_BLOCKS)], axis=1)
            nr = lr * hr - li * hi + br
            ni = lr * hi + li * hr + bi
            for lb in range(S5_LANE_BLOCKS):
                bur[lb, rows, :] = nr[:, lb * LANES:(lb + 1) * LANES]
                bui[lb, rows, :] = ni[:, lb * LANES:(lb + 1) * LANES]
            return nr, ni

        hr, hi = lax.fori_loop(0, slen, step, (cr[q], ci[q]), unroll=4)
        cr[q] = hr
        ci[q] = hi
        return 0

    lax.fori_loop(0, nseq, seq_body, 0)
    sr_ref[...] = cr[...]
    si_ref[...] = ci[...]

    for gc in range(S5_CHUNKS):
        hrb = jnp.concatenate([bur[lb, gc * R:(gc + 1) * R, :] for lb in range(S5_LANE_BLOCKS)], axis=1).astype(BF16)
        hib = jnp.concatenate([bui[lb, gc * R:(gc + 1) * R, :] for lb in range(S5_LANE_BLOCKS)], axis=1).astype(BF16)
        y_scr[:, gc * S5_CHUNK_IN:(gc + 1) * S5_CHUNK_IN] = _dot(hrb, wcr_ref[gc]) - _dot(hib, wci_ref[gc])
    y = y_scr[...] + d_ref[...] * u
    z = jax.nn.gelu(y)
    o_ref[...] = (z * jax.nn.sigmoid(_dot(z.astype(BF16), wg_ref[...]) + bg_ref[...])).astype(o_ref.dtype)


def _s5(h, col_block, h0r, h0i, lam_r, lam_i, wb, wcr, wci, d_skip, w_glu, b_glu, *, nseq, slen, chunks):
    T = h.shape[0]
    W = S5_GROUPS * S5_GROUP
    R = nseq * slen
    nsb = T // (R * chunks)
    n_seq_total = h0r.shape[0]
    st_spec = pl.BlockSpec((nseq, S5_CHUNKS, S5_CHUNK_STATE), lambda sb, c: (sb, 0, 0))
    full3 = lambda sb, c: (0, 0, 0)
    full2 = lambda sb, c: (0, 0)
    kern = functools.partial(_s5_kernel, nseq=nseq, slen=slen)
    return pl.pallas_call(
        kern,
        grid=(nsb, chunks),
        in_specs=[pl.BlockSpec((R, W), lambda sb, c: (sb * chunks + c, col_block)),
                  st_spec, st_spec,
                  pl.BlockSpec((S5_CHUNKS, S5_CHUNK_STATE), full2),
                  pl.BlockSpec((S5_CHUNKS, S5_CHUNK_STATE), full2),
                  pl.BlockSpec((S5_CHUNKS, S5_CHUNK_IN, 2 * S5_CHUNK_STATE), full3),
                  pl.BlockSpec((S5_CHUNKS, S5_CHUNK_STATE, S5_CHUNK_IN), full3),
                  pl.BlockSpec((S5_CHUNKS, S5_CHUNK_STATE, S5_CHUNK_IN), full3),
                  pl.BlockSpec((1, W), full2),
                  pl.BlockSpec((W, W), full2),
                  pl.BlockSpec((1, W), full2)],
        out_specs=[pl.BlockSpec((R, W), lambda sb, c: (sb * chunks + c, 0)), st_spec, st_spec],
        out_shape=[jax.ShapeDtypeStruct((T, W), BF16),
                   jax.ShapeDtypeStruct((n_seq_total, S5_CHUNKS, S5_CHUNK_STATE), F32),
                   jax.ShapeDtypeStruct((n_seq_total, S5_CHUNKS, S5_CHUNK_STATE), F32)],
        scratch_shapes=[pltpu.VMEM((S5_LANE_BLOCKS, S5_CHUNKS * R, LANES), F32),
                        pltpu.VMEM((S5_LANE_BLOCKS, S5_CHUNKS * R, LANES), F32),
                        pltpu.VMEM((R, W), F32),
                        pltpu.VMEM((nseq, S5_CHUNKS, S5_CHUNK_STATE), F32),
                        pltpu.VMEM((nseq, S5_CHUNKS, S5_CHUNK_STATE), F32)],
        compiler_params=_cp(("arbitrary", "arbitrary")),
        name="s5",
    )(h, h0r, h0i, lam_r, lam_i, wb, wcr, wci, d_skip, w_glu, b_glu)


def _mix_gate_kernel(att_ref, ssm_ref, ga_ref, gb_ref, pa_ref, pb_ref, o_ref):
    m = (jax.nn.sigmoid(ga_ref[...]) * _dot(att_ref[...], pa_ref[...])
         + jax.nn.sigmoid(gb_ref[...]) * _dot(ssm_ref[...], pb_ref[...]))
    o_ref[...] = m.astype(o_ref.dtype)


def _mix_gate(att, ssm, h, p_a, p_b):
    T, W = att.shape
    D = p_a.shape[1]
    tm = min(256, T)
    return pl.pallas_call(
        _mix_gate_kernel,
        grid=(T // tm,),
        in_specs=[pl.BlockSpec((tm, W), lambda i: (i, 0)),
                  pl.BlockSpec((tm, W), lambda i: (i, 0)),
                  pl.BlockSpec((tm, D), lambda i: (i, 2)),
                  pl.BlockSpec((tm, D), lambda i: (i, 3)),
                  pl.BlockSpec((W, D), lambda i: (0, 0)),
                  pl.BlockSpec((W, D), lambda i: (0, 0))],
        out_specs=pl.BlockSpec((tm, D), lambda i: (i, 0)),
        out_shape=jax.ShapeDtypeStruct((T, D), BF16),
        compiler_params=_cp(("parallel",)),
        name="mix_gate",
    )(att, ssm, h, h, p_a, p_b)


def _mix_out_kernel(m_ref, x_ref, g1_ref, sh2_ref, sc2_ref, wo_ref, lg_ref, lb_ref, wr_ref, br_ref,
                    x1_ref, u2_ref, ri_ref, rw_ref):
    out = _dot(m_ref[...], wo_ref[...])
    x1 = _layernorm(DN_ALPHA * x_ref[...] + g1_ref[...] * out, lg_ref[...], lb_ref[...])
    x1_ref[...] = x1
    u2 = x1 * (1.0 + sc2_ref[...]) + sh2_ref[...]
    u2_ref[...] = u2
    logits = jnp.dot(u2, wr_ref[...], preferred_element_type=F32,
                     precision=lax.Precision.HIGHEST) + br_ref[...]
    lane = lax.broadcasted_iota(jnp.int32, logits.shape, 1)
    cur = logits
    vals, idxs = [], []
    for _ in range(TOP_K):
        mk = jnp.max(cur, axis=1, keepdims=True)
        ik = jnp.min(jnp.where(cur == mk, lane, LANES), axis=1, keepdims=True)
        vals.append(mk)
        idxs.append(ik)
        cur = jnp.where(lane == ik, -jnp.inf, cur)
    es = [jnp.exp(v - vals[0]) for v in vals]
    den = es[0] + es[1] + es[2] + es[3]
    ri = jnp.zeros(logits.shape, jnp.int32)
    rw = jnp.zeros(logits.shape, F32)
    for k in range(TOP_K):
        ri = jnp.where(lane == k, idxs[k], ri)
        rw = jnp.where(lane == k, es[k] / den, rw)
    ri_ref[...] = ri
    rw_ref[...] = rw


def _mix_out(m, x, g1, sh2, sc2, w_o, ln_g, ln_b, w_r, b_r, *, seq_len):
    T, D = x.shape
    tm = min(256, T)
    if seq_len > 0:
        seq_tiles = seq_len // tm
        mod_spec = pl.BlockSpec((None, 1, D), lambda i: (i // seq_tiles, 0, 0))
    else:
        mod_spec = pl.BlockSpec((tm, D), lambda i: (i, 0))
    row = pl.BlockSpec((tm, D), lambda i: (i, 0))
    vec = pl.BlockSpec((1, D), lambda i: (0, 0))
    small = pl.BlockSpec((tm, LANES), lambda i: (i, 0))
    return pl.pallas_call(
        _mix_out_kernel,
        grid=(T // tm,),
        in_specs=[row, row, mod_spec, mod_spec, mod_spec,
                  pl.BlockSpec((D, D), lambda i: (0, 0)), vec, vec,
                  pl.BlockSpec((D, LANES), lambda i: (0, 0)),
                  pl.BlockSpec((1, LANES), lambda i: (0, 0))],
        out_specs=[row, row, small, small],
        out_shape=[jax.ShapeDtypeStruct((T, D), F32), jax.ShapeDtypeStruct((T, D), F32),
                   jax.ShapeDtypeStruct((T, LANES), jnp.int32), jax.ShapeDtypeStruct((T, LANES), F32)],
        compiler_params=_cp(("parallel",)),
        name="mix_out",
    )(m, x, g1, sh2, sc2, w_o, ln_g, ln_b, w_r, b_r)


def _gather_kernel(nrows_ref, idx_ref, x_hbm, o_ref, buf, sem):
    i = pl.program_id(0)
    gb = buf.shape[0]

    @pl.when(i * gb < nrows_ref[0])
    def _():
        def issue(r, _):
            tok = idx_ref[0, 0, r]
            pltpu.make_async_copy(x_hbm.at[pl.ds(tok, 1), :], buf.at[pl.ds(r, 1), :], sem).start()
            return 0

        lax.fori_loop(0, gb, issue, 0, unroll=8)
        pltpu.make_async_copy(x_hbm.at[pl.ds(0, gb), :], buf, sem).wait()
        o_ref[...] = buf[...].astype(o_ref.dtype)


def _moe_gather(x, idx, nrows):
    T, D = x.shape
    P = idx.shape[0]
    gb = MOE_SUB
    nb = P // gb

    def blk(i, nr):
        return (jnp.minimum(i, (nr[0] - 1) // gb), 0)

    grid_spec = pltpu.PrefetchScalarGridSpec(
        num_scalar_prefetch=1,
        grid=(nb,),
        in_specs=[pl.BlockSpec((1, 1, gb), lambda i, nr: (i, 0, 0), memory_space=pltpu.SMEM),
                  pl.BlockSpec(memory_space=pl.ANY)],
        out_specs=pl.BlockSpec((gb, D), blk),
        scratch_shapes=[pltpu.VMEM((gb, D), F32), pltpu.SemaphoreType.DMA(())],
    )
    return pl.pallas_call(
        _gather_kernel,
        grid_spec=grid_spec,
        out_shape=jax.ShapeDtypeStruct((P, D), BF16),
        compiler_params=_cp(("arbitrary",)),
        name="moe_gather",
    )(nrows, idx.reshape(nb, 1, gb), x)


def _moe_kernel(be_ref, nu_ref, x_ref, wg_ref, wu_ref, wd_ref, bg_ref, bu_ref, bd_ref, y_ref, wgb, wub, wdb):
    i = pl.program_id(0)
    f = pl.program_id(1)

    @pl.when(i < nu_ref[0])
    def _():
        wgb[...] = wg_ref[...].astype(BF16)
        wub[...] = wu_ref[...].astype(BF16)
        wdb[...] = wd_ref[...].astype(BF16)
        for sbk in range(MOE_BLOCK // MOE_SUB):
            rows = pl.ds(sbk * MOE_SUB, MOE_SUB)
            xs = x_ref[rows, :]
            gate = jnp.minimum(_dot(xs, wgb[...]) + bg_ref[...], SWIGLU_LIMIT)
            up = jnp.clip(_dot(xs, wub[...]) + bu_ref[...], -SWIGLU_LIMIT, SWIGLU_LIMIT)
            a = ((up + 1.0) * (gate * jax.nn.sigmoid(SWIGLU_ALPHA * gate))).astype(BF16)
            part = _dot(a, wdb[...])

            @pl.when(f == 0)
            def _():
                y_ref[rows, :] = part + bd_ref[...]

            @pl.when(f > 0)
            def _():
                y_ref[rows, :] += part


def _moe_experts(xs, blk_e, nused, w_gu, b_gu, w_dn, b_dn):
    P, D = xs.shape
    F = w_dn.shape[2]
    NB = P // MOE_BLOCK
    NF = F // MOE_FT

    def eff(i, nu):
        return jnp.minimum(i, nu[0] - 1)

    def feff(i, f, nu):
        return jnp.where(i < nu[0], f, NF - 1)

    grid_spec = pltpu.PrefetchScalarGridSpec(
        num_scalar_prefetch=2,
        grid=(NB, NF),
        in_specs=[
            pl.BlockSpec((MOE_BLOCK, D), lambda i, f, be, nu: (eff(i, nu), 0)),
            pl.BlockSpec((None, None, D, MOE_FT), lambda i, f, be, nu: (0, be[eff(i, nu)], 0, feff(i, f, nu))),
            pl.BlockSpec((None, None, D, MOE_FT), lambda i, f, be, nu: (0, be[eff(i, nu)], 0, NF + feff(i, f, nu))),
            pl.BlockSpec((None, None, MOE_FT, D), lambda i, f, be, nu: (0, be[eff(i, nu)], feff(i, f, nu), 0)),
            pl.BlockSpec((None, 1, MOE_FT), lambda i, f, be, nu: (be[eff(i, nu)], 0, feff(i, f, nu))),
            pl.BlockSpec((None, 1, MOE_FT), lambda i, f, be, nu: (be[eff(i, nu)], 0, NF + feff(i, f, nu))),
            pl.BlockSpec((None, 1, D), lambda i, f, be, nu: (be[eff(i, nu)], 0, 0)),
        ],
        out_specs=pl.BlockSpec((MOE_BLOCK, D), lambda i, f, be, nu: (eff(i, nu), 0)),
        scratch_shapes=[pltpu.VMEM((D, MOE_FT), BF16), pltpu.VMEM((D, MOE_FT), BF16),
                        pltpu.VMEM((MOE_FT, D), BF16)],
    )
    return pl.pallas_call(
        _moe_kernel,
        grid_spec=grid_spec,
        out_shape=jax.ShapeDtypeStruct((P, D), F32),
        compiler_params=_cp(("arbitrary", "arbitrary")),
        name="moe_experts",
    )(blk_e, nused, xs, w_gu, w_gu, w_dn, b_gu, b_gu, b_dn)


def _combine_kernel(idx_ref, y_hbm, x1_ref, g2_ref, rw_ref, lg_ref, lb_ref, o_ref, buf, sem):
    tm = x1_ref.shape[0]

    def issue(r, _):
        for k in range(TOP_K):
            d = idx_ref[0, 0, k * tm + r]
            pltpu.make_async_copy(y_hbm.at[pl.ds(d, 1), :], buf.at[k, pl.ds(r, 1), :], sem).start()
        return 0

    lax.fori_loop(0, tm, issue, 0, unroll=2)
    for k in range(TOP_K):
        pltpu.make_async_copy(y_hbm.at[pl.ds(0, tm), :], buf.at[k], sem).wait()
    rw = rw_ref[...]
    moe = rw[:, 0:1] * buf[0]
    for k in range(1, TOP_K):
        moe = moe + rw[:, k:k + 1] * buf[k]
    o_ref[...] = _layernorm(DN_ALPHA * x1_ref[...] + g2_ref[...] * moe, lg_ref[...], lb_ref[...])


def _moe_combine(y, dest, x1, g2, rw, ln_g, ln_b, *, seq_len):
    T, D = x1.shape
    tm = 128
    nt = T // tm
    idx = dest.reshape(nt, tm, TOP_K).transpose(0, 2, 1).reshape(nt, 1, TOP_K * tm)
    if seq_len > 0:
        seq_tiles = seq_len // tm
        mod_spec = pl.BlockSpec((None, 1, D), lambda i: (i // seq_tiles, 0, 0))
    else:
        mod_spec = pl.BlockSpec((tm, D), lambda i: (i, 0))
    row = pl.BlockSpec((tm, D), lambda i: (i, 0))
    vec = pl.BlockSpec((1, D), lambda i: (0, 0))
    return pl.pallas_call(
        _combine_kernel,
        grid=(nt,),
        in_specs=[pl.BlockSpec((1, 1, TOP_K * tm), lambda i: (i, 0, 0), memory_space=pltpu.SMEM),
                  pl.BlockSpec(memory_space=pl.ANY),
                  row, mod_spec, pl.BlockSpec((tm, LANES), lambda i: (i, 0)), vec, vec],
        out_specs=row,
        out_shape=jax.ShapeDtypeStruct((T, D), F32),
        scratch_shapes=[pltpu.VMEM((TOP_K, tm, D), F32), pltpu.SemaphoreType.DMA(())],
        compiler_params=_cp(("arbitrary",)),
        name="moe_combine",
    )(idx, y, x1, g2, rw, ln_g, ln_b)


def _route(experts):
    T = experts.shape[0]
    A = T * TOP_K
    flat_e = experts.reshape(A)
    onehot = (flat_e[:, None] == jnp.arange(N_EXPERTS, dtype=jnp.int32)[None, :]).astype(jnp.int32)
    counts = jnp.sum(onehot, axis=0)
    rank = jnp.sum((jnp.cumsum(onehot, axis=0) - onehot) * onehot, axis=1)
    nblk = (counts + MOE_BLOCK - 1) // MOE_BLOCK
    blk_end = jnp.cumsum(nblk)
    blk_start = blk_end - nblk
    dest = blk_start[flat_e] * MOE_BLOCK + rank
    NB = -(-A // MOE_BLOCK) + N_EXPERTS
    idx = jnp.zeros((NB * MOE_BLOCK,), jnp.int32).at[dest].set(jnp.arange(A, dtype=jnp.int32) // TOP_K)
    blk_e = jnp.minimum(jnp.searchsorted(blk_end, jnp.arange(NB, dtype=jnp.int32), side='right'),
                        N_EXPERTS - 1).astype(jnp.int32)
    nused = blk_end[-1:].astype(jnp.int32)
    return dest.reshape(T, TOP_K), idx, blk_e, nused


def _block_diag_in(bb):
    gpc = S5_GROUPS // S5_CHUNKS
    x = bb.reshape(S5_CHUNKS, gpc, S5_GROUP, S5_STATE)
    eye = jnp.eye(gpc, dtype=bb.dtype)
    return jnp.einsum('agcn,gh->agchn', x, eye).reshape(S5_CHUNKS, S5_CHUNK_IN, S5_CHUNK_STATE)


def _block_diag_out(cc):
    gpc = S5_GROUPS // S5_CHUNKS
    x = cc.reshape(S5_CHUNKS, gpc, S5_GROUP, S5_STATE)
    eye = jnp.eye(gpc, dtype=cc.dtype)
    return jnp.einsum('agcn,gh->ahngc', x, eye).reshape(S5_CHUNKS, S5_CHUNK_STATE, S5_CHUNK_IN)


def kernel(x_prompt, x_sample, c_prompt, c_sample, cache_k, cache_v, cache_logf, state_s5_re, state_s5_im, page_table, w_ada, b_ada, w_in, fox_b_f, s5_a_re, s5_a_im, s5_log_dt, s5_b_re, s5_b_im, s5_c_re, s5_c_im, s5_d, w_glu, b_glu, p_a, p_b, w_o, ln1_g, ln1_b, w_router, b_router, w_gu, b_gu, w_dn, b_dn, ln2_g, ln2_b):
    B, L, D = x_prompt.shape
    Bd, Ld, _ = x_sample.shape
    Tp, Ts = B * L, Bd * Ld
    W = HEADS * HEAD_DIM
    n_phys = cache_k.shape[1]

    n_c = B + Bd
    c_all = jnp.concatenate([c_prompt, c_sample, jnp.zeros((-n_c % 8, D), F32)], axis=0)
    mod = _adaln(c_all, w_ada[0], b_ada)
    mp = [mod[:B, i * D:(i + 1) * D].reshape(B, 1, D) for i in range(6)]
    ms = [jnp.repeat(mod[B:n_c, i * D:(i + 1) * D], Ld, axis=0) for i in range(6)]

    wi = w_in[0]
    f_lo = 3 * W
    wm = jnp.concatenate([wi[:, :f_lo], wi[:, f_lo + HEADS:]], axis=1).astype(BF16)
    wf = jnp.pad(wi[:, f_lo:f_lo + HEADS], ((0, 0), (0, LANES - HEADS))).astype(BF16)
    bf = jnp.pad(fox_b_f, ((0, 0), (0, LANES - HEADS)))
    hp, logf_p, cT_p = _in_proj(x_prompt.reshape(Tp, D), mp[0], mp[1], wm, wf, wf.T, bf, bf.T,
                                seq_len=L, seg=0)
    hs, logf_s, c_s = _in_proj(x_sample.reshape(Ts, D), ms[0], ms[1], wm, wf, wf.T, bf, bf.T,
                               seq_len=0, seg=Ld)

    lam_r, lam_i, bb_r, bb_i = _s5_params(s5_a_re[0], s5_a_im[0], s5_log_dt[0][:, None],
                                          s5_b_re[0].transpose(0, 2, 1), s5_b_im[0].transpose(0, 2, 1))
    lam_r = lam_r.reshape(S5_CHUNKS, S5_CHUNK_STATE)
    lam_i = lam_i.reshape(S5_CHUNKS, S5_CHUNK_STATE)
    wb = jnp.concatenate([_block_diag_in(bb_r), _block_diag_in(bb_i)], axis=2).astype(BF16)
    wcr = _block_diag_out(s5_c_re[0]).astype(BF16)
    wci = _block_diag_out(s5_c_im[0]).astype(BF16)
    wgl = w_glu[0].astype(BF16)
    s5_args = (lam_r, lam_i, wb, wcr, wci, s5_d, wgl, b_glu)
    zeros_state = jnp.zeros((B, S5_CHUNKS, S5_CHUNK_STATE), F32)
    s5_t = 256
    ssm_p, sr_p, si_p = _s5(hp, 3, zeros_state, zeros_state, *s5_args, nseq=1, slen=s5_t, chunks=L // s5_t)
    ssm_s, sr_s, si_s = _s5(hs, 3, state_s5_re[0].reshape(Bd, S5_CHUNKS, S5_CHUNK_STATE),
                            state_s5_im[0].reshape(Bd, S5_CHUNKS, S5_CHUNK_STATE), *s5_args,
                            nseq=32, slen=Ld, chunks=1)

    att_p = _fox_prompt(hp, cT_p, B, L)
    scale = HEAD_DIM ** -0.5
    q_s = hs[:, :W].reshape(Bd, Ld, HEADS, HEAD_DIM) * scale
    k_s = hs[:, W:2 * W].reshape(Bd, Ld, HEADS, HEAD_DIM)
    v_s = hs[:, 2 * W:3 * W].reshape(Bd, Ld, HEADS, HEAD_DIM)
    qe = q_s.transpose(0, 2, 1, 3).reshape(Bd, HEADS * Ld, HEAD_DIM)
    kn = jnp.broadcast_to(k_s[:, :, :, None, :], (Bd, Ld, HEADS, Ld, HEAD_DIM)).reshape(Bd, Ld, HEADS * Ld, HEAD_DIM)
    vn = jnp.broadcast_to(v_s[:, :, :, None, :], (Bd, Ld, HEADS, Ld, HEAD_DIM)).reshape(Bd, Ld, HEADS * Ld, HEAD_DIM)
    cn = c_s[:, :HEADS].reshape(Bd, Ld, HEADS)
    bn = jnp.broadcast_to(cn.transpose(0, 2, 1)[:, :, None, :], (Bd, HEADS, Ld, Ld)).reshape(Bd, HEADS * Ld, Ld)
    bn = jnp.pad(bn, ((0, 0), (0, 0), (0, LANES - Ld)))
    att_s = _fox_sample(page_table, cache_k[0].reshape(n_phys, PAGE * HEADS, HEAD_DIM),
                        cache_v[0].reshape(n_phys, PAGE * HEADS, HEAD_DIM),
                        cache_logf[0].reshape(n_phys, 1, PAGE * HEADS), qe, kn, vn, bn)
    att_s = att_s.reshape(Bd, HEADS, Ld, HEAD_DIM).transpose(0, 2, 1, 3).reshape(Ts, W).astype(BF16)

    pab, pbb, wob = p_a[0].astype(BF16), p_b[0].astype(BF16), w_o[0].astype(BF16)
    wr = jnp.pad(w_router[0], ((0, 0), (0, LANES - N_EXPERTS)))
    br = jnp.pad(b_router, ((0, 0), (0, LANES - N_EXPERTS)), constant_values=NEG)
    m_p = _mix_gate(att_p, ssm_p, hp, pab, pbb)
    m_s = _mix_gate(att_s, ssm_s, hs, pab, pbb)
    x1_p, u2_p, ri_p, rw_p = _mix_out(m_p, x_prompt.reshape(Tp, D), mp[2], mp[3], mp[4], wob, ln1_g, ln1_b,
                                      wr, br, seq_len=L)
    x1_s, u2_s, ri_s, rw_s = _mix_out(m_s, x_sample.reshape(Ts, D), ms[2], ms[3], ms[4], wob, ln1_g, ln1_b,
                                      wr, br, seq_len=0)

    u2 = jnp.concatenate([u2_p, u2_s], axis=0)
    experts = jnp.concatenate([ri_p[:, :TOP_K], ri_s[:, :TOP_K]], axis=0)
    dest, idx, blk_e, nused = _route(experts)
    xs = _moe_gather(u2, idx, nused * MOE_BLOCK)
    y = _moe_experts(xs, blk_e, nused, w_gu, b_gu[0][:, None, :], w_dn, b_dn[0][:, None, :])
    x2_p = _moe_combine(y, dest[:Tp], x1_p, mp[5], rw_p, ln2_g, ln2_b, seq_len=L)
    x2_s = _moe_combine(y, dest[Tp:], x1_s, ms[5], rw_s, ln2_g, ln2_b, seq_len=0)

    def heads(t, n, l):
        return t.reshape(1, n, l, HEADS, HEAD_DIM)

    def state(s, n):
        return s.reshape(1, n, S5_GROUPS, S5_STATE)

    return (x2_p.reshape(B, L, D), x2_s.reshape(Bd, Ld, D),
            heads(hp[:, W:2 * W], B, L), heads(hp[:, 2 * W:3 * W], B, L),
            logf_p[:, :HEADS].reshape(1, B, L, HEADS), state(sr_p, B), state(si_p, B),
            heads(hs[:, W:2 * W], Bd, Ld), heads(hs[:, 2 * W:3 * W], Bd, Ld),
            logf_s[:, :HEADS].reshape(1, Bd, Ld, HEADS), state(sr_s, Bd), state(si_s, Bd))
```

```python
import functools
import math

import jax
import jax.numpy as jnp
from jax import lax
from jax.experimental import pallas as pl
from jax.experimental.pallas import tpu as pltpu

F32 = jnp.float32
BF16 = jnp.bfloat16

HEADS = 16
HEAD_DIM = 64
PAGE = 128
S5_GROUPS = 64
S5_GROUP = 16
S5_STATE = 64
N_EXPERTS = 32
TOP_K = 4
SWIGLU_LIMIT = 7.0
SWIGLU_ALPHA = 1.702
DN_ALPHA = 2.0 ** 0.25
LN_EPS = 1e-5
NEG = -1e30
LOG2E = math.log2(math.e)

LANES = 128
S5_CHUNKS = 8
S5_CHUNK_STATE = S5_GROUPS * S5_STATE // S5_CHUNKS
S5_CHUNK_IN = S5_GROUPS * S5_GROUP // S5_CHUNKS
S5_LANE_BLOCKS = S5_CHUNK_STATE // LANES
MOE_BLOCK = 1152
MOE_SUB = 384
MOE_FT = 512
VMEM_LIMIT = 56 * 1024 * 1024
MOE_VMEM_LIMIT = 60 * 1024 * 1024


def _cp(sem, vmem=VMEM_LIMIT):
    return pltpu.CompilerParams(dimension_semantics=sem, vmem_limit_bytes=vmem)


def _dot(a, b):
    return jnp.dot(a, b, preferred_element_type=F32)


def _dot_nt(a, b):
    return lax.dot_general(a, b, (((1,), (1,)), ((), ())), preferred_element_type=F32)


def _log_sigmoid(x):
    return jnp.minimum(x, 0.0) - jnp.log1p(jnp.exp(-jnp.abs(x)))


def _split3(v):
    hi = v.astype(BF16)
    r = v - hi.astype(F32)
    mid = r.astype(BF16)
    lo = (r - mid.astype(F32)).astype(BF16)
    return hi, mid, lo


def _layernorm(v, g, b):
    mu = jnp.mean(v, axis=-1, keepdims=True)
    d = v - mu
    var = jnp.mean(d * d, axis=-1, keepdims=True)
    return d * lax.rsqrt(var + LN_EPS) * g + b


def _ada_kernel(c_ref, w_ref, b_ref, o_ref):
    c = c_ref[...]
    a = (c * jax.nn.sigmoid(c)).astype(BF16)
    o_ref[...] = _dot(a, w_ref[...].astype(BF16)) + b_ref[...]


def _adaln(c_all, w_ada, b_ada):
    R, D = c_all.shape
    N = w_ada.shape[1]
    tn = 1024
    return pl.pallas_call(
        _ada_kernel,
        grid=(N // tn,),
        in_specs=[pl.BlockSpec((R, D), lambda j: (0, 0)),
                  pl.BlockSpec((D, tn), lambda j: (0, j)),
                  pl.BlockSpec((1, tn), lambda j: (0, j))],
        out_specs=pl.BlockSpec((R, tn), lambda j: (0, j)),
        out_shape=jax.ShapeDtypeStruct((R, N), F32),
        compiler_params=_cp(("arbitrary",)),
        name="adaln",
    )(c_all, w_ada, b_ada)


def _in_kernel(x_ref, sh_ref, sc_ref, wm_ref, wf_ref, wft_ref, bf_ref, bft_ref,
               h_ref, logf_ref, cum_ref, u_scr, carry_scr, *, seq_tiles, seg):
    i = pl.program_id(0)
    j = pl.program_id(1)
    tm = x_ref.shape[0]

    @pl.when(j == 0)
    def _():
        u = x_ref[...] * (1.0 + sc_ref[...]) + sh_ref[...]
        ub = u.astype(BF16)
        u_scr[...] = ub
        lf = _log_sigmoid(_dot(ub, wf_ref[...]) + bf_ref[...])
        logf_ref[...] = lf
        r = lax.broadcasted_iota(jnp.int32, (tm, tm), 0)
        c = lax.broadcasted_iota(jnp.int32, (tm, tm), 1)
        if seq_tiles > 0:
            lft = _log_sigmoid(_dot_nt(wft_ref[...], ub) + bft_ref[...])
            tri = (r <= c).astype(BF16)
            hi, mid, lo = _split3(lft)
            cs = _dot(hi, tri) + _dot(mid, tri) + _dot(lo, tri)

            @pl.when(i % seq_tiles == 0)
            def _():
                carry_scr[...] = jnp.zeros_like(carry_scr)

            cs = cs + carry_scr[...]
            cum_ref[...] = cs
            carry_scr[...] = cs[:, tm - 1:tm]
        else:
            tri = ((c <= r) & (c // seg == r // seg)).astype(BF16)
            hi, mid, lo = _split3(lf)
            cum_ref[...] = _dot(tri, hi) + _dot(tri, mid) + _dot(tri, lo)

    h_ref[...] = _dot(u_scr[...], wm_ref[...])


def _in_proj(x, shift, scale, wm, wf, wft, bf, bft, *, seq_len, seg):
    T, D = x.shape
    N = wm.shape[1]
    tm, tn = min(512, T), 1024
    nt = T // tm
    if seq_len > 0:
        seq_tiles = seq_len // tm
        mod_spec = pl.BlockSpec((None, 1, D), lambda i, j: (i // seq_tiles, 0, 0))
        cum_shape, cum_spec = (LANES, T), pl.BlockSpec((LANES, tm), lambda i, j: (0, i))
    else:
        seq_tiles = 0
        mod_spec = pl.BlockSpec((tm, D), lambda i, j: (i, 0))
        cum_shape, cum_spec = (T, LANES), pl.BlockSpec((tm, LANES), lambda i, j: (i, 0))
    kern = functools.partial(_in_kernel, seq_tiles=seq_tiles, seg=seg)
    return pl.pallas_call(
        kern,
        grid=(nt, N // tn),
        in_specs=[pl.BlockSpec((tm, D), lambda i, j: (i, 0)), mod_spec, mod_spec,
                  pl.BlockSpec((D, tn), lambda i, j: (0, j)),
                  pl.BlockSpec((D, LANES), lambda i, j: (0, 0)),
                  pl.BlockSpec((LANES, D), lambda i, j: (0, 0)),
                  pl.BlockSpec((1, LANES), lambda i, j: (0, 0)),
                  pl.BlockSpec((LANES, 1), lambda i, j: (0, 0))],
        out_specs=[pl.BlockSpec((tm, tn), lambda i, j: (i, j)),
                   pl.BlockSpec((tm, LANES), lambda i, j: (i, 0)),
                   cum_spec],
        out_shape=[jax.ShapeDtypeStruct((T, N), F32),
                   jax.ShapeDtypeStruct((T, LANES), F32),
                   jax.ShapeDtypeStruct(cum_shape, F32)],
        scratch_shapes=[pltpu.VMEM((tm, D), BF16), pltpu.VMEM((LANES, 1), F32)],
        compiler_params=_cp(("arbitrary", "arbitrary")),
        name="in_proj",
    )(x, shift, scale, wm, wf, wft, bf, bft)


def _fox_prompt_kernel(q_ref, k_ref, v_ref, c_ref, o_ref, qs, m_scr, l_scr, acc):
    hp = pl.program_id(1)
    qi = pl.program_id(2)
    kv = pl.program_id(3)
    tq, tk = q_ref.shape[0], k_ref.shape[0]
    lane = lax.broadcasted_iota(jnp.int32, (tq, LANES), 1)

    @pl.when(kv == 0)
    def _():
        q2 = q_ref[...] * (HEAD_DIM ** -0.5 * LOG2E)
        qs[0] = jnp.where(lane < HEAD_DIM, q2, 0.0).astype(BF16)
        qs[1] = jnp.where(lane >= HEAD_DIM, q2, 0.0).astype(BF16)
        m_scr[...] = jnp.full_like(m_scr, NEG)
        l_scr[...] = jnp.zeros_like(l_scr)
        acc[...] = jnp.zeros_like(acc)

    def block(diagonal):
        kb = k_ref[...].astype(BF16)
        vb = v_ref[...].astype(BF16)
        r0 = (2 * hp) % 8
        for a in range(2):
            crow = c_ref[pl.ds(r0 + a, 1), :] * LOG2E
            s = _dot_nt(qs[a], kb) - crow
            if diagonal:
                row = lax.broadcasted_iota(jnp.int32, (tq, tk), 0)
                col = lax.broadcasted_iota(jnp.int32, (tq, tk), 1)
                s = jnp.where(col <= row, s, NEG)
            m_prev = m_scr[a]
            m_new = jnp.maximum(m_prev, jnp.max(s, axis=1, keepdims=True))
            alpha = jnp.exp2(m_prev - m_new)
            p = jnp.exp2(s - m_new)
            l_scr[a] = alpha * l_scr[a] + jnp.sum(p, axis=1, keepdims=True)
            acc[a] = alpha * acc[a] + _dot(p.astype(BF16), vb)
            m_scr[a] = m_new

    @pl.when(kv < qi)
    def _():
        block(False)

    @pl.when(kv == qi)
    def _():
        block(True)
        o = jnp.where(lane < HEAD_DIM, acc[0] / l_scr[0], acc[1] / l_scr[1])
        o_ref[...] = o.astype(o_ref.dtype)


def _fox_prompt(h, cT, B, L):
    tq = tk = 512
    nq, nk = L // tq, L // tk
    npairs = HEADS // 2
    return pl.pallas_call(
        _fox_prompt_kernel,
        grid=(B, npairs, nq, nk),
        in_specs=[
            pl.BlockSpec((tq, LANES), lambda b, hp, qi, kv: (b * nq + qi, hp)),
            pl.BlockSpec((tk, LANES), lambda b, hp, qi, kv: (b * nk + jnp.minimum(kv, qi), npairs + hp)),
            pl.BlockSpec((tk, LANES), lambda b, hp, qi, kv: (b * nk + jnp.minimum(kv, qi), 2 * npairs + hp)),
            pl.BlockSpec((8, tk), lambda b, hp, qi, kv: (hp // 4, b * nk + jnp.minimum(kv, qi))),
        ],
        out_specs=pl.BlockSpec((tq, LANES), lambda b, hp, qi, kv: (b * nq + qi, hp)),
        out_shape=jax.ShapeDtypeStruct((B * L, HEADS * HEAD_DIM), BF16),
        scratch_shapes=[pltpu.VMEM((2, tq, LANES), BF16), pltpu.VMEM((2, tq, 1), F32),
                        pltpu.VMEM((2, tq, 1), F32), pltpu.VMEM((2, tq, LANES), F32)],
        compiler_params=_cp(("parallel", "parallel", "arbitrary", "arbitrary")),
        name="fox_prompt",
    )(h, h, h, cT)


def _fox_sample_kernel(pt_ref, *refs, n_pages):
    k_refs = refs[:n_pages]
    v_refs = refs[n_pages:2 * n_pages]
    lf_refs = refs[2 * n_pages:3 * n_pages]
    qbd_ref, qe_ref, kn_ref, vn_ref, bn_ref, o_ref = refs[3 * n_pages:]
    nq = qe_ref.shape[0] // HEADS
    nrow = nq * HEADS
    width = HEADS * HEAD_DIM

    xs = jnp.concatenate([r[...] for r in lf_refs], axis=0)
    hi, mid, lo = _split3(xs)
    after = (lax.broadcasted_iota(jnp.int32, (PAGE, PAGE), 0)
             > lax.broadcasted_iota(jnp.int32, (PAGE, PAGE), 1)).astype(BF16)
    within = _dot(hi, after) + _dot(mid, after) + _dot(lo, after)
    n = n_pages * HEADS
    pr = lax.broadcasted_iota(jnp.int32, (n, n), 0)
    pc = lax.broadcasted_iota(jnp.int32, (n, n), 1)
    later = ((pc // HEADS > pr // HEADS) & (pc % HEADS == pr % HEADS)).astype(BF16)
    beyond = jnp.sum(_dot(later, hi) + _dot(later, mid) + _dot(later, lo), axis=1, keepdims=True)
    bias_rows = within + beyond
    bias = jnp.concatenate([bias_rows[j * HEADS:(j + 1) * HEADS] for j in range(n_pages)], axis=1)
    bias = jnp.concatenate([bias] * nq, axis=0)

    kt = jnp.concatenate([r[...].astype(BF16) for r in k_refs], axis=1)
    s = _dot(qbd_ref[...], kt) + bias

    q_of_row = lax.broadcasted_iota(jnp.int32, (nrow, 1), 0) // HEADS
    qe = qe_ref[...]
    bn = bn_ref[...]
    sn = []
    for j in range(nq):
        sj = jnp.sum(qe * kn_ref[j], axis=1, keepdims=True) - bn[:, j:j + 1]
        sn.append(jnp.where(q_of_row >= j, sj, NEG))
    m = jnp.max(s, axis=1, keepdims=True)
    for sj in sn:
        m = jnp.maximum(m, sj)
    p = jnp.exp(s - m)
    l = jnp.sum(p, axis=1, keepdims=True)

    vt = jnp.concatenate([r[...].astype(BF16) for r in v_refs], axis=1)
    o_all = _dot_nt(p.astype(BF16), vt)
    rh = lax.broadcasted_iota(jnp.int32, (nrow, width), 0) % HEADS
    ch = lax.broadcasted_iota(jnp.int32, (nrow, width), 1) // HEAD_DIM
    o_hi, o_mid, o_lo = _split3(jnp.where(rh == ch, o_all, 0.0))
    fold = (lax.broadcasted_iota(jnp.int32, (width, HEAD_DIM), 0) % HEAD_DIM
            == lax.broadcasted_iota(jnp.int32, (width, HEAD_DIM), 1)).astype(BF16)
    o = _dot(o_hi, fold) + _dot(o_mid, fold) + _dot(o_lo, fold)
    for j in range(nq):
        pj = jnp.exp(sn[j] - m)
        l = l + pj
        o = o + pj * vn_ref[j]
    o_ref[...] = o / l


def _fox_sample(page_table, ckt, cvt, clft, qbd, qe, kn, vn, bn):
    Bd, n_pages = page_table.shape
    nrow = qe.shape[1]
    nq = nrow // HEADS
    width = HEADS * HEAD_DIM

    def page_map(j):
        return lambda b, pt: (pt[b * n_pages + j], 0, 0)

    per_row3 = lambda b, pt: (b, 0, 0)
    per_row4 = lambda b, pt: (b, 0, 0, 0)
    grid_spec = pltpu.PrefetchScalarGridSpec(
        num_scalar_prefetch=1,
        grid=(Bd,),
        in_specs=([pl.BlockSpec((None, width, PAGE), page_map(j)) for j in range(n_pages)]
                  + [pl.BlockSpec((None, width, PAGE), page_map(j)) for j in range(n_pages)]
                  + [pl.BlockSpec((None, HEADS, PAGE), page_map(j)) for j in range(n_pages)]
                  + [pl.BlockSpec((None, nrow, width), per_row3),
                     pl.BlockSpec((None, nrow, HEAD_DIM), per_row3),
                     pl.BlockSpec((None, nq, nrow, HEAD_DIM), per_row4),
                     pl.BlockSpec((None, nq, nrow, HEAD_DIM), per_row4),
                     pl.BlockSpec((None, nrow, LANES), per_row3)]),
        out_specs=pl.BlockSpec((None, nrow, HEAD_DIM), per_row3),
    )
    return pl.pallas_call(
        functools.partial(_fox_sample_kernel, n_pages=n_pages),
        grid_spec=grid_spec,
        out_shape=jax.ShapeDtypeStruct((Bd, nrow, HEAD_DIM), F32),
        compiler_params=_cp(("parallel",)),
        name="fox_sample",
    )(page_table.reshape(-1), *([ckt] * n_pages), *([cvt] * n_pages), *([clft] * n_pages),
      qbd, qe, kn, vn, bn)


def _s5_param_kernel(ar_ref, ai_ref, ldt_ref, br_ref, bi_ref, lr_ref, li_ref, bbr_ref, bbi_ref):
    ar, ai = ar_ref[...], ai_ref[...]
    dt = jnp.exp(ldt_ref[...])
    mag = jnp.exp(ar * dt)
    lr = mag * jnp.cos(ai * dt)
    li = mag * jnp.sin(ai * dt)
    den = ar * ar + ai * ai
    zr = ((lr - 1.0) * ar + li * ai) / den
    zi = (li * ar - (lr - 1.0) * ai) / den
    lr_ref[...] = lr
    li_ref[...] = li
    br, bi = br_ref[...], bi_ref[...]
    zr3, zi3 = zr[:, None, :], zi[:, None, :]
    bbr_ref[...] = zr3 * br - zi3 * bi
    bbi_ref[...] = zr3 * bi + zi3 * br


def _s5_params(a_re, a_im, log_dt, bt_re, bt_im):
    G, N = a_re.shape
    C = bt_re.shape[1]
    return pl.pallas_call(
        _s5_param_kernel,
        out_shape=[jax.ShapeDtypeStruct((G, N), F32), jax.ShapeDtypeStruct((G, N), F32),
                   jax.ShapeDtypeStruct((G, C, N), F32), jax.ShapeDtypeStruct((G, C, N), F32)],
        name="s5_params",
    )(a_re, a_im, log_dt, bt_re, bt_im)


def _s5_kernel(u_ref, h0r_ref, h0i_ref, lr_ref, li_ref, wb_ref, wcr_ref, wci_ref, d_ref, wg_ref, bg_ref,
               o_ref, sr_ref, si_ref, bur, bui, y_scr, cr, ci, *, nseq, slen):
    c_id = pl.program_id(1)
    R = u_ref.shape[0]
    u = u_ref[...]
    ub = u.astype(BF16)
    for gc in range(S5_CHUNKS):
        r = _dot(ub[:, gc * S5_CHUNK_IN:(gc + 1) * S5_CHUNK_IN], wb_ref[gc])
        for lb in range(S5_LANE_BLOCKS):
            bur[lb, gc * R:(gc + 1) * R, :] = r[:, lb * LANES:(lb + 1) * LANES]
            bui[lb, gc * R:(gc + 1) * R, :] = r[:, S5_CHUNK_STATE + lb * LANES:S5_CHUNK_STATE + (lb + 1) * LANES]

    @pl.when(c_id == 0)
    def _():
        cr[...] = h0r_ref[...]
        ci[...] = h0i_ref[...]

    lr = lr_ref[...]
    li = li_ref[...]

    def seq_body(q, _):
        base = q * slen

        def step(t, hc):
            hr, hi = hc
            rows = pl.ds(base + t, S5_CHUNKS, stride=R)
            br = jnp.concatenate([bur[lb, rows, :] for lb in range(S5_LANE_BLOCKS)], axis=1)
            bi = jnp.concatenate([bui[lb, rows, :] for lb in range(S5_LANE_BLOCKS)], axis=1)
            nr = lr * hr - li * hi + br
            ni = lr * hi + li * hr + bi
            for lb in range(S5_LANE_BLOCKS):
                bur[lb, rows, :] = nr[:, lb * LANES:(lb + 1) * LANES]
                bui[lb, rows, :] = ni[:, lb * LANES:(lb + 1) * LANES]
            return nr, ni

        hr, hi = lax.fori_loop(0, slen, step, (cr[q], ci[q]), unroll=4)
        cr[q] = hr
        ci[q] = hi
        return 0

    lax.fori_loop(0, nseq, seq_body, 0)
    sr_ref[...] = cr[...]
    si_ref[...] = ci[...]

    for gc in range(S5_CHUNKS):
        hrb = jnp.concatenate([bur[lb, gc * R:(gc + 1) * R, :] for lb in range(S5_LANE_BLOCKS)], axis=1).astype(BF16)
        hib = jnp.concatenate([bui[lb, gc * R:(gc + 1) * R, :] for lb in range(S5_LANE_BLOCKS)], axis=1).astype(BF16)
        y_scr[:, gc * S5_CHUNK_IN:(gc + 1) * S5_CHUNK_IN] = _dot(hrb, wcr_ref[gc]) - _dot(hib, wci_ref[gc])
    y = y_scr[...] + d_ref[...] * u
    z = jax.nn.gelu(y)
    o_ref[...] = (z * jax.nn.sigmoid(_dot(z.astype(BF16), wg_ref[...]) + bg_ref[...])).astype(o_ref.dtype)


def _s5(h, col_block, h0r, h0i, lam_r, lam_i, wb, wcr, wci, d_skip, w_glu, b_glu, *, nseq, slen, chunks):
    T = h.shape[0]
    W = S5_GROUPS * S5_GROUP
    R = nseq * slen
    nsb = T // (R * chunks)
    n_seq_total = h0r.shape[0]
    st_spec = pl.BlockSpec((nseq, S5_CHUNKS, S5_CHUNK_STATE), lambda sb, c: (sb, 0, 0))
    full3 = lambda sb, c: (0, 0, 0)
    full2 = lambda sb, c: (0, 0)
    kern = functools.partial(_s5_kernel, nseq=nseq, slen=slen)
    return pl.pallas_call(
        kern,
        grid=(nsb, chunks),
        in_specs=[pl.BlockSpec((R, W), lambda sb, c: (sb * chunks + c, col_block)),
                  st_spec, st_spec,
                  pl.BlockSpec((S5_CHUNKS, S5_CHUNK_STATE), full2),
                  pl.BlockSpec((S5_CHUNKS, S5_CHUNK_STATE), full2),
                  pl.BlockSpec((S5_CHUNKS, S5_CHUNK_IN, 2 * S5_CHUNK_STATE), full3),
                  pl.BlockSpec((S5_CHUNKS, S5_CHUNK_STATE, S5_CHUNK_IN), full3),
                  pl.BlockSpec((S5_CHUNKS, S5_CHUNK_STATE, S5_CHUNK_IN), full3),
                  pl.BlockSpec((1, W), full2),
                  pl.BlockSpec((W, W), full2),
                  pl.BlockSpec((1, W), full2)],
        out_specs=[pl.BlockSpec((R, W), lambda sb, c: (sb * chunks + c, 0)), st_spec, st_spec],
        out_shape=[jax.ShapeDtypeStruct((T, W), BF16),
                   jax.ShapeDtypeStruct((n_seq_total, S5_CHUNKS, S5_CHUNK_STATE), F32),
                   jax.ShapeDtypeStruct((n_seq_total, S5_CHUNKS, S5_CHUNK_STATE), F32)],
        scratch_shapes=[pltpu.VMEM((S5_LANE_BLOCKS, S5_CHUNKS * R, LANES), F32),
                        pltpu.VMEM((S5_LANE_BLOCKS, S5_CHUNKS * R, LANES), F32),
                        pltpu.VMEM((R, W), F32),
                        pltpu.VMEM((nseq, S5_CHUNKS, S5_CHUNK_STATE), F32),
                        pltpu.VMEM((nseq, S5_CHUNKS, S5_CHUNK_STATE), F32)],
        compiler_params=_cp(("arbitrary", "arbitrary")),
        name="s5",
    )(h, h0r, h0i, lam_r, lam_i, wb, wcr, wci, d_skip, w_glu, b_glu)


def _mix_gate_kernel(att_ref, ssm_ref, ga_ref, gb_ref, pa_ref, pb_ref, o_ref):
    m = (jax.nn.sigmoid(ga_ref[...]) * _dot(att_ref[...], pa_ref[...])
         + jax.nn.sigmoid(gb_ref[...]) * _dot(ssm_ref[...], pb_ref[...]))
    o_ref[...] = m.astype(o_ref.dtype)


def _mix_gate(att, ssm, h, p_a, p_b):
    T, W = att.shape
    D = p_a.shape[1]
    tm = min(256, T)
    return pl.pallas_call(
        _mix_gate_kernel,
        grid=(T // tm,),
        in_specs=[pl.BlockSpec((tm, W), lambda i: (i, 0)),
                  pl.BlockSpec((tm, W), lambda i: (i, 0)),
                  pl.BlockSpec((tm, D), lambda i: (i, 2)),
                  pl.BlockSpec((tm, D), lambda i: (i, 3)),
                  pl.BlockSpec((W, D), lambda i: (0, 0)),
                  pl.BlockSpec((W, D), lambda i: (0, 0))],
        out_specs=pl.BlockSpec((tm, D), lambda i: (i, 0)),
        out_shape=jax.ShapeDtypeStruct((T, D), BF16),
        compiler_params=_cp(("parallel",)),
        name="mix_gate",
    )(att, ssm, h, h, p_a, p_b)


def _mix_out_kernel(m_ref, x_ref, g1_ref, sh2_ref, sc2_ref, wo_ref, lg_ref, lb_ref, wr_ref, br_ref,
                    x1_ref, u2_ref, ri_ref, rw_ref):
    out = _dot(m_ref[...], wo_ref[...])
    x1 = _layernorm(DN_ALPHA * x_ref[...] + g1_ref[...] * out, lg_ref[...], lb_ref[...])
    x1_ref[...] = x1
    u2 = x1 * (1.0 + sc2_ref[...]) + sh2_ref[...]
    u2_ref[...] = u2
    logits = jnp.dot(u2, wr_ref[...], preferred_element_type=F32,
                     precision=lax.Precision.HIGHEST) + br_ref[...]
    lane = lax.broadcasted_iota(jnp.int32, logits.shape, 1)
    cur = logits
    vals, idxs = [], []
    for _ in range(TOP_K):
        mk = jnp.max(cur, axis=1, keepdims=True)
        ik = jnp.min(jnp.where(cur == mk, lane, LANES), axis=1, keepdims=True)
        vals.append(mk)
        idxs.append(ik)
        cur = jnp.where(lane == ik, -jnp.inf, cur)
    es = [jnp.exp(v - vals[0]) for v in vals]
    den = es[0] + es[1] + es[2] + es[3]
    ri = jnp.zeros(logits.shape, jnp.int32)
    rw = jnp.zeros(logits.shape, F32)
    for k in range(TOP_K):
        ri = jnp.where(lane == k, idxs[k], ri)
        rw = jnp.where(lane == k, es[k] / den, rw)
    ri_ref[...] = ri
    rw_ref[...] = rw


def _mix_out(m, x, g1, sh2, sc2, w_o, ln_g, ln_b, w_r, b_r, *, seq_len):
    T, D = x.shape
    tm = min(256, T)
    if seq_len > 0:
        seq_tiles = seq_len // tm
        mod_spec = pl.BlockSpec((None, 1, D), lambda i: (i // seq_tiles, 0, 0))
    else:
        mod_spec = pl.BlockSpec((tm, D), lambda i: (i, 0))
    row = pl.BlockSpec((tm, D), lambda i: (i, 0))
    vec = pl.BlockSpec((1, D), lambda i: (0, 0))
    small = pl.BlockSpec((tm, LANES), lambda i: (i, 0))
    return pl.pallas_call(
        _mix_out_kernel,
        grid=(T // tm,),
        in_specs=[row, row, mod_spec, mod_spec, mod_spec,
                  pl.BlockSpec((D, D), lambda i: (0, 0)), vec, vec,
                  pl.BlockSpec((D, LANES), lambda i: (0, 0)),
                  pl.BlockSpec((1, LANES), lambda i: (0, 0))],
        out_specs=[row, row, small, small],
        out_shape=[jax.ShapeDtypeStruct((T, D), F32), jax.ShapeDtypeStruct((T, D), F32),
                   jax.ShapeDtypeStruct((T, LANES), jnp.int32), jax.ShapeDtypeStruct((T, LANES), F32)],
        compiler_params=_cp(("parallel",)),
        name="mix_out",
    )(m, x, g1, sh2, sc2, w_o, ln_g, ln_b, w_r, b_r)


def _gather_kernel(cnt_ref, blk_ref, idx_ref, x_hbm, o_ref, buf, sem):
    i = pl.program_id(0)
    gb = buf.shape[0]
    n = cnt_ref[i]

    @pl.when(i == 0)
    def _():
        buf[...] = jnp.zeros_like(buf)

    @pl.when(n > 0)
    def _():
        def issue(r, _):
            tok = idx_ref[0, 0, r]
            pltpu.make_async_copy(x_hbm.at[pl.ds(tok, 1), :], buf.at[pl.ds(r, 1), :], sem).start()
            return 0

        n8 = pl.multiple_of((n + 7) // 8 * 8, 8)
        lax.fori_loop(0, n8, issue, 0)
        pltpu.make_async_copy(x_hbm.at[pl.ds(0, n8), :], buf.at[pl.ds(0, n8), :], sem).wait()
        o_ref[...] = buf[...].astype(o_ref.dtype)


def _moe_gather(x, idx, cnt, blk):
    T, D = x.shape
    P = idx.shape[0]
    gb = MOE_SUB
    nb = P // gb
    grid_spec = pltpu.PrefetchScalarGridSpec(
        num_scalar_prefetch=2,
        grid=(nb,),
        in_specs=[pl.BlockSpec((1, 1, gb), lambda i, cnt, blk: (blk[i], 0, 0), memory_space=pltpu.SMEM),
                  pl.BlockSpec(memory_space=pl.ANY)],
        out_specs=pl.BlockSpec((gb, D), lambda i, cnt, blk: (blk[i], 0)),
        scratch_shapes=[pltpu.VMEM((gb, D), F32), pltpu.SemaphoreType.DMA(())],
    )
    return pl.pallas_call(
        _gather_kernel,
        grid_spec=grid_spec,
        out_shape=jax.ShapeDtypeStruct((P, D), BF16),
        compiler_params=pltpu.CompilerParams(dimension_semantics=("arbitrary",), vmem_limit_bytes=VMEM_LIMIT,
                                             disable_bounds_checks=True),
        name="moe_gather",
    )(cnt, blk, idx.reshape(nb, 1, gb), x)


def _moe_kernel(be_ref, nu_ref, nv_ref, x_ref, wg_ref, wu_ref, wd_ref, bg_ref, bu_ref, bd_ref, y_ref,
                a_scr, wgb, wub, wdb):
    i = pl.program_id(0)
    s = pl.program_id(1)
    nf = pl.num_programs(1) // 2
    valid = nv_ref[i]

    @pl.when((s < nf) & (valid > 0))
    def _():
        wgb[...] = wg_ref[...].astype(BF16)
        wub[...] = wu_ref[...].astype(BF16)
        for sbk in range(MOE_BLOCK // MOE_SUB):
            @pl.when(valid > sbk * MOE_SUB)
            def _():
                rows = pl.ds(sbk * MOE_SUB, MOE_SUB)
                xs = x_ref[rows, :]
                gate = jnp.minimum(_dot(xs, wgb[...]) + bg_ref[...], SWIGLU_LIMIT)
                up = jnp.clip(_dot(xs, wub[...]) + bu_ref[...], -SWIGLU_LIMIT, SWIGLU_LIMIT)
                a_scr[s, rows, :] = ((up + 1.0) * (gate * jax.nn.sigmoid(SWIGLU_ALPHA * gate))).astype(BF16)

    @pl.when((s >= nf) & (valid > 0))
    def _():
        wdb[...] = wd_ref[...].astype(BF16)
        for sbk in range(MOE_BLOCK // MOE_SUB):
            rows = pl.ds(sbk * MOE_SUB, MOE_SUB)

            @pl.when(valid > sbk * MOE_SUB)
            def _():
                a = jnp.concatenate([a_scr[f, rows, :] for f in range(a_scr.shape[0])], axis=1)
                y_ref[rows, :] = _dot(a, wdb[...]) + bd_ref[...]

            @pl.when(valid <= sbk * MOE_SUB)
            def _():
                y_ref[rows, :] = jnp.zeros((MOE_SUB, y_ref.shape[1]), F32)


def _moe_experts(xs, blk_e, nused, nvalid, w_gu, b_gu, w_dn, b_dn):
    P, D = xs.shape
    F = w_dn.shape[2]
    NB = P // MOE_BLOCK
    NF = F // MOE_FT
    ND = D // MOE_FT

    def eff(i, nu):
        return jnp.minimum(i, nu[0] - 1)

    def fa(i, s, nu):
        return jnp.where(i < nu[0], jnp.minimum(s, NF - 1), NF - 1)

    def fb(i, s, nu):
        return jnp.where(i < nu[0], jnp.maximum(s - NF, 0), ND - 1)

    grid_spec = pltpu.PrefetchScalarGridSpec(
        num_scalar_prefetch=3,
        grid=(NB, NF + ND),
        in_specs=[
            pl.BlockSpec((MOE_BLOCK, D), lambda i, s, be, nu, nv: (eff(i, nu), 0)),
            pl.BlockSpec((None, None, D, MOE_FT), lambda i, s, be, nu, nv: (0, be[eff(i, nu)], 0, fa(i, s, nu))),
            pl.BlockSpec((None, None, D, MOE_FT),
                         lambda i, s, be, nu, nv: (0, be[eff(i, nu)], 0, NF + fa(i, s, nu))),
            pl.BlockSpec((None, None, F, MOE_FT), lambda i, s, be, nu, nv: (0, be[eff(i, nu)], 0, fb(i, s, nu))),
            pl.BlockSpec((None, 1, MOE_FT), lambda i, s, be, nu, nv: (be[eff(i, nu)], 0, fa(i, s, nu))),
            pl.BlockSpec((None, 1, MOE_FT), lambda i, s, be, nu, nv: (be[eff(i, nu)], 0, NF + fa(i, s, nu))),
            pl.BlockSpec((None, 1, MOE_FT), lambda i, s, be, nu, nv: (be[eff(i, nu)], 0, fb(i, s, nu))),
        ],
        out_specs=pl.BlockSpec((MOE_BLOCK, MOE_FT), lambda i, s, be, nu, nv: (eff(i, nu), fb(i, s, nu))),
        scratch_shapes=[pltpu.VMEM((NF, MOE_BLOCK, MOE_FT), BF16),
                        pltpu.VMEM((D, MOE_FT), BF16), pltpu.VMEM((D, MOE_FT), BF16),
                        pltpu.VMEM((F, MOE_FT), BF16)],
    )
    assert NF == ND
    return pl.pallas_call(
        _moe_kernel,
        grid_spec=grid_spec,
        out_shape=jax.ShapeDtypeStruct((P, D), F32),
        compiler_params=_cp(("arbitrary", "arbitrary"), MOE_VMEM_LIMIT),
        name="moe_experts",
    )(blk_e, nused, nvalid, xs, w_gu, w_gu, w_dn, b_gu, b_gu, b_dn)


def _combine_kernel(idx_ref, y_hbm, x1_ref, g2_ref, rw_ref, lg_ref, lb_ref, o_ref, buf, sem):
    tm = x1_ref.shape[0]

    def issue(r, _):
        for k in range(TOP_K):
            d = idx_ref[0, 0, k * tm + r]
            pltpu.make_async_copy(y_hbm.at[pl.ds(d, 1), :], buf.at[k, pl.ds(r, 1), :], sem).start()
        return 0

    lax.fori_loop(0, tm, issue, 0, unroll=2)
    for k in range(TOP_K):
        pltpu.make_async_copy(y_hbm.at[pl.ds(0, tm), :], buf.at[k], sem).wait()
    rw = rw_ref[...]
    moe = rw[:, 0:1] * buf[0]
    for k in range(1, TOP_K):
        moe = moe + rw[:, k:k + 1] * buf[k]
    o_ref[...] = _layernorm(DN_ALPHA * x1_ref[...] + g2_ref[...] * moe, lg_ref[...], lb_ref[...])


def _moe_combine(y, dest, x1, g2, rw, ln_g, ln_b, *, seq_len):
    T, D = x1.shape
    tm = 128
    nt = T // tm
    idx = dest.reshape(nt, tm, TOP_K).transpose(0, 2, 1).reshape(nt, 1, TOP_K * tm)
    if seq_len > 0:
        seq_tiles = seq_len // tm
        mod_spec = pl.BlockSpec((None, 1, D), lambda i: (i // seq_tiles, 0, 0))
    else:
        mod_spec = pl.BlockSpec((tm, D), lambda i: (i, 0))
    row = pl.BlockSpec((tm, D), lambda i: (i, 0))
    vec = pl.BlockSpec((1, D), lambda i: (0, 0))
    return pl.pallas_call(
        _combine_kernel,
        grid=(nt,),
        in_specs=[pl.BlockSpec((1, 1, TOP_K * tm), lambda i: (i, 0, 0), memory_space=pltpu.SMEM),
                  pl.BlockSpec(memory_space=pl.ANY),
                  row, mod_spec, pl.BlockSpec((tm, LANES), lambda i: (i, 0)), vec, vec],
        out_specs=row,
        out_shape=jax.ShapeDtypeStruct((T, D), F32),
        scratch_shapes=[pltpu.VMEM((TOP_K, tm, D), F32), pltpu.SemaphoreType.DMA(())],
        compiler_params=pltpu.CompilerParams(dimension_semantics=("arbitrary",), vmem_limit_bytes=VMEM_LIMIT,
                                             disable_bounds_checks=True),
        name="moe_combine",
    )(idx, y, x1, g2, rw, ln_g, ln_b)


def _route(experts):
    T = experts.shape[0]
    A = T * TOP_K
    flat_e = experts.reshape(A)
    onehot = (flat_e[:, None] == jnp.arange(N_EXPERTS, dtype=jnp.int32)[None, :]).astype(jnp.int32)
    counts = jnp.sum(onehot, axis=0)
    rank = jnp.sum((jnp.cumsum(onehot, axis=0) - onehot) * onehot, axis=1)
    nblk = (counts + MOE_BLOCK - 1) // MOE_BLOCK
    blk_end = jnp.cumsum(nblk)
    blk_start = blk_end - nblk
    dest = blk_start[flat_e] * MOE_BLOCK + rank
    NB = -(-A // MOE_BLOCK) + N_EXPERTS
    idx = jnp.zeros((NB * MOE_BLOCK,), jnp.int32).at[dest].set(jnp.arange(A, dtype=jnp.int32) // TOP_K)
    blk_e = jnp.minimum(jnp.searchsorted(blk_end, jnp.arange(NB, dtype=jnp.int32), side='right'),
                        N_EXPERTS - 1).astype(jnp.int32)
    nused = blk_end[-1:].astype(jnp.int32)
    blk = jnp.arange(NB, dtype=jnp.int32)
    nvalid = jnp.clip(counts[blk_e] - (blk - blk_start[blk_e]) * MOE_BLOCK, 0, MOE_BLOCK)
    nvalid = jnp.where(blk < nused[0], nvalid, 0).astype(jnp.int32)
    per = MOE_BLOCK // MOE_SUB
    gblk = jnp.arange(NB * per, dtype=jnp.int32)
    gcnt = jnp.clip(jnp.repeat(nvalid, per) - (gblk % per) * MOE_SUB, 0, MOE_SUB).astype(jnp.int32)
    gvisit = lax.cummax(jnp.where(gcnt > 0, gblk, 0), axis=0)
    return dest.reshape(T, TOP_K), idx, blk_e, nused, nvalid, gcnt, gvisit


def _block_diag_in(bb):
    gpc = S5_GROUPS // S5_CHUNKS
    x = bb.reshape(S5_CHUNKS, gpc, S5_GROUP, S5_STATE)
    eye = jnp.eye(gpc, dtype=bb.dtype)
    return jnp.einsum('agcn,gh->agchn', x, eye).reshape(S5_CHUNKS, S5_CHUNK_IN, S5_CHUNK_STATE)


def _block_diag_out(cc):
    gpc = S5_GROUPS // S5_CHUNKS
    x = cc.reshape(S5_CHUNKS, gpc, S5_GROUP, S5_STATE)
    eye = jnp.eye(gpc, dtype=cc.dtype)
    return jnp.einsum('agcn,gh->ahngc', x, eye).reshape(S5_CHUNKS, S5_CHUNK_STATE, S5_CHUNK_IN)


def kernel(x_prompt, x_sample, c_prompt, c_sample, cache_k, cache_v, cache_logf, state_s5_re, state_s5_im, page_table, w_ada, b_ada, w_in, fox_b_f, s5_a_re, s5_a_im, s5_log_dt, s5_b_re, s5_b_im, s5_c_re, s5_c_im, s5_d, w_glu, b_glu, p_a, p_b, w_o, ln1_g, ln1_b, w_router, b_router, w_gu, b_gu, w_dn, b_dn, ln2_g, ln2_b):
    B, L, D = x_prompt.shape
    Bd, Ld, _ = x_sample.shape
    Tp, Ts = B * L, Bd * Ld
    W = HEADS * HEAD_DIM
    n_phys = cache_k.shape[1]

    n_c = B + Bd
    c_all = jnp.concatenate([c_prompt, c_sample, jnp.zeros((-n_c % 8, D), F32)], axis=0)
    mod = _adaln(c_all, w_ada[0], b_ada)
    mp = [mod[:B, i * D:(i + 1) * D].reshape(B, 1, D) for i in range(6)]
    ms = [jnp.repeat(mod[B:n_c, i * D:(i + 1) * D], Ld, axis=0) for i in range(6)]

    wi = w_in[0]
    f_lo = 3 * W
    wm = jnp.concatenate([wi[:, :f_lo], wi[:, f_lo + HEADS:]], axis=1).astype(BF16)
    wf = jnp.pad(wi[:, f_lo:f_lo + HEADS], ((0, 0), (0, LANES - HEADS))).astype(BF16)
    bf = jnp.pad(fox_b_f, ((0, 0), (0, LANES - HEADS)))
    hp, logf_p, cT_p = _in_proj(x_prompt.reshape(Tp, D), mp[0], mp[1], wm, wf, wf.T, bf, bf.T,
                                seq_len=L, seg=0)
    hs, logf_s, c_s = _in_proj(x_sample.reshape(Ts, D), ms[0], ms[1], wm, wf, wf.T, bf, bf.T,
                               seq_len=0, seg=Ld)

    lam_r, lam_i, bb_r, bb_i = _s5_params(s5_a_re[0], s5_a_im[0], s5_log_dt[0][:, None],
                                          s5_b_re[0].transpose(0, 2, 1), s5_b_im[0].transpose(0, 2, 1))
    lam_r = lam_r.reshape(S5_CHUNKS, S5_CHUNK_STATE)
    lam_i = lam_i.reshape(S5_CHUNKS, S5_CHUNK_STATE)
    wb = jnp.concatenate([_block_diag_in(bb_r), _block_diag_in(bb_i)], axis=2).astype(BF16)
    wcr = _block_diag_out(s5_c_re[0]).astype(BF16)
    wci = _block_diag_out(s5_c_im[0]).astype(BF16)
    wgl = w_glu[0].astype(BF16)
    s5_args = (lam_r, lam_i, wb, wcr, wci, s5_d, wgl, b_glu)
    zeros_state = jnp.zeros((B, S5_CHUNKS, S5_CHUNK_STATE), F32)
    s5_t = 256
    ssm_p, sr_p, si_p = _s5(hp, 3, zeros_state, zeros_state, *s5_args, nseq=1, slen=s5_t, chunks=L // s5_t)
    ssm_s, sr_s, si_s = _s5(hs, 3, state_s5_re[0].reshape(Bd, S5_CHUNKS, S5_CHUNK_STATE),
                            state_s5_im[0].reshape(Bd, S5_CHUNKS, S5_CHUNK_STATE), *s5_args,
                            nseq=32, slen=Ld, chunks=1)

    att_p = _fox_prompt(hp, cT_p, B, L)
    scale = HEAD_DIM ** -0.5
    q_s = hs[:, :W].reshape(Bd, Ld, HEADS, HEAD_DIM) * scale
    k_s = hs[:, W:2 * W].reshape(Bd, Ld, HEADS, HEAD_DIM)
    v_s = hs[:, 2 * W:3 * W].reshape(Bd, Ld, HEADS, HEAD_DIM)
    qe = q_s.reshape(Bd, Ld * HEADS, HEAD_DIM)
    qbd = jnp.einsum('bqgd,gh->bqghd', q_s, jnp.eye(HEADS, dtype=F32)).reshape(Bd, Ld * HEADS, W).astype(BF16)
    kn = jnp.broadcast_to(k_s[:, :, None], (Bd, Ld, Ld, HEADS, HEAD_DIM)).reshape(Bd, Ld, Ld * HEADS, HEAD_DIM)
    vn = jnp.broadcast_to(v_s[:, :, None], (Bd, Ld, Ld, HEADS, HEAD_DIM)).reshape(Bd, Ld, Ld * HEADS, HEAD_DIM)
    cn = c_s[:, :HEADS].reshape(Bd, Ld, HEADS)
    bn = jnp.broadcast_to(cn.transpose(0, 2, 1)[:, None], (Bd, Ld, HEADS, Ld)).reshape(Bd, Ld * HEADS, Ld)
    bn = jnp.pad(bn, ((0, 0), (0, 0), (0, LANES - Ld)))
    ckt = jnp.transpose(cache_k[0], (0, 2, 3, 1)).reshape(n_phys, W, PAGE)
    cvt = jnp.transpose(cache_v[0], (0, 2, 3, 1)).reshape(n_phys, W, PAGE)
    clft = jnp.transpose(cache_logf[0], (0, 2, 1))
    att_s = _fox_sample(page_table, ckt, cvt, clft, qbd, qe, kn, vn, bn)
    att_s = att_s.reshape(Ts, W).astype(BF16)

    pab, pbb, wob = p_a[0].astype(BF16), p_b[0].astype(BF16), w_o[0].astype(BF16)
    wr = jnp.pad(w_router[0], ((0, 0), (0, LANES - N_EXPERTS)))
    br = jnp.pad(b_router, ((0, 0), (0, LANES - N_EXPERTS)), constant_values=NEG)
    m_p = _mix_gate(att_p, ssm_p, hp, pab, pbb)
    m_s = _mix_gate(att_s, ssm_s, hs, pab, pbb)
    x1_p, u2_p, ri_p, rw_p = _mix_out(m_p, x_prompt.reshape(Tp, D), mp[2], mp[3], mp[4], wob, ln1_g, ln1_b,
                                      wr, br, seq_len=L)
    x1_s, u2_s, ri_s, rw_s = _mix_out(m_s, x_sample.reshape(Ts, D), ms[2], ms[3], ms[4], wob, ln1_g, ln1_b,
                                      wr, br, seq_len=0)

    u2 = jnp.concatenate([u2_p, u2_s], axis=0)
    experts = jnp.concatenate([ri_p[:, :TOP_K], ri_s[:, :TOP_K]], axis=0)
    dest, idx, blk_e, nused, nvalid, gcnt, gvisit = _route(experts)
    xs = _moe_gather(u2, idx, gcnt, gvisit)
    y = _moe_experts(xs, blk_e, nused, nvalid, w_gu, b_gu[0][:, None, :], w_dn, b_dn[0][:, None, :])
    x2_p = _moe_combine(y, dest[:Tp], x1_p, mp[5], rw_p, ln2_g, ln2_b, seq_len=L)
    x2_s = _moe_combine(y, dest[Tp:], x1_s, ms[5], rw_s, ln2_g, ln2_b, seq_len=0)

    def heads(t, n, l):
        return t.reshape(1, n, l, HEADS, HEAD_DIM)

    def state(s, n):
        return s.reshape(1, n, S5_GROUPS, S5_STATE)

    return (x2_p.reshape(B, L, D), x2_s.reshape(Bd, Ld, D),
            heads(hp[:, W:2 * W], B, L), heads(hp[:, 2 * W:3 * W], B, L),
            logf_p[:, :HEADS].reshape(1, B, L, HEADS), state(sr_p, B), state(si_p, B),
            heads(hs[:, W:2 * W], Bd, Ld), heads(hs[:, 2 * W:3 * W], Bd, Ld),
            logf_s[:, :HEADS].reshape(1, Bd, Ld, HEADS), state(sr_s, Bd), state(si_s, Bd))
```

```python
import functools
import math

import jax
import jax.numpy as jnp
from jax import lax
from jax.experimental import pallas as pl
from jax.experimental.pallas import tpu as pltpu

F32 = jnp.float32
BF16 = jnp.bfloat16

HEADS = 16
HEAD_DIM = 64
PAGE = 128
S5_GROUPS = 64
S5_GROUP = 16
S5_STATE = 64
N_EXPERTS = 32
TOP_K = 4
SWIGLU_LIMIT = 7.0
SWIGLU_ALPHA = 1.702
DN_ALPHA = 2.0 ** 0.25
LN_EPS = 1e-5
NEG = -1e30
LOG2E = math.log2(math.e)

LANES = 128
S5_CHUNKS = 8
S5_CHUNK_STATE = S5_GROUPS * S5_STATE // S5_CHUNKS
S5_CHUNK_IN = S5_GROUPS * S5_GROUP // S5_CHUNKS
S5_LANE_BLOCKS = S5_CHUNK_STATE // LANES
FOX_TQ = 1024
FOX_TK = 1024
MOE_BLOCK = 1152
MOE_SUB = 384
MOE_FT = 512
VMEM_LIMIT = 56 * 1024 * 1024
MOE_VMEM_LIMIT = 60 * 1024 * 1024


def _cp(sem, vmem=VMEM_LIMIT):
    return pltpu.CompilerParams(dimension_semantics=sem, vmem_limit_bytes=vmem)


def _dot(a, b):
    return jnp.dot(a, b, preferred_element_type=F32)


def _dot_nt(a, b):
    return lax.dot_general(a, b, (((1,), (1,)), ((), ())), preferred_element_type=F32)


def _log_sigmoid(x):
    return jnp.minimum(x, 0.0) - jnp.log1p(jnp.exp(-jnp.abs(x)))


def _split3(v):
    hi = v.astype(BF16)
    r = v - hi.astype(F32)
    mid = r.astype(BF16)
    lo = (r - mid.astype(F32)).astype(BF16)
    return hi, mid, lo


def _layernorm(v, g, b):
    mu = jnp.mean(v, axis=-1, keepdims=True)
    d = v - mu
    var = jnp.mean(d * d, axis=-1, keepdims=True)
    return d * lax.rsqrt(var + LN_EPS) * g + b


def _ada_kernel(c_ref, w_ref, b_ref, o_ref):
    c = c_ref[...]
    a = (c * jax.nn.sigmoid(c)).astype(BF16)
    o_ref[...] = _dot(a, w_ref[...].astype(BF16)) + b_ref[...]


def _adaln(c_all, w_ada, b_ada):
    R, D = c_all.shape
    N = w_ada.shape[1]
    tn = 1024
    return pl.pallas_call(
        _ada_kernel,
        grid=(N // tn,),
        in_specs=[pl.BlockSpec((R, D), lambda j: (0, 0)),
                  pl.BlockSpec((D, tn), lambda j: (0, j)),
                  pl.BlockSpec((1, tn), lambda j: (0, j))],
        out_specs=pl.BlockSpec((R, tn), lambda j: (0, j)),
        out_shape=jax.ShapeDtypeStruct((R, N), F32),
        compiler_params=_cp(("arbitrary",)),
        name="adaln",
    )(c_all, w_ada, b_ada)


def _in_kernel(x_ref, sh_ref, sc_ref, wm_ref, wf_ref, wft_ref, bf_ref, bft_ref,
               h_ref, logf_ref, cum_ref, u_scr, carry_scr, *, seq_tiles, seg):
    i = pl.program_id(0)
    j = pl.program_id(1)
    tm = x_ref.shape[0]

    @pl.when(j == 0)
    def _():
        u = x_ref[...] * (1.0 + sc_ref[...]) + sh_ref[...]
        ub = u.astype(BF16)
        u_scr[...] = ub
        lf = _log_sigmoid(_dot(ub, wf_ref[...]) + bf_ref[...])
        logf_ref[...] = lf
        r = lax.broadcasted_iota(jnp.int32, (tm, tm), 0)
        c = lax.broadcasted_iota(jnp.int32, (tm, tm), 1)
        if seq_tiles > 0:
            lft = _log_sigmoid(_dot_nt(wft_ref[...], ub) + bft_ref[...])
            tri = (r <= c).astype(BF16)
            hi, mid, lo = _split3(lft)
            cs = _dot(hi, tri) + _dot(mid, tri) + _dot(lo, tri)

            @pl.when(i % seq_tiles == 0)
            def _():
                carry_scr[...] = jnp.zeros_like(carry_scr)

            cs = cs + carry_scr[...]
            cum_ref[...] = cs
            carry_scr[...] = cs[:, tm - 1:tm]
        else:
            tri = ((c <= r) & (c // seg == r // seg)).astype(BF16)
            hi, mid, lo = _split3(lf)
            cum_ref[...] = _dot(tri, hi) + _dot(tri, mid) + _dot(tri, lo)

    h_ref[...] = _dot(u_scr[...], wm_ref[...])


def _in_proj(x, shift, scale, wm, wf, wft, bf, bft, *, seq_len, seg):
    T, D = x.shape
    N = wm.shape[1]
    tm, tn = min(512, T), 1024
    nt = T // tm
    if seq_len > 0:
        seq_tiles = seq_len // tm
        mod_spec = pl.BlockSpec((None, 1, D), lambda i, j: (i // seq_tiles, 0, 0))
        cum_shape, cum_spec = (LANES, T), pl.BlockSpec((LANES, tm), lambda i, j: (0, i))
    else:
        seq_tiles = 0
        mod_spec = pl.BlockSpec((tm, D), lambda i, j: (i, 0))
        cum_shape, cum_spec = (T, LANES), pl.BlockSpec((tm, LANES), lambda i, j: (i, 0))
    kern = functools.partial(_in_kernel, seq_tiles=seq_tiles, seg=seg)
    return pl.pallas_call(
        kern,
        grid=(nt, N // tn),
        in_specs=[pl.BlockSpec((tm, D), lambda i, j: (i, 0)), mod_spec, mod_spec,
                  pl.BlockSpec((D, tn), lambda i, j: (0, j)),
                  pl.BlockSpec((D, LANES), lambda i, j: (0, 0)),
                  pl.BlockSpec((LANES, D), lambda i, j: (0, 0)),
                  pl.BlockSpec((1, LANES), lambda i, j: (0, 0)),
                  pl.BlockSpec((LANES, 1), lambda i, j: (0, 0))],
        out_specs=[pl.BlockSpec((tm, tn), lambda i, j: (i, j)),
                   pl.BlockSpec((tm, LANES), lambda i, j: (i, 0)),
                   cum_spec],
        out_shape=[jax.ShapeDtypeStruct((T, N), F32),
                   jax.ShapeDtypeStruct((T, LANES), F32),
                   jax.ShapeDtypeStruct(cum_shape, F32)],
        scratch_shapes=[pltpu.VMEM((tm, D), BF16), pltpu.VMEM((LANES, 1), F32)],
        compiler_params=_cp(("arbitrary", "arbitrary")),
        name="in_proj",
    )(x, shift, scale, wm, wf, wft, bf, bft)


def _fox_prompt_kernel(qt_ref, kt_ref, q_ref, k_ref, v_ref, c_ref, o_ref, qs, m_scr, l_scr, acc, *, ratio):
    hp = pl.program_id(1)
    t = pl.program_id(2)
    qi = qt_ref[t]
    kv = kt_ref[t]
    tq, tk = q_ref.shape[0], k_ref.shape[0]
    lane = lax.broadcasted_iota(jnp.int32, (tq, LANES), 1)

    @pl.when(kv == 0)
    def _():
        q2 = q_ref[...] * (HEAD_DIM ** -0.5 * LOG2E)
        qs[0] = jnp.where(lane < HEAD_DIM, q2, 0.0).astype(BF16)
        qs[1] = jnp.where(lane >= HEAD_DIM, q2, 0.0).astype(BF16)
        m_scr[...] = jnp.full_like(m_scr, NEG)
        l_scr[...] = jnp.zeros_like(l_scr)
        acc[...] = jnp.zeros_like(acc)

    def block(diagonal):
        kb = k_ref[...].astype(BF16)
        vb = v_ref[...].astype(BF16)
        r0 = (2 * hp) % 8
        for a in range(2):
            crow = c_ref[pl.ds(r0 + a, 1), :] * LOG2E
            s = _dot_nt(qs[a], kb) - crow
            if diagonal:
                row = lax.broadcasted_iota(jnp.int32, (tq, tk), 0)
                col = lax.broadcasted_iota(jnp.int32, (tq, tk), 1) + (kv * tk - qi * tq)
                s = jnp.where(col <= row, s, NEG)
            cols = [s[:, c * LANES:(c + 1) * LANES] for c in range(tk // LANES)]
            mx = cols[0]
            for sc in cols[1:]:
                mx = jnp.maximum(mx, sc)
            m_prev = m_scr[a]
            m_new = jnp.maximum(m_prev, jnp.max(mx, axis=1, keepdims=True))
            alpha = jnp.exp2(m_prev - m_new)
            ps = [jnp.exp2(sc - m_new) for sc in cols]
            lsum = ps[0]
            for pc in ps[1:]:
                lsum = lsum + pc
            l_scr[a] = alpha * l_scr[a] + lsum
            p = jnp.concatenate(ps, axis=1).astype(BF16)
            acc[a] = alpha * acc[a] + _dot(p, vb)
            m_scr[a] = m_new

    @pl.when(kv < qi * ratio)
    def _():
        block(False)

    @pl.when(kv >= qi * ratio)
    def _():
        block(True)

    @pl.when(kv == (qi + 1) * ratio - 1)
    def _():
        l0 = jnp.sum(l_scr[0], axis=1, keepdims=True)
        l1 = jnp.sum(l_scr[1], axis=1, keepdims=True)
        o = jnp.where(lane < HEAD_DIM, acc[0] / l0, acc[1] / l1)
        o_ref[...] = o.astype(o_ref.dtype)


def _fox_prompt(h, cT, B, L):
    tq, tk = FOX_TQ, FOX_TK
    ratio = tq // tk
    nq, nk = L // tq, L // tk
    npairs = HEADS // 2
    steps = [(qi, kv) for qi in range(nq) for kv in range((qi + 1) * ratio)]
    qt = jnp.asarray([p[0] for p in steps], jnp.int32)
    kt = jnp.asarray([p[1] for p in steps], jnp.int32)
    grid_spec = pltpu.PrefetchScalarGridSpec(
        num_scalar_prefetch=2,
        grid=(B, npairs, len(steps)),
        in_specs=[
            pl.BlockSpec((tq, LANES), lambda b, hp, t, qt, kt: (b * nq + qt[t], hp)),
            pl.BlockSpec((tk, LANES), lambda b, hp, t, qt, kt: (b * nk + kt[t], npairs + hp)),
            pl.BlockSpec((tk, LANES), lambda b, hp, t, qt, kt: (b * nk + kt[t], 2 * npairs + hp)),
            pl.BlockSpec((8, tk), lambda b, hp, t, qt, kt: (hp // 4, b * nk + kt[t])),
        ],
        out_specs=pl.BlockSpec((tq, LANES), lambda b, hp, t, qt, kt: (b * nq + qt[t], hp)),
        scratch_shapes=[pltpu.VMEM((2, tq, LANES), BF16), pltpu.VMEM((2, tq, LANES), F32),
                        pltpu.VMEM((2, tq, LANES), F32), pltpu.VMEM((2, tq, LANES), F32)],
    )
    return pl.pallas_call(
        functools.partial(_fox_prompt_kernel, ratio=ratio),
        grid_spec=grid_spec,
        out_shape=jax.ShapeDtypeStruct((B * L, HEADS * HEAD_DIM), BF16),
        compiler_params=_cp(("parallel", "parallel", "arbitrary")),
        name="fox_prompt",
    )(qt, kt, h, h, h, cT)


def _fox_sample_kernel(pt_ref, *refs, n_pages):
    k_refs = refs[:n_pages]
    v_refs = refs[n_pages:2 * n_pages]
    lf_refs = refs[2 * n_pages:3 * n_pages]
    qbd_ref, qe_ref, kn_ref, vn_ref, bn_ref, o_ref = refs[3 * n_pages:]
    nq = qe_ref.shape[0] // HEADS
    nrow = nq * HEADS
    width = HEADS * HEAD_DIM

    xs = jnp.concatenate([r[...] for r in lf_refs], axis=0)
    hi, mid, lo = _split3(xs)
    after = (lax.broadcasted_iota(jnp.int32, (PAGE, PAGE), 0)
             > lax.broadcasted_iota(jnp.int32, (PAGE, PAGE), 1)).astype(BF16)
    within = _dot(hi, after) + _dot(mid, after) + _dot(lo, after)
    n = n_pages * HEADS
    pr = lax.broadcasted_iota(jnp.int32, (n, n), 0)
    pc = lax.broadcasted_iota(jnp.int32, (n, n), 1)
    later = ((pc // HEADS > pr // HEADS) & (pc % HEADS == pr % HEADS)).astype(BF16)
    beyond = jnp.sum(_dot(later, hi) + _dot(later, mid) + _dot(later, lo), axis=1, keepdims=True)
    bias_rows = within + beyond
    bias = jnp.concatenate([bias_rows[j * HEADS:(j + 1) * HEADS] for j in range(n_pages)], axis=1)
    bias = jnp.concatenate([bias] * nq, axis=0)

    kt = jnp.concatenate([r[...].astype(BF16) for r in k_refs], axis=1)
    s = _dot(qbd_ref[...], kt) + bias

    q_of_row = lax.broadcasted_iota(jnp.int32, (nrow, 1), 0) // HEADS
    qe = qe_ref[...]
    bn = bn_ref[...]
    sn = []
    for j in range(nq):
        sj = jnp.sum(qe * kn_ref[j], axis=1, keepdims=True) - bn[:, j:j + 1]
        sn.append(jnp.where(q_of_row >= j, sj, NEG))
    m = jnp.max(s, axis=1, keepdims=True)
    for sj in sn:
        m = jnp.maximum(m, sj)
    p = jnp.exp(s - m)
    l = jnp.sum(p, axis=1, keepdims=True)

    vt = jnp.concatenate([r[...].astype(BF16) for r in v_refs], axis=1)
    o_all = _dot_nt(p.astype(BF16), vt)
    rh = lax.broadcasted_iota(jnp.int32, (nrow, width), 0) % HEADS
    ch = lax.broadcasted_iota(jnp.int32, (nrow, width), 1) // HEAD_DIM
    o_hi, o_mid, o_lo = _split3(jnp.where(rh == ch, o_all, 0.0))
    fold = (lax.broadcasted_iota(jnp.int32, (width, HEAD_DIM), 0) % HEAD_DIM
            == lax.broadcasted_iota(jnp.int32, (width, HEAD_DIM), 1)).astype(BF16)
    o = _dot(o_hi, fold) + _dot(o_mid, fold) + _dot(o_lo, fold)
    for j in range(nq):
        pj = jnp.exp(sn[j] - m)
        l = l + pj
        o = o + pj * vn_ref[j]
    o_ref[...] = o / l


def _fox_sample(page_table, ckt, cvt, clft, qbd, qe, kn, vn, bn):
    Bd, n_pages = page_table.shape
    nrow = qe.shape[1]
    nq = nrow // HEADS
    width = HEADS * HEAD_DIM

    def page_map(j):
        return lambda b, pt: (pt[b * n_pages + j], 0, 0)

    per_row3 = lambda b, pt: (b, 0, 0)
    per_row4 = lambda b, pt: (b, 0, 0, 0)
    grid_spec = pltpu.PrefetchScalarGridSpec(
        num_scalar_prefetch=1,
        grid=(Bd,),
        in_specs=([pl.BlockSpec((None, width, PAGE), page_map(j)) for j in range(n_pages)]
                  + [pl.BlockSpec((None, width, PAGE), page_map(j)) for j in range(n_pages)]
                  + [pl.BlockSpec((None, HEADS, PAGE), page_map(j)) for j in range(n_pages)]
                  + [pl.BlockSpec((None, nrow, width), per_row3),
                     pl.BlockSpec((None, nrow, HEAD_DIM), per_row3),
                     pl.BlockSpec((None, nq, nrow, HEAD_DIM), per_row4),
                     pl.BlockSpec((None, nq, nrow, HEAD_DIM), per_row4),
                     pl.BlockSpec((None, nrow, LANES), per_row3)]),
        out_specs=pl.BlockSpec((None, nrow, HEAD_DIM), per_row3),
    )
    return pl.pallas_call(
        functools.partial(_fox_sample_kernel, n_pages=n_pages),
        grid_spec=grid_spec,
        out_shape=jax.ShapeDtypeStruct((Bd, nrow, HEAD_DIM), F32),
        compiler_params=_cp(("parallel",)),
        name="fox_sample",
    )(page_table.reshape(-1), *([ckt] * n_pages), *([cvt] * n_pages), *([clft] * n_pages),
      qbd, qe, kn, vn, bn)


def _s5_param_kernel(ar_ref, ai_ref, ldt_ref, br_ref, bi_ref, lr_ref, li_ref, bbr_ref, bbi_ref):
    ar, ai = ar_ref[...], ai_ref[...]
    dt = jnp.exp(ldt_ref[...])
    mag = jnp.exp(ar * dt)
    lr = mag * jnp.cos(ai * dt)
    li = mag * jnp.sin(ai * dt)
    den = ar * ar + ai * ai
    zr = ((lr - 1.0) * ar + li * ai) / den
    zi = (li * ar - (lr - 1.0) * ai) / den
    lr_ref[...] = lr
    li_ref[...] = li
    br, bi = br_ref[...], bi_ref[...]
    zr3, zi3 = zr[:, None, :], zi[:, None, :]
    bbr_ref[...] = zr3 * br - zi3 * bi
    bbi_ref[...] = zr3 * bi + zi3 * br


def _s5_params(a_re, a_im, log_dt, bt_re, bt_im):
    G, N = a_re.shape
    C = bt_re.shape[1]
    return pl.pallas_call(
        _s5_param_kernel,
        out_shape=[jax.ShapeDtypeStruct((G, N), F32), jax.ShapeDtypeStruct((G, N), F32),
                   jax.ShapeDtypeStruct((G, C, N), F32), jax.ShapeDtypeStruct((G, C, N), F32)],
        name="s5_params",
    )(a_re, a_im, log_dt, bt_re, bt_im)


def _s5_kernel(u_ref, h0r_ref, h0i_ref, lr_ref, li_ref, wb_ref, wcr_ref, wci_ref, d_ref, wg_ref, bg_ref,
               o_ref, sr_ref, si_ref, bur, bui, y_scr, cr, ci, *, nseq, slen):
    c_id = pl.program_id(1)
    R = u_ref.shape[0]
    u = u_ref[...]
    ub = u.astype(BF16)
    rs = [_dot(ub[:, gc * S5_CHUNK_IN:(gc + 1) * S5_CHUNK_IN], wb_ref[gc]) for gc in range(S5_CHUNKS)]
    for lb in range(S5_LANE_BLOCKS):
        lo = lb * LANES
        xr = jnp.stack([r[:, lo:lo + LANES] for r in rs], axis=0)
        xi = jnp.stack([r[:, S5_CHUNK_STATE + lo:S5_CHUNK_STATE + lo + LANES] for r in rs], axis=0)
        bur[lb] = jnp.swapaxes(xr, 0, 1)
        bui[lb] = jnp.swapaxes(xi, 0, 1)

    @pl.when(c_id == 0)
    def _():
        cr[...] = h0r_ref[...]
        ci[...] = h0i_ref[...]

    lr = lr_ref[...]
    li = li_ref[...]

    def seq_body(q, _):
        base = q * slen

        def step(t, hc):
            hr, hi = hc
            br = jnp.concatenate([bur[lb, base + t] for lb in range(S5_LANE_BLOCKS)], axis=1)
            bi = jnp.concatenate([bui[lb, base + t] for lb in range(S5_LANE_BLOCKS)], axis=1)
            nr = lr * hr - li * hi + br
            ni = lr * hi + li * hr + bi
            for lb in range(S5_LANE_BLOCKS):
                bur[lb, base + t] = nr[:, lb * LANES:(lb + 1) * LANES]
                bui[lb, base + t] = ni[:, lb * LANES:(lb + 1) * LANES]
            return nr, ni

        hr, hi = lax.fori_loop(0, slen, step, (cr[q], ci[q]), unroll=4)
        cr[q] = hr
        ci[q] = hi
        return 0

    lax.fori_loop(0, nseq, seq_body, 0)
    sr_ref[...] = cr[...]
    si_ref[...] = ci[...]

    hr_t = [jnp.swapaxes(bur[lb], 0, 1) for lb in range(S5_LANE_BLOCKS)]
    hi_t = [jnp.swapaxes(bui[lb], 0, 1) for lb in range(S5_LANE_BLOCKS)]
    for gc in range(S5_CHUNKS):
        hrb = jnp.concatenate([x[gc] for x in hr_t], axis=1).astype(BF16)
        hib = jnp.concatenate([x[gc] for x in hi_t], axis=1).astype(BF16)
        y_scr[:, gc * S5_CHUNK_IN:(gc + 1) * S5_CHUNK_IN] = _dot(hrb, wcr_ref[gc]) - _dot(hib, wci_ref[gc])
    y = y_scr[...] + d_ref[...] * u
    z = jax.nn.gelu(y)
    o_ref[...] = (z * jax.nn.sigmoid(_dot(z.astype(BF16), wg_ref[...]) + bg_ref[...])).astype(o_ref.dtype)


def _s5(h, col_block, h0r, h0i, lam_r, lam_i, wb, wcr, wci, d_skip, w_glu, b_glu, *, nseq, slen, chunks):
    T = h.shape[0]
    W = S5_GROUPS * S5_GROUP
    R = nseq * slen
    nsb = T // (R * chunks)
    n_seq_total = h0r.shape[0]
    st_spec = pl.BlockSpec((nseq, S5_CHUNKS, S5_CHUNK_STATE), lambda sb, c: (sb, 0, 0))
    full3 = lambda sb, c: (0, 0, 0)
    full2 = lambda sb, c: (0, 0)
    kern = functools.partial(_s5_kernel, nseq=nseq, slen=slen)
    return pl.pallas_call(
        kern,
        grid=(nsb, chunks),
        in_specs=[pl.BlockSpec((R, W), lambda sb, c: (sb * chunks + c, col_block)),
                  st_spec, st_spec,
                  pl.BlockSpec((S5_CHUNKS, S5_CHUNK_STATE), full2),
                  pl.BlockSpec((S5_CHUNKS, S5_CHUNK_STATE), full2),
                  pl.BlockSpec((S5_CHUNKS, S5_CHUNK_IN, 2 * S5_CHUNK_STATE), full3),
                  pl.BlockSpec((S5_CHUNKS, S5_CHUNK_STATE, S5_CHUNK_IN), full3),
                  pl.BlockSpec((S5_CHUNKS, S5_CHUNK_STATE, S5_CHUNK_IN), full3),
                  pl.BlockSpec((1, W), full2),
                  pl.BlockSpec((W, W), full2),
                  pl.BlockSpec((1, W), full2)],
        out_specs=[pl.BlockSpec((R, W), lambda sb, c: (sb * chunks + c, 0)), st_spec, st_spec],
        out_shape=[jax.ShapeDtypeStruct((T, W), BF16),
                   jax.ShapeDtypeStruct((n_seq_total, S5_CHUNKS, S5_CHUNK_STATE), F32),
                   jax.ShapeDtypeStruct((n_seq_total, S5_CHUNKS, S5_CHUNK_STATE), F32)],
        scratch_shapes=[pltpu.VMEM((S5_LANE_BLOCKS, R, S5_CHUNKS, LANES), F32),
                        pltpu.VMEM((S5_LANE_BLOCKS, R, S5_CHUNKS, LANES), F32),
                        pltpu.VMEM((R, W), F32),
                        pltpu.VMEM((nseq, S5_CHUNKS, S5_CHUNK_STATE), F32),
                        pltpu.VMEM((nseq, S5_CHUNKS, S5_CHUNK_STATE), F32)],
        compiler_params=_cp(("arbitrary", "arbitrary")),
        name="s5",
    )(h, h0r, h0i, lam_r, lam_i, wb, wcr, wci, d_skip, w_glu, b_glu)


def _mix_gate_kernel(att_ref, ssm_ref, ga_ref, gb_ref, pa_ref, pb_ref, o_ref):
    m = (jax.nn.sigmoid(ga_ref[...]) * _dot(att_ref[...], pa_ref[...])
         + jax.nn.sigmoid(gb_ref[...]) * _dot(ssm_ref[...], pb_ref[...]))
    o_ref[...] = m.astype(o_ref.dtype)


def _mix_gate(att, ssm, h, p_a, p_b):
    T, W = att.shape
    D = p_a.shape[1]
    tm = min(256, T)
    return pl.pallas_call(
        _mix_gate_kernel,
        grid=(T // tm,),
        in_specs=[pl.BlockSpec((tm, W), lambda i: (i, 0)),
                  pl.BlockSpec((tm, W), lambda i: (i, 0)),
                  pl.BlockSpec((tm, D), lambda i: (i, 2)),
                  pl.BlockSpec((tm, D), lambda i: (i, 3)),
                  pl.BlockSpec((W, D), lambda i: (0, 0)),
                  pl.BlockSpec((W, D), lambda i: (0, 0))],
        out_specs=pl.BlockSpec((tm, D), lambda i: (i, 0)),
        out_shape=jax.ShapeDtypeStruct((T, D), BF16),
        compiler_params=_cp(("parallel",)),
        name="mix_gate",
    )(att, ssm, h, h, p_a, p_b)


def _mix_out_kernel(m_ref, x_ref, g1_ref, sh2_ref, sc2_ref, wo_ref, lg_ref, lb_ref, wr_ref, br_ref,
                    x1_ref, u2_ref, ri_ref, rw_ref):
    out = _dot(m_ref[...], wo_ref[...])
    x1 = _layernorm(DN_ALPHA * x_ref[...] + g1_ref[...] * out, lg_ref[...], lb_ref[...])
    x1_ref[...] = x1
    u2 = x1 * (1.0 + sc2_ref[...]) + sh2_ref[...]
    u2_ref[...] = u2
    u_hi = u2.astype(BF16)
    u_lo = (u2 - u_hi.astype(F32)).astype(BF16)
    wr = wr_ref[...]
    w_hi = wr.astype(BF16)
    w_lo = (wr - w_hi.astype(F32)).astype(BF16)
    logits = _dot(u_hi, w_hi) + _dot(u_hi, w_lo) + _dot(u_lo, w_hi) + br_ref[...]
    lane = lax.broadcasted_iota(jnp.int32, logits.shape, 1)
    cur = logits
    vals, idxs = [], []
    for _ in range(TOP_K):
        mk = jnp.max(cur, axis=1, keepdims=True)
        ik = jnp.min(jnp.where(cur == mk, lane, LANES), axis=1, keepdims=True)
        vals.append(mk)
        idxs.append(ik)
        cur = jnp.where(lane == ik, -jnp.inf, cur)
    es = [jnp.exp(v - vals[0]) for v in vals]
    den = es[0] + es[1] + es[2] + es[3]
    ri = jnp.zeros(logits.shape, jnp.int32)
    rw = jnp.zeros(logits.shape, F32)
    for k in range(TOP_K):
        ri = jnp.where(lane == k, idxs[k], ri)
        rw = jnp.where(lane == k, es[k] / den, rw)
    ri_ref[...] = ri
    rw_ref[...] = rw


def _mix_out(m, x, g1, sh2, sc2, w_o, ln_g, ln_b, w_r, b_r, *, seq_len):
    T, D = x.shape
    tm = min(256, T)
    if seq_len > 0:
        seq_tiles = seq_len // tm
        mod_spec = pl.BlockSpec((None, 1, D), lambda i: (i // seq_tiles, 0, 0))
    else:
        mod_spec = pl.BlockSpec((tm, D), lambda i: (i, 0))
    row = pl.BlockSpec((tm, D), lambda i: (i, 0))
    vec = pl.BlockSpec((1, D), lambda i: (0, 0))
    small = pl.BlockSpec((tm, LANES), lambda i: (i, 0))
    return pl.pallas_call(
        _mix_out_kernel,
        grid=(T // tm,),
        in_specs=[row, row, mod_spec, mod_spec, mod_spec,
                  pl.BlockSpec((D, D), lambda i: (0, 0)), vec, vec,
                  pl.BlockSpec((D, LANES), lambda i: (0, 0)),
                  pl.BlockSpec((1, LANES), lambda i: (0, 0))],
        out_specs=[row, row, small, small],
        out_shape=[jax.ShapeDtypeStruct((T, D), F32), jax.ShapeDtypeStruct((T, D), F32),
                   jax.ShapeDtypeStruct((T, LANES), jnp.int32), jax.ShapeDtypeStruct((T, LANES), F32)],
        compiler_params=_cp(("parallel",)),
        name="mix_out",
    )(m, x, g1, sh2, sc2, w_o, ln_g, ln_b, w_r, b_r)


def _gather_kernel(cnt_ref, blk_ref, idx_ref, x_hbm, o_ref, buf, sem):
    i = pl.program_id(0)
    gb = buf.shape[0]
    n = cnt_ref[i]

    @pl.when(i == 0)
    def _():
        buf[...] = jnp.zeros_like(buf)

    @pl.when(n > 0)
    def _():
        def issue(r, _):
            tok = idx_ref[0, 0, r]
            pltpu.make_async_copy(x_hbm.at[pl.ds(tok, 1), :], buf.at[pl.ds(r, 1), :], sem).start()
            return 0

        n8 = pl.multiple_of((n + 7) // 8 * 8, 8)
        lax.fori_loop(0, n8, issue, 0)
        pltpu.make_async_copy(x_hbm.at[pl.ds(0, n8), :], buf.at[pl.ds(0, n8), :], sem).wait()
        o_ref[...] = buf[...].astype(o_ref.dtype)


def _moe_gather(x, idx, cnt, blk):
    T, D = x.shape
    P = idx.shape[0]
    gb = MOE_SUB
    nb = P // gb
    grid_spec = pltpu.PrefetchScalarGridSpec(
        num_scalar_prefetch=2,
        grid=(nb,),
        in_specs=[pl.BlockSpec((1, 1, gb), lambda i, cnt, blk: (blk[i], 0, 0), memory_space=pltpu.SMEM),
                  pl.BlockSpec(memory_space=pl.ANY)],
        out_specs=pl.BlockSpec((gb, D), lambda i, cnt, blk: (blk[i], 0)),
        scratch_shapes=[pltpu.VMEM((gb, D), F32), pltpu.SemaphoreType.DMA(())],
    )
    return pl.pallas_call(
        _gather_kernel,
        grid_spec=grid_spec,
        out_shape=jax.ShapeDtypeStruct((P, D), BF16),
        compiler_params=pltpu.CompilerParams(dimension_semantics=("arbitrary",), vmem_limit_bytes=VMEM_LIMIT,
                                             disable_bounds_checks=True),
        name="moe_gather",
    )(cnt, blk, idx.reshape(nb, 1, gb), x)


def _moe_kernel(be_ref, nu_ref, nv_ref, x_ref, wg_ref, wu_ref, wd_ref, bg_ref, bu_ref, bd_ref, y_ref,
                a_scr, wgb, wub, wdb):
    i = pl.program_id(0)
    s = pl.program_id(1)
    nf = pl.num_programs(1) // 2
    valid = nv_ref[i]

    @pl.when((s < nf) & (valid > 0))
    def _():
        wgb[...] = wg_ref[...].astype(BF16)
        wub[...] = wu_ref[...].astype(BF16)
        for sbk in range(MOE_BLOCK // MOE_SUB):
            @pl.when(valid > sbk * MOE_SUB)
            def _():
                rows = pl.ds(sbk * MOE_SUB, MOE_SUB)
                xs = x_ref[rows, :]
                gate = jnp.minimum(_dot(xs, wgb[...]) + bg_ref[...], SWIGLU_LIMIT)
                up = jnp.clip(_dot(xs, wub[...]) + bu_ref[...], -SWIGLU_LIMIT, SWIGLU_LIMIT)
                a_scr[s, rows, :] = ((up + 1.0) * (gate * jax.nn.sigmoid(SWIGLU_ALPHA * gate))).astype(BF16)

    @pl.when((s >= nf) & (valid > 0))
    def _():
        wdb[...] = wd_ref[...].astype(BF16)
        for sbk in range(MOE_BLOCK // MOE_SUB):
            rows = pl.ds(sbk * MOE_SUB, MOE_SUB)

            @pl.when(valid > sbk * MOE_SUB)
            def _():
                a = jnp.concatenate([a_scr[f, rows, :] for f in range(a_scr.shape[0])], axis=1)
                y_ref[rows, :] = _dot(a, wdb[...]) + bd_ref[...]

            @pl.when(valid <= sbk * MOE_SUB)
            def _():
                y_ref[rows, :] = jnp.zeros((MOE_SUB, y_ref.shape[1]), F32)


def _moe_experts(xs, blk_e, nused, nvalid, w_gu, b_gu, w_dn, b_dn):
    P, D = xs.shape
    F = w_dn.shape[2]
    NB = P // MOE_BLOCK
    NF = F // MOE_FT
    ND = D // MOE_FT

    def eff(i, nu):
        return jnp.minimum(i, nu[0] - 1)

    def fa(i, s, nu):
        return jnp.where(i < nu[0], jnp.minimum(s, NF - 1), NF - 1)

    def fb(i, s, nu):
        return jnp.where(i < nu[0], jnp.maximum(s - NF, 0), ND - 1)

    grid_spec = pltpu.PrefetchScalarGridSpec(
        num_scalar_prefetch=3,
        grid=(NB, NF + ND),
        in_specs=[
            pl.BlockSpec((MOE_BLOCK, D), lambda i, s, be, nu, nv: (eff(i, nu), 0)),
            pl.BlockSpec((None, None, D, MOE_FT), lambda i, s, be, nu, nv: (0, be[eff(i, nu)], 0, fa(i, s, nu))),
            pl.BlockSpec((None, None, D, MOE_FT),
                         lambda i, s, be, nu, nv: (0, be[eff(i, nu)], 0, NF + fa(i, s, nu))),
            pl.BlockSpec((None, None, F, MOE_FT), lambda i, s, be, nu, nv: (0, be[eff(i, nu)], 0, fb(i, s, nu))),
            pl.BlockSpec((None, 1, MOE_FT), lambda i, s, be, nu, nv: (be[eff(i, nu)], 0, fa(i, s, nu))),
            pl.BlockSpec((None, 1, MOE_FT), lambda i, s, be, nu, nv: (be[eff(i, nu)], 0, NF + fa(i, s, nu))),
            pl.BlockSpec((None, 1, MOE_FT), lambda i, s, be, nu, nv: (be[eff(i, nu)], 0, fb(i, s, nu))),
        ],
        out_specs=pl.BlockSpec((MOE_BLOCK, MOE_FT), lambda i, s, be, nu, nv: (eff(i, nu), fb(i, s, nu))),
        scratch_shapes=[pltpu.VMEM((NF, MOE_BLOCK, MOE_FT), BF16),
                        pltpu.VMEM((D, MOE_FT), BF16), pltpu.VMEM((D, MOE_FT), BF16),
                        pltpu.VMEM((F, MOE_FT), BF16)],
    )
    assert NF == ND
    return pl.pallas_call(
        _moe_kernel,
        grid_spec=grid_spec,
        out_shape=jax.ShapeDtypeStruct((P, D), F32),
        compiler_params=_cp(("arbitrary", "arbitrary"), MOE_VMEM_LIMIT),
        name="moe_experts",
    )(blk_e, nused, nvalid, xs, w_gu, w_gu, w_dn, b_gu, b_gu, b_dn)


def _combine_kernel(idx_ref, y_hbm, x1_ref, g2_ref, rw_ref, lg_ref, lb_ref, o_ref, buf, sem):
    tm = x1_ref.shape[0]

    def issue(r, _):
        for k in range(TOP_K):
            d = idx_ref[0, 0, k * tm + r]
            pltpu.make_async_copy(y_hbm.at[pl.ds(d, 1), :], buf.at[k, pl.ds(r, 1), :], sem).start()
        return 0

    lax.fori_loop(0, tm, issue, 0, unroll=2)
    for k in range(TOP_K):
        pltpu.make_async_copy(y_hbm.at[pl.ds(0, tm), :], buf.at[k], sem).wait()
    rw = rw_ref[...]
    moe = rw[:, 0:1] * buf[0]
    for k in range(1, TOP_K):
        moe = moe + rw[:, k:k + 1] * buf[k]
    o_ref[...] = _layernorm(DN_ALPHA * x1_ref[...] + g2_ref[...] * moe, lg_ref[...], lb_ref[...])


def _moe_combine(y, dest, x1, g2, rw, ln_g, ln_b, *, seq_len):
    T, D = x1.shape
    tm = 128
    nt = T // tm
    idx = dest.reshape(nt, tm, TOP_K).transpose(0, 2, 1).reshape(nt, 1, TOP_K * tm)
    if seq_len > 0:
        seq_tiles = seq_len // tm
        mod_spec = pl.BlockSpec((None, 1, D), lambda i: (i // seq_tiles, 0, 0))
    else:
        mod_spec = pl.BlockSpec((tm, D), lambda i: (i, 0))
    row = pl.BlockSpec((tm, D), lambda i: (i, 0))
    vec = pl.BlockSpec((1, D), lambda i: (0, 0))
    return pl.pallas_call(
        _combine_kernel,
        grid=(nt,),
        in_specs=[pl.BlockSpec((1, 1, TOP_K * tm), lambda i: (i, 0, 0), memory_space=pltpu.SMEM),
                  pl.BlockSpec(memory_space=pl.ANY),
                  row, mod_spec, pl.BlockSpec((tm, LANES), lambda i: (i, 0)), vec, vec],
        out_specs=row,
        out_shape=jax.ShapeDtypeStruct((T, D), F32),
        scratch_shapes=[pltpu.VMEM((TOP_K, tm, D), F32), pltpu.SemaphoreType.DMA(())],
        compiler_params=pltpu.CompilerParams(dimension_semantics=("arbitrary",), vmem_limit_bytes=VMEM_LIMIT,
                                             disable_bounds_checks=True),
        name="moe_combine",
    )(idx, y, x1, g2, rw, ln_g, ln_b)


def _route(experts):
    T = experts.shape[0]
    A = T * TOP_K
    flat_e = experts.reshape(A)
    onehot = (flat_e[:, None] == jnp.arange(N_EXPERTS, dtype=jnp.int32)[None, :]).astype(jnp.int32)
    counts = jnp.sum(onehot, axis=0)
    rank = jnp.sum((jnp.cumsum(onehot, axis=0) - onehot) * onehot, axis=1)
    nblk = (counts + MOE_BLOCK - 1) // MOE_BLOCK
    blk_end = jnp.cumsum(nblk)
    blk_start = blk_end - nblk
    dest = blk_start[flat_e] * MOE_BLOCK + rank
    NB = -(-A // MOE_BLOCK) + N_EXPERTS
    idx = jnp.zeros((NB * MOE_BLOCK,), jnp.int32).at[dest].set(jnp.arange(A, dtype=jnp.int32) // TOP_K)
    blk_e = jnp.minimum(jnp.searchsorted(blk_end, jnp.arange(NB, dtype=jnp.int32), side='right'),
                        N_EXPERTS - 1).astype(jnp.int32)
    nused = blk_end[-1:].astype(jnp.int32)
    blk = jnp.arange(NB, dtype=jnp.int32)
    nvalid = jnp.clip(counts[blk_e] - (blk - blk_start[blk_e]) * MOE_BLOCK, 0, MOE_BLOCK)
    nvalid = jnp.where(blk < nused[0], nvalid, 0).astype(jnp.int32)
    per = MOE_BLOCK // MOE_SUB
    gblk = jnp.arange(NB * per, dtype=jnp.int32)
    gcnt = jnp.clip(jnp.repeat(nvalid, per) - (gblk % per) * MOE_SUB, 0, MOE_SUB).astype(jnp.int32)
    gvisit = lax.cummax(jnp.where(gcnt > 0, gblk, 0), axis=0)
    return dest.reshape(T, TOP_K), idx, blk_e, nused, nvalid, gcnt, gvisit


def _block_diag_in(bb):
    gpc = S5_GROUPS // S5_CHUNKS
    x = bb.reshape(S5_CHUNKS, gpc, S5_GROUP, S5_STATE)
    eye = jnp.eye(gpc, dtype=bb.dtype)
    return jnp.einsum('agcn,gh->agchn', x, eye).reshape(S5_CHUNKS, S5_CHUNK_IN, S5_CHUNK_STATE)


def _block_diag_out(cc):
    gpc = S5_GROUPS // S5_CHUNKS
    x = cc.reshape(S5_CHUNKS, gpc, S5_GROUP, S5_STATE)
    eye = jnp.eye(gpc, dtype=cc.dtype)
    return jnp.einsum('agcn,gh->ahngc', x, eye).reshape(S5_CHUNKS, S5_CHUNK_STATE, S5_CHUNK_IN)


def kernel(x_prompt, x_sample, c_prompt, c_sample, cache_k, cache_v, cache_logf, state_s5_re, state_s5_im, page_table, w_ada, b_ada, w_in, fox_b_f, s5_a_re, s5_a_im, s5_log_dt, s5_b_re, s5_b_im, s5_c_re, s5_c_im, s5_d, w_glu, b_glu, p_a, p_b, w_o, ln1_g, ln1_b, w_router, b_router, w_gu, b_gu, w_dn, b_dn, ln2_g, ln2_b):
    B, L, D = x_prompt.shape
    Bd, Ld, _ = x_sample.shape
    Tp, Ts = B * L, Bd * Ld
    W = HEADS * HEAD_DIM
    n_phys = cache_k.shape[1]

    n_c = B + Bd
    c_all = jnp.concatenate([c_prompt, c_sample, jnp.zeros((-n_c % 8, D), F32)], axis=0)
    mod = _adaln(c_all, w_ada[0], b_ada)
    mp = [mod[:B, i * D:(i + 1) * D].reshape(B, 1, D) for i in range(6)]
    ms = [jnp.repeat(mod[B:n_c, i * D:(i + 1) * D], Ld, axis=0) for i in range(6)]

    wi = w_in[0]
    f_lo = 3 * W
    wm = jnp.concatenate([wi[:, :f_lo], wi[:, f_lo + HEADS:]], axis=1).astype(BF16)
    wf = jnp.pad(wi[:, f_lo:f_lo + HEADS], ((0, 0), (0, LANES - HEADS))).astype(BF16)
    bf = jnp.pad(fox_b_f, ((0, 0), (0, LANES - HEADS)))
    hp, logf_p, cT_p = _in_proj(x_prompt.reshape(Tp, D), mp[0], mp[1], wm, wf, wf.T, bf, bf.T,
                                seq_len=L, seg=0)
    hs, logf_s, c_s = _in_proj(x_sample.reshape(Ts, D), ms[0], ms[1], wm, wf, wf.T, bf, bf.T,
                               seq_len=0, seg=Ld)

    lam_r, lam_i, bb_r, bb_i = _s5_params(s5_a_re[0], s5_a_im[0], s5_log_dt[0][:, None],
                                          s5_b_re[0].transpose(0, 2, 1), s5_b_im[0].transpose(0, 2, 1))
    lam_r = lam_r.reshape(S5_CHUNKS, S5_CHUNK_STATE)
    lam_i = lam_i.reshape(S5_CHUNKS, S5_CHUNK_STATE)
    wb = jnp.concatenate([_block_diag_in(bb_r), _block_diag_in(bb_i)], axis=2).astype(BF16)
    wcr = _block_diag_out(s5_c_re[0]).astype(BF16)
    wci = _block_diag_out(s5_c_im[0]).astype(BF16)
    wgl = w_glu[0].astype(BF16)
    s5_args = (lam_r, lam_i, wb, wcr, wci, s5_d, wgl, b_glu)
    zeros_state = jnp.zeros((B, S5_CHUNKS, S5_CHUNK_STATE), F32)
    s5_t = 256
    ssm_p, sr_p, si_p = _s5(hp, 3, zeros_state, zeros_state, *s5_args, nseq=1, slen=s5_t, chunks=L // s5_t)
    ssm_s, sr_s, si_s = _s5(hs, 3, state_s5_re[0].reshape(Bd, S5_CHUNKS, S5_CHUNK_STATE),
                            state_s5_im[0].reshape(Bd, S5_CHUNKS, S5_CHUNK_STATE), *s5_args,
                            nseq=32, slen=Ld, chunks=1)

    att_p = _fox_prompt(hp, cT_p, B, L)
    scale = HEAD_DIM ** -0.5
    q_s = hs[:, :W].reshape(Bd, Ld, HEADS, HEAD_DIM) * scale
    k_s = hs[:, W:2 * W].reshape(Bd, Ld, HEADS, HEAD_DIM)
    v_s = hs[:, 2 * W:3 * W].reshape(Bd, Ld, HEADS, HEAD_DIM)
    qe = q_s.reshape(Bd, Ld * HEADS, HEAD_DIM)
    qbd = jnp.einsum('bqgd,gh->bqghd', q_s, jnp.eye(HEADS, dtype=F32)).reshape(Bd, Ld * HEADS, W).astype(BF16)
    kn = jnp.broadcast_to(k_s[:, :, None], (Bd, Ld, Ld, HEADS, HEAD_DIM)).reshape(Bd, Ld, Ld * HEADS, HEAD_DIM)
    vn = jnp.broadcast_to(v_s[:, :, None], (Bd, Ld, Ld, HEADS, HEAD_DIM)).reshape(Bd, Ld, Ld * HEADS, HEAD_DIM)
    cn = c_s[:, :HEADS].reshape(Bd, Ld, HEADS)
    bn = jnp.broadcast_to(cn.transpose(0, 2, 1)[:, None], (Bd, Ld, HEADS, Ld)).reshape(Bd, Ld * HEADS, Ld)
    bn = jnp.pad(bn, ((0, 0), (0, 0), (0, LANES - Ld)))
    ckt = jnp.transpose(cache_k[0], (0, 2, 3, 1)).reshape(n_phys, W, PAGE)
    cvt = jnp.transpose(cache_v[0], (0, 2, 3, 1)).reshape(n_phys, W, PAGE)
    clft = jnp.transpose(cache_logf[0], (0, 2, 1))
    att_s = _fox_sample(page_table, ckt, cvt, clft, qbd, qe, kn, vn, bn)
    att_s = att_s.reshape(Ts, W).astype(BF16)

    pab, pbb, wob = p_a[0].astype(BF16), p_b[0].astype(BF16), w_o[0].astype(BF16)
    wr = jnp.pad(w_router[0], ((0, 0), (0, LANES - N_EXPERTS)))
    br = jnp.pad(b_router, ((0, 0), (0, LANES - N_EXPERTS)), constant_values=NEG)
    m_p = _mix_gate(att_p, ssm_p, hp, pab, pbb)
    m_s = _mix_gate(att_s, ssm_s, hs, pab, pbb)
    x1_p, u2_p, ri_p, rw_p = _mix_out(m_p, x_prompt.reshape(Tp, D), mp[2], mp[3], mp[4], wob, ln1_g, ln1_b,
                                      wr, br, seq_len=L)
    x1_s, u2_s, ri_s, rw_s = _mix_out(m_s, x_sample.reshape(Ts, D), ms[2], ms[3], ms[4], wob, ln1_g, ln1_b,
                                      wr, br, seq_len=0)

    u2 = jnp.concatenate([u2_p, u2_s], axis=0)
    experts = jnp.concatenate([ri_p[:, :TOP_K], ri_s[:, :TOP_K]], axis=0)
    dest, idx, blk_e, nused, nvalid, gcnt, gvisit = _route(experts)
    xs = _moe_gather(u2, idx, gcnt, gvisit)
    y = _moe_experts(xs, blk_e, nused, nvalid, w_gu, b_gu[0][:, None, :], w_dn, b_dn[0][:, None, :])
    x2_p = _moe_combine(y, dest[:Tp], x1_p, mp[5], rw_p, ln2_g, ln2_b, seq_len=L)
    x2_s = _moe_combine(y, dest[Tp:], x1_s, ms[5], rw_s, ln2_g, ln2_b, seq_len=0)

    def heads(t, n, l):
        return t.reshape(1, n, l, HEADS, HEAD_DIM)

    def state(s, n):
        return s.reshape(1, n, S5_GROUPS, S5_STATE)

    return (x2_p.reshape(B, L, D), x2_s.reshape(Bd, Ld, D),
            heads(hp[:, W:2 * W], B, L), heads(hp[:, 2 * W:3 * W], B, L),
            logf_p[:, :HEADS].reshape(1, B, L, HEADS), state(sr_p, B), state(si_p, B),
            heads(hs[:, W:2 * W], Bd, Ld), heads(hs[:, 2 * W:3 * W], Bd, Ld),
            logf_s[:, :HEADS].reshape(1, Bd, Ld, HEADS), state(sr_s, Bd), state(si_s, Bd))
```

```python
import functools
import math

import jax
import jax.numpy as jnp
from jax import lax
from jax.experimental import pallas as pl
from jax.experimental.pallas import tpu as pltpu

F32 = jnp.float32
BF16 = jnp.bfloat16

HEADS = 16
HEAD_DIM = 64
PAGE = 128
S5_GROUPS = 64
S5_GROUP = 16
S5_STATE = 64
N_EXPERTS = 32
TOP_K = 4
SWIGLU_LIMIT = 7.0
SWIGLU_ALPHA = 1.702
DN_ALPHA = 2.0 ** 0.25
LN_EPS = 1e-5
NEG = -1e30
LOG2E = math.log2(math.e)

LANES = 128
S5_CHUNKS = 8
S5_CHUNK_STATE = S5_GROUPS * S5_STATE // S5_CHUNKS
S5_CHUNK_IN = S5_GROUPS * S5_GROUP // S5_CHUNKS
S5_LANE_BLOCKS = S5_CHUNK_STATE // LANES
FOX_TQ = 1024
FOX_TK = 1024
MOE_BLOCK = 1536
MOE_SUB = 384
MOE_FT = 512
VMEM_LIMIT = 56 * 1024 * 1024
MOE_VMEM_LIMIT = 60 * 1024 * 1024


def _cp(sem, vmem=VMEM_LIMIT):
    return pltpu.CompilerParams(dimension_semantics=sem, vmem_limit_bytes=vmem)


def _dot(a, b):
    return jnp.dot(a, b, preferred_element_type=F32)


def _dot_nt(a, b):
    return lax.dot_general(a, b, (((1,), (1,)), ((), ())), preferred_element_type=F32)


def _log_sigmoid(x):
    return jnp.minimum(x, 0.0) - jnp.log1p(jnp.exp(-jnp.abs(x)))


def _split3(v):
    hi = v.astype(BF16)
    r = v - hi.astype(F32)
    mid = r.astype(BF16)
    lo = (r - mid.astype(F32)).astype(BF16)
    return hi, mid, lo


def _layernorm(v, g, b):
    mu = jnp.mean(v, axis=-1, keepdims=True)
    d = v - mu
    var = jnp.mean(d * d, axis=-1, keepdims=True)
    return d * lax.rsqrt(var + LN_EPS) * g + b


def _ada_kernel(c_ref, w_ref, b_ref, o_ref):
    c = c_ref[...]
    a = (c * jax.nn.sigmoid(c)).astype(BF16)
    o_ref[...] = _dot(a, w_ref[...].astype(BF16)) + b_ref[...]


def _adaln(c_all, w_ada, b_ada):
    R, D = c_all.shape
    N = w_ada.shape[1]
    tn = 1024
    return pl.pallas_call(
        _ada_kernel,
        grid=(N // tn,),
        in_specs=[pl.BlockSpec((R, D), lambda j: (0, 0)),
                  pl.BlockSpec((D, tn), lambda j: (0, j)),
                  pl.BlockSpec((1, tn), lambda j: (0, j))],
        out_specs=pl.BlockSpec((R, tn), lambda j: (0, j)),
        out_shape=jax.ShapeDtypeStruct((R, N), F32),
        compiler_params=_cp(("arbitrary",)),
        name="adaln",
    )(c_all, w_ada, b_ada)


def _in_kernel(x_ref, sh_ref, sc_ref, wm_ref, wf_ref, wft_ref, bf_ref, bft_ref,
               h_ref, logf_ref, cum_ref, u_scr, carry_scr, *, seq_tiles, seg):
    i = pl.program_id(0)
    j = pl.program_id(1)
    tm = x_ref.shape[0]

    @pl.when(j == 0)
    def _():
        u = x_ref[...] * (1.0 + sc_ref[...]) + sh_ref[...]
        ub = u.astype(BF16)
        u_scr[...] = ub
        lf = _log_sigmoid(_dot(ub, wf_ref[...]) + bf_ref[...])
        logf_ref[...] = lf
        r = lax.broadcasted_iota(jnp.int32, (tm, tm), 0)
        c = lax.broadcasted_iota(jnp.int32, (tm, tm), 1)
        if seq_tiles > 0:
            lft = _log_sigmoid(_dot_nt(wft_ref[...], ub) + bft_ref[...])
            tri = (r <= c).astype(BF16)
            hi, mid, lo = _split3(lft)
            cs = _dot(hi, tri) + _dot(mid, tri) + _dot(lo, tri)

            @pl.when(i % seq_tiles == 0)
            def _():
                carry_scr[...] = jnp.zeros_like(carry_scr)

            cs = cs + carry_scr[...]
            cum_ref[...] = cs
            carry_scr[...] = cs[:, tm - 1:tm]
        else:
            tri = ((c <= r) & (c // seg == r // seg)).astype(BF16)
            hi, mid, lo = _split3(lf)
            cum_ref[...] = _dot(tri, hi) + _dot(tri, mid) + _dot(tri, lo)

    h_ref[...] = _dot(u_scr[...], wm_ref[...])


def _in_proj(x, shift, scale, wm, wf, wft, bf, bft, *, seq_len, seg):
    T, D = x.shape
    N = wm.shape[1]
    tm, tn = min(512, T), 1024
    nt = T // tm
    if seq_len > 0:
        seq_tiles = seq_len // tm
        mod_spec = pl.BlockSpec((None, 1, D), lambda i, j: (i // seq_tiles, 0, 0))
        cum_shape, cum_spec = (LANES, T), pl.BlockSpec((LANES, tm), lambda i, j: (0, i))
    else:
        seq_tiles = 0
        mod_spec = pl.BlockSpec((tm, D), lambda i, j: (i, 0))
        cum_shape, cum_spec = (T, LANES), pl.BlockSpec((tm, LANES), lambda i, j: (i, 0))
    kern = functools.partial(_in_kernel, seq_tiles=seq_tiles, seg=seg)
    return pl.pallas_call(
        kern,
        grid=(nt, N // tn),
        in_specs=[pl.BlockSpec((tm, D), lambda i, j: (i, 0)), mod_spec, mod_spec,
                  pl.BlockSpec((D, tn), lambda i, j: (0, j)),
                  pl.BlockSpec((D, LANES), lambda i, j: (0, 0)),
                  pl.BlockSpec((LANES, D), lambda i, j: (0, 0)),
                  pl.BlockSpec((1, LANES), lambda i, j: (0, 0)),
                  pl.BlockSpec((LANES, 1), lambda i, j: (0, 0))],
        out_specs=[pl.BlockSpec((tm, tn), lambda i, j: (i, j)),
                   pl.BlockSpec((tm, LANES), lambda i, j: (i, 0)),
                   cum_spec],
        out_shape=[jax.ShapeDtypeStruct((T, N), F32),
                   jax.ShapeDtypeStruct((T, LANES), F32),
                   jax.ShapeDtypeStruct(cum_shape, F32)],
        scratch_shapes=[pltpu.VMEM((tm, D), BF16), pltpu.VMEM((LANES, 1), F32)],
        compiler_params=_cp(("arbitrary", "arbitrary")),
        name="in_proj",
    )(x, shift, scale, wm, wf, wft, bf, bft)


def _fox_prompt_kernel(qt_ref, kt_ref, q_ref, k_ref, v_ref, c_ref, o_ref, qs, m_scr, l_scr, acc, *, ratio):
    hp = pl.program_id(1)
    t = pl.program_id(2)
    qi = qt_ref[t]
    kv = kt_ref[t]
    tq, tk = q_ref.shape[0], k_ref.shape[0]
    lane = lax.broadcasted_iota(jnp.int32, (tq, LANES), 1)

    @pl.when(kv == 0)
    def _():
        q2 = q_ref[...] * (HEAD_DIM ** -0.5 * LOG2E)
        qs[0] = jnp.where(lane < HEAD_DIM, q2, 0.0).astype(BF16)
        qs[1] = jnp.where(lane >= HEAD_DIM, q2, 0.0).astype(BF16)
        m_scr[...] = jnp.full_like(m_scr, NEG)
        l_scr[...] = jnp.zeros_like(l_scr)
        acc[...] = jnp.zeros_like(acc)

    def block(diagonal):
        kb = k_ref[...].astype(BF16)
        vb = v_ref[...].astype(BF16)
        r0 = (2 * hp) % 8
        for a in range(2):
            crow = c_ref[pl.ds(r0 + a, 1), :] * LOG2E
            s = _dot_nt(qs[a], kb) - crow
            if diagonal:
                row = lax.broadcasted_iota(jnp.int32, (tq, tk), 0)
                col = lax.broadcasted_iota(jnp.int32, (tq, tk), 1) + (kv * tk - qi * tq)
                s = jnp.where(col <= row, s, NEG)
            cols = [s[:, c * LANES:(c + 1) * LANES] for c in range(tk // LANES)]
            mx = cols[0]
            for sc in cols[1:]:
                mx = jnp.maximum(mx, sc)
            m_prev = m_scr[a]
            m_new = jnp.maximum(m_prev, jnp.max(mx, axis=1, keepdims=True))
            alpha = jnp.exp2(m_prev - m_new)
            ps = [jnp.exp2(sc - m_new) for sc in cols]
            lsum = ps[0]
            for pc in ps[1:]:
                lsum = lsum + pc
            l_scr[a] = alpha * l_scr[a] + lsum
            p = jnp.concatenate(ps, axis=1).astype(BF16)
            acc[a] = alpha * acc[a] + _dot(p, vb)
            m_scr[a] = m_new

    @pl.when(kv < qi * ratio)
    def _():
        block(False)

    @pl.when(kv >= qi * ratio)
    def _():
        block(True)

    @pl.when(kv == (qi + 1) * ratio - 1)
    def _():
        l0 = jnp.sum(l_scr[0], axis=1, keepdims=True)
        l1 = jnp.sum(l_scr[1], axis=1, keepdims=True)
        o = jnp.where(lane < HEAD_DIM, acc[0] / l0, acc[1] / l1)
        o_ref[...] = o.astype(o_ref.dtype)


def _fox_prompt(h, cT, B, L):
    tq, tk = FOX_TQ, FOX_TK
    ratio = tq // tk
    nq, nk = L // tq, L // tk
    npairs = HEADS // 2
    steps = [(qi, kv) for qi in range(nq) for kv in range((qi + 1) * ratio)]
    qt = jnp.asarray([p[0] for p in steps], jnp.int32)
    kt = jnp.asarray([p[1] for p in steps], jnp.int32)
    grid_spec = pltpu.PrefetchScalarGridSpec(
        num_scalar_prefetch=2,
        grid=(B, npairs, len(steps)),
        in_specs=[
            pl.BlockSpec((tq, LANES), lambda b, hp, t, qt, kt: (b * nq + qt[t], hp)),
            pl.BlockSpec((tk, LANES), lambda b, hp, t, qt, kt: (b * nk + kt[t], npairs + hp)),
            pl.BlockSpec((tk, LANES), lambda b, hp, t, qt, kt: (b * nk + kt[t], 2 * npairs + hp)),
            pl.BlockSpec((8, tk), lambda b, hp, t, qt, kt: (hp // 4, b * nk + kt[t])),
        ],
        out_specs=pl.BlockSpec((tq, LANES), lambda b, hp, t, qt, kt: (b * nq + qt[t], hp)),
        scratch_shapes=[pltpu.VMEM((2, tq, LANES), BF16), pltpu.VMEM((2, tq, LANES), F32),
                        pltpu.VMEM((2, tq, LANES), F32), pltpu.VMEM((2, tq, LANES), F32)],
    )
    return pl.pallas_call(
        functools.partial(_fox_prompt_kernel, ratio=ratio),
        grid_spec=grid_spec,
        out_shape=jax.ShapeDtypeStruct((B * L, HEADS * HEAD_DIM), BF16),
        compiler_params=_cp(("parallel", "parallel", "arbitrary")),
        name="fox_prompt",
    )(qt, kt, h, h, h, cT)


def _fox_sample_kernel(pt_ref, *refs, n_pages):
    k_refs = refs[:n_pages]
    v_refs = refs[n_pages:2 * n_pages]
    lf_refs = refs[2 * n_pages:3 * n_pages]
    qbd_ref, qe_ref, kn_ref, vn_ref, bn_ref, o_ref = refs[3 * n_pages:]
    nq = qe_ref.shape[0] // HEADS
    nrow = nq * HEADS
    width = HEADS * HEAD_DIM

    xs = jnp.concatenate([r[...] for r in lf_refs], axis=0)
    hi, mid, lo = _split3(xs)
    after = (lax.broadcasted_iota(jnp.int32, (PAGE, PAGE), 0)
             > lax.broadcasted_iota(jnp.int32, (PAGE, PAGE), 1)).astype(BF16)
    within = _dot(hi, after) + _dot(mid, after) + _dot(lo, after)
    n = n_pages * HEADS
    pr = lax.broadcasted_iota(jnp.int32, (n, n), 0)
    pc = lax.broadcasted_iota(jnp.int32, (n, n), 1)
    later = ((pc // HEADS > pr // HEADS) & (pc % HEADS == pr % HEADS)).astype(BF16)
    beyond = jnp.sum(_dot(later, hi) + _dot(later, mid) + _dot(later, lo), axis=1, keepdims=True)
    bias_rows = within + beyond
    bias = jnp.concatenate([bias_rows[j * HEADS:(j + 1) * HEADS] for j in range(n_pages)], axis=1)
    bias = jnp.concatenate([bias] * nq, axis=0)

    kt = jnp.concatenate([r[...].astype(BF16) for r in k_refs], axis=1)
    s = _dot(qbd_ref[...], kt) + bias

    q_of_row = lax.broadcasted_iota(jnp.int32, (nrow, 1), 0) // HEADS
    qe = qe_ref[...]
    bn = bn_ref[...]
    sn = []
    for j in range(nq):
        sj = jnp.sum(qe * kn_ref[j], axis=1, keepdims=True) - bn[:, j:j + 1]
        sn.append(jnp.where(q_of_row >= j, sj, NEG))
    m = jnp.max(s, axis=1, keepdims=True)
    for sj in sn:
        m = jnp.maximum(m, sj)
    p = jnp.exp(s - m)
    l = jnp.sum(p, axis=1, keepdims=True)

    vt = jnp.concatenate([r[...].astype(BF16) for r in v_refs], axis=1)
    o_all = _dot_nt(p.astype(BF16), vt)
    rh = lax.broadcasted_iota(jnp.int32, (nrow, width), 0) % HEADS
    ch = lax.broadcasted_iota(jnp.int32, (nrow, width), 1) // HEAD_DIM
    o_hi, o_mid, o_lo = _split3(jnp.where(rh == ch, o_all, 0.0))
    fold = (lax.broadcasted_iota(jnp.int32, (width, HEAD_DIM), 0) % HEAD_DIM
            == lax.broadcasted_iota(jnp.int32, (width, HEAD_DIM), 1)).astype(BF16)
    o = _dot(o_hi, fold) + _dot(o_mid, fold) + _dot(o_lo, fold)
    for j in range(nq):
        pj = jnp.exp(sn[j] - m)
        l = l + pj
        o = o + pj * vn_ref[j]
    o_ref[...] = o / l


def _fox_sample(page_table, ckt, cvt, clft, qbd, qe, kn, vn, bn):
    Bd, n_pages = page_table.shape
    nrow = qe.shape[1]
    nq = nrow // HEADS
    width = HEADS * HEAD_DIM

    def page_map(j):
        return lambda b, pt: (pt[b * n_pages + j], 0, 0)

    per_row3 = lambda b, pt: (b, 0, 0)
    per_row4 = lambda b, pt: (b, 0, 0, 0)
    grid_spec = pltpu.PrefetchScalarGridSpec(
        num_scalar_prefetch=1,
        grid=(Bd,),
        in_specs=([pl.BlockSpec((None, width, PAGE), page_map(j)) for j in range(n_pages)]
                  + [pl.BlockSpec((None, width, PAGE), page_map(j)) for j in range(n_pages)]
                  + [pl.BlockSpec((None, HEADS, PAGE), page_map(j)) for j in range(n_pages)]
                  + [pl.BlockSpec((None, nrow, width), per_row3),
                     pl.BlockSpec((None, nrow, HEAD_DIM), per_row3),
                     pl.BlockSpec((None, nq, nrow, HEAD_DIM), per_row4),
                     pl.BlockSpec((None, nq, nrow, HEAD_DIM), per_row4),
                     pl.BlockSpec((None, nrow, LANES), per_row3)]),
        out_specs=pl.BlockSpec((None, nrow, HEAD_DIM), per_row3),
    )
    return pl.pallas_call(
        functools.partial(_fox_sample_kernel, n_pages=n_pages),
        grid_spec=grid_spec,
        out_shape=jax.ShapeDtypeStruct((Bd, nrow, HEAD_DIM), F32),
        compiler_params=_cp(("parallel",)),
        name="fox_sample",
    )(page_table.reshape(-1), *([ckt] * n_pages), *([cvt] * n_pages), *([clft] * n_pages),
      qbd, qe, kn, vn, bn)


def _s5_param_kernel(ar_ref, ai_ref, ldt_ref, br_ref, bi_ref, lr_ref, li_ref, bbr_ref, bbi_ref):
    ar, ai = ar_ref[...], ai_ref[...]
    dt = jnp.exp(ldt_ref[...])
    mag = jnp.exp(ar * dt)
    lr = mag * jnp.cos(ai * dt)
    li = mag * jnp.sin(ai * dt)
    den = ar * ar + ai * ai
    zr = ((lr - 1.0) * ar + li * ai) / den
    zi = (li * ar - (lr - 1.0) * ai) / den
    lr_ref[...] = lr
    li_ref[...] = li
    br, bi = br_ref[...], bi_ref[...]
    zr3, zi3 = zr[:, None, :], zi[:, None, :]
    bbr_ref[...] = zr3 * br - zi3 * bi
    bbi_ref[...] = zr3 * bi + zi3 * br


def _s5_params(a_re, a_im, log_dt, bt_re, bt_im):
    G, N = a_re.shape
    C = bt_re.shape[1]
    return pl.pallas_call(
        _s5_param_kernel,
        out_shape=[jax.ShapeDtypeStruct((G, N), F32), jax.ShapeDtypeStruct((G, N), F32),
                   jax.ShapeDtypeStruct((G, C, N), F32), jax.ShapeDtypeStruct((G, C, N), F32)],
        name="s5_params",
    )(a_re, a_im, log_dt, bt_re, bt_im)


def _s5_kernel(u_ref, h0r_ref, h0i_ref, lr_ref, li_ref, wb_ref, wcr_ref, wci_ref, d_ref, wg_ref, bg_ref,
               o_ref, sr_ref, si_ref, bur, bui, y_scr, cr, ci, *, nseq, slen):
    c_id = pl.program_id(1)
    R = u_ref.shape[0]
    u = u_ref[...]
    ub = u.astype(BF16)
    rs = [_dot(ub[:, gc * S5_CHUNK_IN:(gc + 1) * S5_CHUNK_IN], wb_ref[gc]) for gc in range(S5_CHUNKS)]
    for lb in range(S5_LANE_BLOCKS):
        lo = lb * LANES
        xr = jnp.stack([r[:, lo:lo + LANES] for r in rs], axis=0)
        xi = jnp.stack([r[:, S5_CHUNK_STATE + lo:S5_CHUNK_STATE + lo + LANES] for r in rs], axis=0)
        bur[lb] = jnp.swapaxes(xr, 0, 1)
        bui[lb] = jnp.swapaxes(xi, 0, 1)

    @pl.when(c_id == 0)
    def _():
        cr[...] = h0r_ref[...]
        ci[...] = h0i_ref[...]

    lr = lr_ref[...]
    li = li_ref[...]

    def seq_body(q, _):
        base = q * slen

        def step(t, hc):
            hr, hi = hc
            br = jnp.concatenate([bur[lb, base + t] for lb in range(S5_LANE_BLOCKS)], axis=1)
            bi = jnp.concatenate([bui[lb, base + t] for lb in range(S5_LANE_BLOCKS)], axis=1)
            nr = lr * hr - li * hi + br
            ni = lr * hi + li * hr + bi
            for lb in range(S5_LANE_BLOCKS):
                bur[lb, base + t] = nr[:, lb * LANES:(lb + 1) * LANES]
                bui[lb, base + t] = ni[:, lb * LANES:(lb + 1) * LANES]
            return nr, ni

        hr, hi = lax.fori_loop(0, slen, step, (cr[q], ci[q]), unroll=4)
        cr[q] = hr
        ci[q] = hi
        return 0

    lax.fori_loop(0, nseq, seq_body, 0)
    sr_ref[...] = cr[...]
    si_ref[...] = ci[...]

    hr_t = [jnp.swapaxes(bur[lb], 0, 1) for lb in range(S5_LANE_BLOCKS)]
    hi_t = [jnp.swapaxes(bui[lb], 0, 1) for lb in range(S5_LANE_BLOCKS)]
    for gc in range(S5_CHUNKS):
        hrb = jnp.concatenate([x[gc] for x in hr_t], axis=1).astype(BF16)
        hib = jnp.concatenate([x[gc] for x in hi_t], axis=1).astype(BF16)
        y_scr[:, gc * S5_CHUNK_IN:(gc + 1) * S5_CHUNK_IN] = _dot(hrb, wcr_ref[gc]) - _dot(hib, wci_ref[gc])
    y = y_scr[...] + d_ref[...] * u
    z = jax.nn.gelu(y)
    o_ref[...] = (z * jax.nn.sigmoid(_dot(z.astype(BF16), wg_ref[...]) + bg_ref[...])).astype(o_ref.dtype)


def _s5(h, col_block, h0r, h0i, lam_r, lam_i, wb, wcr, wci, d_skip, w_glu, b_glu, *, nseq, slen, chunks):
    T = h.shape[0]
    W = S5_GROUPS * S5_GROUP
    R = nseq * slen
    nsb = T // (R * chunks)
    n_seq_total = h0r.shape[0]
    st_spec = pl.BlockSpec((nseq, S5_CHUNKS, S5_CHUNK_STATE), lambda sb, c: (sb, 0, 0))
    full3 = lambda sb, c: (0, 0, 0)
    full2 = lambda sb, c: (0, 0)
    kern = functools.partial(_s5_kernel, nseq=nseq, slen=slen)
    return pl.pallas_call(
        kern,
        grid=(nsb, chunks),
        in_specs=[pl.BlockSpec((R, W), lambda sb, c: (sb * chunks + c, col_block)),
                  st_spec, st_spec,
                  pl.BlockSpec((S5_CHUNKS, S5_CHUNK_STATE), full2),
                  pl.BlockSpec((S5_CHUNKS, S5_CHUNK_STATE), full2),
                  pl.BlockSpec((S5_CHUNKS, S5_CHUNK_IN, 2 * S5_CHUNK_STATE), full3),
                  pl.BlockSpec((S5_CHUNKS, S5_CHUNK_STATE, S5_CHUNK_IN), full3),
                  pl.BlockSpec((S5_CHUNKS, S5_CHUNK_STATE, S5_CHUNK_IN), full3),
                  pl.BlockSpec((1, W), full2),
                  pl.BlockSpec((W, W), full2),
                  pl.BlockSpec((1, W), full2)],
        out_specs=[pl.BlockSpec((R, W), lambda sb, c: (sb * chunks + c, 0)), st_spec, st_spec],
        out_shape=[jax.ShapeDtypeStruct((T, W), BF16),
                   jax.ShapeDtypeStruct((n_seq_total, S5_CHUNKS, S5_CHUNK_STATE), F32),
                   jax.ShapeDtypeStruct((n_seq_total, S5_CHUNKS, S5_CHUNK_STATE), F32)],
        scratch_shapes=[pltpu.VMEM((S5_LANE_BLOCKS, R, S5_CHUNKS, LANES), F32),
                        pltpu.VMEM((S5_LANE_BLOCKS, R, S5_CHUNKS, LANES), F32),
                        pltpu.VMEM((R, W), F32),
                        pltpu.VMEM((nseq, S5_CHUNKS, S5_CHUNK_STATE), F32),
                        pltpu.VMEM((nseq, S5_CHUNKS, S5_CHUNK_STATE), F32)],
        compiler_params=_cp(("arbitrary", "arbitrary")),
        name="s5",
    )(h, h0r, h0i, lam_r, lam_i, wb, wcr, wci, d_skip, w_glu, b_glu)


def _mix_gate_kernel(att_ref, ssm_ref, ga_ref, gb_ref, pa_ref, pb_ref, o_ref):
    m = (jax.nn.sigmoid(ga_ref[...]) * _dot(att_ref[...], pa_ref[...])
         + jax.nn.sigmoid(gb_ref[...]) * _dot(ssm_ref[...], pb_ref[...]))
    o_ref[...] = m.astype(o_ref.dtype)


def _mix_gate(att, ssm, h, p_a, p_b):
    T, W = att.shape
    D = p_a.shape[1]
    tm = min(256, T)
    return pl.pallas_call(
        _mix_gate_kernel,
        grid=(T // tm,),
        in_specs=[pl.BlockSpec((tm, W), lambda i: (i, 0)),
                  pl.BlockSpec((tm, W), lambda i: (i, 0)),
                  pl.BlockSpec((tm, D), lambda i: (i, 2)),
                  pl.BlockSpec((tm, D), lambda i: (i, 3)),
                  pl.BlockSpec((W, D), lambda i: (0, 0)),
                  pl.BlockSpec((W, D), lambda i: (0, 0))],
        out_specs=pl.BlockSpec((tm, D), lambda i: (i, 0)),
        out_shape=jax.ShapeDtypeStruct((T, D), BF16),
        compiler_params=_cp(("parallel",)),
        name="mix_gate",
    )(att, ssm, h, h, p_a, p_b)


def _mix_out_kernel(m_ref, x_ref, g1_ref, sh2_ref, sc2_ref, wo_ref, lg_ref, lb_ref, wr_ref, br_ref,
                    x1_ref, u2_ref, ri_ref, rw_ref):
    out = _dot(m_ref[...], wo_ref[...])
    x1 = _layernorm(DN_ALPHA * x_ref[...] + g1_ref[...] * out, lg_ref[...], lb_ref[...])
    x1_ref[...] = x1
    u2 = x1 * (1.0 + sc2_ref[...]) + sh2_ref[...]
    u2_ref[...] = u2
    u_hi = u2.astype(BF16)
    u_lo = (u2 - u_hi.astype(F32)).astype(BF16)
    wr = wr_ref[...]
    w_hi = wr.astype(BF16)
    w_lo = (wr - w_hi.astype(F32)).astype(BF16)
    logits = _dot(u_hi, w_hi) + _dot(u_hi, w_lo) + _dot(u_lo, w_hi) + br_ref[...]
    lane = lax.broadcasted_iota(jnp.int32, logits.shape, 1)
    cur = logits
    vals, idxs = [], []
    for _ in range(TOP_K):
        mk = jnp.max(cur, axis=1, keepdims=True)
        ik = jnp.min(jnp.where(cur == mk, lane, LANES), axis=1, keepdims=True)
        vals.append(mk)
        idxs.append(ik)
        cur = jnp.where(lane == ik, -jnp.inf, cur)
    es = [jnp.exp(v - vals[0]) for v in vals]
    den = es[0] + es[1] + es[2] + es[3]
    ri = jnp.zeros(logits.shape, jnp.int32)
    rw = jnp.zeros(logits.shape, F32)
    for k in range(TOP_K):
        ri = jnp.where(lane == k, idxs[k], ri)
        rw = jnp.where(lane == k, es[k] / den, rw)
    ri_ref[...] = ri
    rw_ref[...] = rw


def _mix_out(m, x, g1, sh2, sc2, w_o, ln_g, ln_b, w_r, b_r, *, seq_len):
    T, D = x.shape
    tm = min(256, T)
    if seq_len > 0:
        seq_tiles = seq_len // tm
        mod_spec = pl.BlockSpec((None, 1, D), lambda i: (i // seq_tiles, 0, 0))
    else:
        mod_spec = pl.BlockSpec((tm, D), lambda i: (i, 0))
    row = pl.BlockSpec((tm, D), lambda i: (i, 0))
    vec = pl.BlockSpec((1, D), lambda i: (0, 0))
    small = pl.BlockSpec((tm, LANES), lambda i: (i, 0))
    return pl.pallas_call(
        _mix_out_kernel,
        grid=(T // tm,),
        in_specs=[row, row, mod_spec, mod_spec, mod_spec,
                  pl.BlockSpec((D, D), lambda i: (0, 0)), vec, vec,
                  pl.BlockSpec((D, LANES), lambda i: (0, 0)),
                  pl.BlockSpec((1, LANES), lambda i: (0, 0))],
        out_specs=[row, row, small, small],
        out_shape=[jax.ShapeDtypeStruct((T, D), F32), jax.ShapeDtypeStruct((T, D), F32),
                   jax.ShapeDtypeStruct((T, LANES), jnp.int32), jax.ShapeDtypeStruct((T, LANES), F32)],
        compiler_params=_cp(("parallel",)),
        name="mix_out",
    )(m, x, g1, sh2, sc2, w_o, ln_g, ln_b, w_r, b_r)


def _gather_kernel(cnt_ref, blk_ref, idx_ref, idxn_ref, x_hbm, o_ref, buf, sem):
    j = pl.program_id(0)
    nb = pl.num_programs(0)
    slot = j % 2
    jn = jnp.minimum(j + 1, nb - 1)

    def rows8(step):
        return pl.multiple_of((cnt_ref[step] + 7) // 8 * 8, 8)

    def issue_block(idx_r, n8, sl):
        def issue(r, _):
            tok = idx_r[0, 0, r]
            pltpu.make_async_copy(x_hbm.at[pl.ds(tok, 1), :], buf.at[sl, pl.ds(r, 1), :], sem.at[sl]).start()
            return 0

        lax.fori_loop(0, n8, issue, 0)

    @pl.when(j == 0)
    def _():
        buf[...] = jnp.zeros_like(buf)
        issue_block(idx_ref, rows8(0), 0)

    @pl.when((j + 1 < nb) & (cnt_ref[jn] > 0))
    def _():
        issue_block(idxn_ref, rows8(jn), 1 - slot)

    @pl.when(cnt_ref[j] > 0)
    def _():
        n8 = rows8(j)
        pltpu.make_async_copy(x_hbm.at[pl.ds(0, n8), :], buf.at[slot, pl.ds(0, n8), :], sem.at[slot]).wait()
        o_ref[...] = buf[slot].astype(o_ref.dtype)


def _moe_gather(x, idx, cnt, blk):
    T, D = x.shape
    P = idx.shape[0]
    gb = MOE_SUB
    nb = P // gb
    grid_spec = pltpu.PrefetchScalarGridSpec(
        num_scalar_prefetch=2,
        grid=(nb,),
        in_specs=[pl.BlockSpec((1, 1, gb), lambda j, cnt, blk: (blk[j], 0, 0), memory_space=pltpu.SMEM),
                  pl.BlockSpec((1, 1, gb), lambda j, cnt, blk: (blk[jnp.minimum(j + 1, nb - 1)], 0, 0),
                               memory_space=pltpu.SMEM),
                  pl.BlockSpec(memory_space=pl.ANY)],
        out_specs=pl.BlockSpec((gb, D), lambda j, cnt, blk: (blk[j], 0)),
        scratch_shapes=[pltpu.VMEM((2, gb, D), F32), pltpu.SemaphoreType.DMA((2,))],
    )
    idx3 = idx.reshape(nb, 1, gb)
    return pl.pallas_call(
        _gather_kernel,
        grid_spec=grid_spec,
        out_shape=jax.ShapeDtypeStruct((P, D), BF16),
        compiler_params=pltpu.CompilerParams(dimension_semantics=("arbitrary",), vmem_limit_bytes=VMEM_LIMIT,
                                             disable_bounds_checks=True),
        name="moe_gather",
    )(cnt, blk, idx3, idx3, x)


def _moe_kernel(be_ref, nu_ref, nv_ref, x_ref, wg_ref, wu_ref, wd_ref, bg_ref, bu_ref, bd_ref, y_ref,
                a_scr):
    i = pl.program_id(0)
    s = pl.program_id(1)
    nf = pl.num_programs(1) // 2
    valid = nv_ref[i]

    @pl.when((s < nf) & (valid > 0))
    def _():
        for sbk in range(MOE_BLOCK // MOE_SUB):
            @pl.when(valid > sbk * MOE_SUB)
            def _():
                rows = pl.ds(sbk * MOE_SUB, MOE_SUB)
                xs = x_ref[rows, :].astype(F32)
                gate = jnp.minimum(_dot(xs, wg_ref[...]) + bg_ref[...], SWIGLU_LIMIT)
                up = jnp.clip(_dot(xs, wu_ref[...]) + bu_ref[...], -SWIGLU_LIMIT, SWIGLU_LIMIT)
                a_scr[s, rows, :] = ((up + 1.0) * (gate * jax.nn.sigmoid(SWIGLU_ALPHA * gate))).astype(BF16)

    @pl.when((s >= nf) & (valid > 0))
    def _():
        for sbk in range(MOE_BLOCK // MOE_SUB):
            rows = pl.ds(sbk * MOE_SUB, MOE_SUB)

            @pl.when(valid > sbk * MOE_SUB)
            def _():
                a = jnp.concatenate([a_scr[f, rows, :] for f in range(a_scr.shape[0])], axis=1)
                y_ref[rows, :] = _dot(a.astype(F32), wd_ref[...]) + bd_ref[...]

            @pl.when(valid <= sbk * MOE_SUB)
            def _():
                y_ref[rows, :] = jnp.zeros((MOE_SUB, y_ref.shape[1]), F32)


def _moe_experts(xs, blk_e, nused, nvalid, w_gu, b_gu, w_dn, b_dn):
    P, D = xs.shape
    F = w_dn.shape[2]
    NB = P // MOE_BLOCK
    NF = F // MOE_FT
    ND = D // MOE_FT

    def eff(i, nu):
        return jnp.minimum(i, nu[0] - 1)

    def fa(i, s, nu):
        return jnp.where(i < nu[0], jnp.minimum(s, NF - 1), NF - 1)

    def fb(i, s, nu):
        return jnp.where(i < nu[0], jnp.maximum(s - NF, 0), ND - 1)

    grid_spec = pltpu.PrefetchScalarGridSpec(
        num_scalar_prefetch=3,
        grid=(NB, NF + ND),
        in_specs=[
            pl.BlockSpec((MOE_BLOCK, D), lambda i, s, be, nu, nv: (eff(i, nu), 0)),
            pl.BlockSpec((None, None, D, MOE_FT), lambda i, s, be, nu, nv: (0, be[eff(i, nu)], 0, fa(i, s, nu))),
            pl.BlockSpec((None, None, D, MOE_FT),
                         lambda i, s, be, nu, nv: (0, be[eff(i, nu)], 0, NF + fa(i, s, nu))),
            pl.BlockSpec((None, None, F, MOE_FT), lambda i, s, be, nu, nv: (0, be[eff(i, nu)], 0, fb(i, s, nu))),
            pl.BlockSpec((None, 1, MOE_FT), lambda i, s, be, nu, nv: (be[eff(i, nu)], 0, fa(i, s, nu))),
            pl.BlockSpec((None, 1, MOE_FT), lambda i, s, be, nu, nv: (be[eff(i, nu)], 0, NF + fa(i, s, nu))),
            pl.BlockSpec((None, 1, MOE_FT), lambda i, s, be, nu, nv: (be[eff(i, nu)], 0, fb(i, s, nu))),
        ],
        out_specs=pl.BlockSpec((MOE_BLOCK, MOE_FT), lambda i, s, be, nu, nv: (eff(i, nu), fb(i, s, nu))),
        scratch_shapes=[pltpu.VMEM((NF, MOE_BLOCK, MOE_FT), BF16)],
    )
    assert NF == ND
    return pl.pallas_call(
        _moe_kernel,
        grid_spec=grid_spec,
        out_shape=jax.ShapeDtypeStruct((P, D), F32),
        compiler_params=_cp(("arbitrary", "arbitrary"), MOE_VMEM_LIMIT),
        name="moe_experts",
    )(blk_e, nused, nvalid, xs, w_gu, w_gu, w_dn, b_gu, b_gu, b_dn)


def _combine_kernel(idx_ref, idxn_ref, y_hbm, x1_ref, g2_ref, rw_ref, lg_ref, lb_ref, o_ref, buf, sem):
    i = pl.program_id(0)
    nt = pl.num_programs(0)
    slot = i % 2
    tm = x1_ref.shape[0]

    def issue_tile(idx_r, sl):
        def issue(r, _):
            for k in range(TOP_K):
                d = idx_r[0, 0, k * tm + r]
                pltpu.make_async_copy(y_hbm.at[pl.ds(d, 1), :], buf.at[sl, k, pl.ds(r, 1), :], sem.at[sl]).start()
            return 0

        lax.fori_loop(0, tm, issue, 0, unroll=2)

    @pl.when(i == 0)
    def _():
        issue_tile(idx_ref, 0)

    @pl.when(i + 1 < nt)
    def _():
        issue_tile(idxn_ref, 1 - slot)

    for k in range(TOP_K):
        pltpu.make_async_copy(y_hbm.at[pl.ds(0, tm), :], buf.at[slot, k], sem.at[slot]).wait()
    rw = rw_ref[...]
    moe = rw[:, 0:1] * buf[slot, 0]
    for k in range(1, TOP_K):
        moe = moe + rw[:, k:k + 1] * buf[slot, k]
    o_ref[...] = _layernorm(DN_ALPHA * x1_ref[...] + g2_ref[...] * moe, lg_ref[...], lb_ref[...])


def _moe_combine(y, dest, x1, g2, rw, ln_g, ln_b, *, seq_len):
    T, D = x1.shape
    tm = 128
    nt = T // tm
    idx = dest.reshape(nt, tm, TOP_K).transpose(0, 2, 1).reshape(nt, 1, TOP_K * tm)
    if seq_len > 0:
        seq_tiles = seq_len // tm
        mod_spec = pl.BlockSpec((None, 1, D), lambda i: (i // seq_tiles, 0, 0))
    else:
        mod_spec = pl.BlockSpec((tm, D), lambda i: (i, 0))
    row = pl.BlockSpec((tm, D), lambda i: (i, 0))
    vec = pl.BlockSpec((1, D), lambda i: (0, 0))
    return pl.pallas_call(
        _combine_kernel,
        grid=(nt,),
        in_specs=[pl.BlockSpec((1, 1, TOP_K * tm), lambda i: (i, 0, 0), memory_space=pltpu.SMEM),
                  pl.BlockSpec((1, 1, TOP_K * tm), lambda i: (jnp.minimum(i + 1, nt - 1), 0, 0),
                               memory_space=pltpu.SMEM),
                  pl.BlockSpec(memory_space=pl.ANY),
                  row, mod_spec, pl.BlockSpec((tm, LANES), lambda i: (i, 0)), vec, vec],
        out_specs=row,
        out_shape=jax.ShapeDtypeStruct((T, D), F32),
        scratch_shapes=[pltpu.VMEM((2, TOP_K, tm, D), F32), pltpu.SemaphoreType.DMA((2,))],
        compiler_params=pltpu.CompilerParams(dimension_semantics=("arbitrary",), vmem_limit_bytes=VMEM_LIMIT,
                                             disable_bounds_checks=True),
        name="moe_combine",
    )(idx, idx, y, x1, g2, rw, ln_g, ln_b)


def _route(experts):
    T = experts.shape[0]
    A = T * TOP_K
    flat_e = experts.reshape(A)
    onehot = (flat_e[:, None] == jnp.arange(N_EXPERTS, dtype=jnp.int32)[None, :]).astype(jnp.int32)
    counts = jnp.sum(onehot, axis=0)
    rank = jnp.sum((jnp.cumsum(onehot, axis=0) - onehot) * onehot, axis=1)
    nblk = (counts + MOE_BLOCK - 1) // MOE_BLOCK
    blk_end = jnp.cumsum(nblk)
    blk_start = blk_end - nblk
    dest = blk_start[flat_e] * MOE_BLOCK + rank
    NB = -(-A // MOE_BLOCK) + N_EXPERTS
    idx = jnp.zeros((NB * MOE_BLOCK,), jnp.int32).at[dest].set(jnp.arange(A, dtype=jnp.int32) // TOP_K)
    blk_e = jnp.minimum(jnp.searchsorted(blk_end, jnp.arange(NB, dtype=jnp.int32), side='right'),
                        N_EXPERTS - 1).astype(jnp.int32)
    nused = blk_end[-1:].astype(jnp.int32)
    blk = jnp.arange(NB, dtype=jnp.int32)
    nvalid = jnp.clip(counts[blk_e] - (blk - blk_start[blk_e]) * MOE_BLOCK, 0, MOE_BLOCK)
    nvalid = jnp.where(blk < nused[0], nvalid, 0).astype(jnp.int32)
    per = MOE_BLOCK // MOE_SUB
    gblk = jnp.arange(NB * per, dtype=jnp.int32)
    gcnt = jnp.clip(jnp.repeat(nvalid, per) - (gblk % per) * MOE_SUB, 0, MOE_SUB).astype(jnp.int32)
    order = jnp.argsort(gcnt == 0, stable=True).astype(jnp.int32)
    nnz = jnp.sum(gcnt > 0)
    gvisit = jnp.where(gblk < nnz, order, order[jnp.maximum(nnz - 1, 0)])
    gcnt = jnp.where(gblk < nnz, gcnt[order], 0)
    return dest.reshape(T, TOP_K), idx, blk_e, nused, nvalid, gcnt, gvisit


def _block_diag_in(bb):
    gpc = S5_GROUPS // S5_CHUNKS
    x = bb.reshape(S5_CHUNKS, gpc, S5_GROUP, S5_STATE)
    eye = jnp.eye(gpc, dtype=bb.dtype)
    return jnp.einsum('agcn,gh->agchn', x, eye).reshape(S5_CHUNKS, S5_CHUNK_IN, S5_CHUNK_STATE)


def _block_diag_out(cc):
    gpc = S5_GROUPS // S5_CHUNKS
    x = cc.reshape(S5_CHUNKS, gpc, S5_GROUP, S5_STATE)
    eye = jnp.eye(gpc, dtype=cc.dtype)
    return jnp.einsum('agcn,gh->ahngc', x, eye).reshape(S5_CHUNKS, S5_CHUNK_STATE, S5_CHUNK_IN)


def kernel(x_prompt, x_sample, c_prompt, c_sample, cache_k, cache_v, cache_logf, state_s5_re, state_s5_im, page_table, w_ada, b_ada, w_in, fox_b_f, s5_a_re, s5_a_im, s5_log_dt, s5_b_re, s5_b_im, s5_c_re, s5_c_im, s5_d, w_glu, b_glu, p_a, p_b, w_o, ln1_g, ln1_b, w_router, b_router, w_gu, b_gu, w_dn, b_dn, ln2_g, ln2_b):
    B, L, D = x_prompt.shape
    Bd, Ld, _ = x_sample.shape
    Tp, Ts = B * L, Bd * Ld
    W = HEADS * HEAD_DIM
    n_phys = cache_k.shape[1]

    n_c = B + Bd
    c_all = jnp.concatenate([c_prompt, c_sample, jnp.zeros((-n_c % 8, D), F32)], axis=0)
    mod = _adaln(c_all, w_ada[0], b_ada)
    mp = [mod[:B, i * D:(i + 1) * D].reshape(B, 1, D) for i in range(6)]
    ms = [jnp.repeat(mod[B:n_c, i * D:(i + 1) * D], Ld, axis=0) for i in range(6)]

    wi = w_in[0]
    f_lo = 3 * W
    wm = jnp.concatenate([wi[:, :f_lo], wi[:, f_lo + HEADS:]], axis=1).astype(BF16)
    wf = jnp.pad(wi[:, f_lo:f_lo + HEADS], ((0, 0), (0, LANES - HEADS))).astype(BF16)
    bf = jnp.pad(fox_b_f, ((0, 0), (0, LANES - HEADS)))
    hp, logf_p, cT_p = _in_proj(x_prompt.reshape(Tp, D), mp[0], mp[1], wm, wf, wf.T, bf, bf.T,
                                seq_len=L, seg=0)
    hs, logf_s, c_s = _in_proj(x_sample.reshape(Ts, D), ms[0], ms[1], wm, wf, wf.T, bf, bf.T,
                               seq_len=0, seg=Ld)

    lam_r, lam_i, bb_r, bb_i = _s5_params(s5_a_re[0], s5_a_im[0], s5_log_dt[0][:, None],
                                          s5_b_re[0].transpose(0, 2, 1), s5_b_im[0].transpose(0, 2, 1))
    lam_r = lam_r.reshape(S5_CHUNKS, S5_CHUNK_STATE)
    lam_i = lam_i.reshape(S5_CHUNKS, S5_CHUNK_STATE)
    wb = jnp.concatenate([_block_diag_in(bb_r), _block_diag_in(bb_i)], axis=2).astype(BF16)
    wcr = _block_diag_out(s5_c_re[0]).astype(BF16)
    wci = _block_diag_out(s5_c_im[0]).astype(BF16)
    wgl = w_glu[0].astype(BF16)
    s5_args = (lam_r, lam_i, wb, wcr, wci, s5_d, wgl, b_glu)
    zeros_state = jnp.zeros((B, S5_CHUNKS, S5_CHUNK_STATE), F32)
    s5_t = 256
    ssm_p, sr_p, si_p = _s5(hp, 3, zeros_state, zeros_state, *s5_args, nseq=1, slen=s5_t, chunks=L // s5_t)
    ssm_s, sr_s, si_s = _s5(hs, 3, state_s5_re[0].reshape(Bd, S5_CHUNKS, S5_CHUNK_STATE),
                            state_s5_im[0].reshape(Bd, S5_CHUNKS, S5_CHUNK_STATE), *s5_args,
                            nseq=32, slen=Ld, chunks=1)

    att_p = _fox_prompt(hp, cT_p, B, L)
    scale = HEAD_DIM ** -0.5
    q_s = hs[:, :W].reshape(Bd, Ld, HEADS, HEAD_DIM) * scale
    k_s = hs[:, W:2 * W].reshape(Bd, Ld, HEADS, HEAD_DIM)
    v_s = hs[:, 2 * W:3 * W].reshape(Bd, Ld, HEADS, HEAD_DIM)
    qe = q_s.reshape(Bd, Ld * HEADS, HEAD_DIM)
    qbd = jnp.einsum('bqgd,gh->bqghd', q_s, jnp.eye(HEADS, dtype=F32)).reshape(Bd, Ld * HEADS, W).astype(BF16)
    kn = jnp.broadcast_to(k_s[:, :, None], (Bd, Ld, Ld, HEADS, HEAD_DIM)).reshape(Bd, Ld, Ld * HEADS, HEAD_DIM)
    vn = jnp.broadcast_to(v_s[:, :, None], (Bd, Ld, Ld, HEADS, HEAD_DIM)).reshape(Bd, Ld, Ld * HEADS, HEAD_DIM)
    cn = c_s[:, :HEADS].reshape(Bd, Ld, HEADS)
    bn = jnp.broadcast_to(cn.transpose(0, 2, 1)[:, None], (Bd, Ld, HEADS, Ld)).reshape(Bd, Ld * HEADS, Ld)
    bn = jnp.pad(bn, ((0, 0), (0, 0), (0, LANES - Ld)))
    ckt = jnp.transpose(cache_k[0], (0, 2, 3, 1)).reshape(n_phys, W, PAGE)
    cvt = jnp.transpose(cache_v[0], (0, 2, 3, 1)).reshape(n_phys, W, PAGE)
    clft = jnp.transpose(cache_logf[0], (0, 2, 1))
    att_s = _fox_sample(page_table, ckt, cvt, clft, qbd, qe, kn, vn, bn)
    att_s = att_s.reshape(Ts, W).astype(BF16)

    pab, pbb, wob = p_a[0].astype(BF16), p_b[0].astype(BF16), w_o[0].astype(BF16)
    wr = jnp.pad(w_router[0], ((0, 0), (0, LANES - N_EXPERTS)))
    br = jnp.pad(b_router, ((0, 0), (0, LANES - N_EXPERTS)), constant_values=NEG)
    m_p = _mix_gate(att_p, ssm_p, hp, pab, pbb)
    m_s = _mix_gate(att_s, ssm_s, hs, pab, pbb)
    x1_p, u2_p, ri_p, rw_p = _mix_out(m_p, x_prompt.reshape(Tp, D), mp[2], mp[3], mp[4], wob, ln1_g, ln1_b,
                                      wr, br, seq_len=L)
    x1_s, u2_s, ri_s, rw_s = _mix_out(m_s, x_sample.reshape(Ts, D), ms[2], ms[3], ms[4], wob, ln1_g, ln1_b,
                                      wr, br, seq_len=0)

    u2 = jnp.concatenate([u2_p, u2_s], axis=0)
    experts = jnp.concatenate([ri_p[:, :TOP_K], ri_s[:, :TOP_K]], axis=0)
    dest, idx, blk_e, nused, nvalid, gcnt, gvisit = _route(experts)
    xs = _moe_gather(u2, idx, gcnt, gvisit)
    y = _moe_experts(xs, blk_e, nused, nvalid, w_gu, b_gu[0][:, None, :], w_dn, b_dn[0][:, None, :])
    x2_p = _moe_combine(y, dest[:Tp], x1_p, mp[5], rw_p, ln2_g, ln2_b, seq_len=L)
    x2_s = _moe_combine(y, dest[Tp:], x1_s, ms[5], rw_s, ln2_g, ln2_b, seq_len=0)

    def heads(t, n, l):
        return t.reshape(1, n, l, HEADS, HEAD_DIM)

    def state(s, n):
        return s.reshape(1, n, S5_GROUPS, S5_STATE)

    return (x2_p.reshape(B, L, D), x2_s.reshape(Bd, Ld, D),
            heads(hp[:, W:2 * W], B, L), heads(hp[:, 2 * W:3 * W], B, L),
            logf_p[:, :HEADS].reshape(1, B, L, HEADS), state(sr_p, B), state(si_p, B),
            heads(hs[:, W:2 * W], Bd, Ld), heads(hs[:, 2 * W:3 * W], Bd, Ld),
            logf_s[:, :HEADS].reshape(1, Bd, Ld, HEADS), state(sr_s, Bd), state(si_s, Bd))
```

```python
import functools
import math

import jax
import jax.numpy as jnp
from jax import lax
from jax.experimental import pallas as pl
from jax.experimental.pallas import tpu as pltpu

F32 = jnp.float32
BF16 = jnp.bfloat16

HEADS = 16
HEAD_DIM = 64
PAGE = 128
S5_GROUPS = 64
S5_GROUP = 16
S5_STATE = 64
N_EXPERTS = 32
TOP_K = 4
SWIGLU_LIMIT = 7.0
SWIGLU_ALPHA = 1.702
DN_ALPHA = 2.0 ** 0.25
LN_EPS = 1e-5
NEG = -1e30
LOG2E = math.log2(math.e)

LANES = 128
S5_CHUNKS = 8
S5_CHUNK_STATE = S5_GROUPS * S5_STATE // S5_CHUNKS
S5_CHUNK_IN = S5_GROUPS * S5_GROUP // S5_CHUNKS
S5_LANE_BLOCKS = S5_CHUNK_STATE // LANES
FOX_TQ = 1024
FOX_TK = 1024
MOE_BLOCK = 1536
MOE_SUB = 384
MOE_FT = 512
VMEM_LIMIT = 56 * 1024 * 1024
MOE_VMEM_LIMIT = 60 * 1024 * 1024


def _cp(sem, vmem=VMEM_LIMIT):
    return pltpu.CompilerParams(dimension_semantics=sem, vmem_limit_bytes=vmem)


def _dot(a, b):
    return jnp.dot(a, b, preferred_element_type=F32)


def _dot_nt(a, b):
    return lax.dot_general(a, b, (((1,), (1,)), ((), ())), preferred_element_type=F32)


def _log_sigmoid(x):
    return jnp.minimum(x, 0.0) - jnp.log1p(jnp.exp(-jnp.abs(x)))


def _split3(v):
    hi = v.astype(BF16)
    r = v - hi.astype(F32)
    mid = r.astype(BF16)
    lo = (r - mid.astype(F32)).astype(BF16)
    return hi, mid, lo


def _layernorm(v, g, b):
    mu = jnp.mean(v, axis=-1, keepdims=True)
    d = v - mu
    var = jnp.mean(d * d, axis=-1, keepdims=True)
    return d * lax.rsqrt(var + LN_EPS) * g + b


def _ada_kernel(c_ref, w_ref, b_ref, o_ref):
    c = c_ref[...]
    a = (c * jax.nn.sigmoid(c)).astype(BF16)
    o_ref[...] = _dot(a, w_ref[...].astype(BF16)) + b_ref[...]


def _adaln(c_all, w_ada, b_ada):
    R, D = c_all.shape
    N = w_ada.shape[1]
    tn = 1024
    return pl.pallas_call(
        _ada_kernel,
        grid=(N // tn,),
        in_specs=[pl.BlockSpec((R, D), lambda j: (0, 0)),
                  pl.BlockSpec((D, tn), lambda j: (0, j)),
                  pl.BlockSpec((1, tn), lambda j: (0, j))],
        out_specs=pl.BlockSpec((R, tn), lambda j: (0, j)),
        out_shape=jax.ShapeDtypeStruct((R, N), F32),
        compiler_params=_cp(("arbitrary",)),
        name="adaln",
    )(c_all, w_ada, b_ada)


def _in_kernel(x_ref, sh_ref, sc_ref, wm_ref, wf_ref, wft_ref, bf_ref, bft_ref,
               h_ref, logf_ref, cum_ref, u_scr, carry_scr, *, seq_tiles, seg):
    i = pl.program_id(0)
    j = pl.program_id(1)
    tm = x_ref.shape[0]

    @pl.when(j == 0)
    def _():
        u = x_ref[...] * (1.0 + sc_ref[...]) + sh_ref[...]
        ub = u.astype(BF16)
        u_scr[...] = ub
        lf = _log_sigmoid(_dot(ub, wf_ref[...]) + bf_ref[...])
        logf_ref[...] = lf
        r = lax.broadcasted_iota(jnp.int32, (tm, tm), 0)
        c = lax.broadcasted_iota(jnp.int32, (tm, tm), 1)
        if seq_tiles > 0:
            lft = _log_sigmoid(_dot_nt(wft_ref[...], ub) + bft_ref[...])
            tri = (r <= c).astype(BF16)
            hi, mid, lo = _split3(lft)
            cs = _dot(hi, tri) + _dot(mid, tri) + _dot(lo, tri)

            @pl.when(i % seq_tiles == 0)
            def _():
                carry_scr[...] = jnp.zeros_like(carry_scr)

            cs = cs + carry_scr[...]
            cum_ref[...] = cs
            carry_scr[...] = cs[:, tm - 1:tm]
        else:
            tri = ((c <= r) & (c // seg == r // seg)).astype(BF16)
            hi, mid, lo = _split3(lf)
            cum_ref[...] = _dot(tri, hi) + _dot(tri, mid) + _dot(tri, lo)

    h_ref[...] = _dot(u_scr[...], wm_ref[...])


def _in_proj(x, shift, scale, wm, wf, wft, bf, bft, *, seq_len, seg):
    T, D = x.shape
    N = wm.shape[1]
    tm, tn = min(1024, T), 1024
    nt = T // tm
    if seq_len > 0:
        seq_tiles = seq_len // tm
        mod_spec = pl.BlockSpec((None, 1, D), lambda i, j: (i // seq_tiles, 0, 0))
        cum_shape, cum_spec = (LANES, T), pl.BlockSpec((LANES, tm), lambda i, j: (0, i))
    else:
        seq_tiles = 0
        mod_spec = pl.BlockSpec((tm, D), lambda i, j: (i, 0))
        cum_shape, cum_spec = (T, LANES), pl.BlockSpec((tm, LANES), lambda i, j: (i, 0))
    kern = functools.partial(_in_kernel, seq_tiles=seq_tiles, seg=seg)
    return pl.pallas_call(
        kern,
        grid=(nt, N // tn),
        in_specs=[pl.BlockSpec((tm, D), lambda i, j: (i, 0)), mod_spec, mod_spec,
                  pl.BlockSpec((D, tn), lambda i, j: (0, j)),
                  pl.BlockSpec((D, LANES), lambda i, j: (0, 0)),
                  pl.BlockSpec((LANES, D), lambda i, j: (0, 0)),
                  pl.BlockSpec((1, LANES), lambda i, j: (0, 0)),
                  pl.BlockSpec((LANES, 1), lambda i, j: (0, 0))],
        out_specs=[pl.BlockSpec((tm, tn), lambda i, j: (i, j)),
                   pl.BlockSpec((tm, LANES), lambda i, j: (i, 0)),
                   cum_spec],
        out_shape=[jax.ShapeDtypeStruct((T, N), F32),
                   jax.ShapeDtypeStruct((T, LANES), F32),
                   jax.ShapeDtypeStruct(cum_shape, F32)],
        scratch_shapes=[pltpu.VMEM((tm, D), BF16), pltpu.VMEM((LANES, 1), F32)],
        compiler_params=_cp(("arbitrary", "arbitrary")),
        name="in_proj",
    )(x, shift, scale, wm, wf, wft, bf, bft)


def _fox_prompt_kernel(qt_ref, kt_ref, q_ref, k_ref, v_ref, c_ref, o_ref, qs, m_scr, l_scr, acc, *, ratio):
    hp = pl.program_id(1)
    t = pl.program_id(2)
    qi = qt_ref[t]
    kv = kt_ref[t]
    tq, tk = q_ref.shape[0], k_ref.shape[0]
    lane = lax.broadcasted_iota(jnp.int32, (tq, LANES), 1)

    @pl.when(kv == 0)
    def _():
        q2 = q_ref[...] * (HEAD_DIM ** -0.5 * LOG2E)
        qs[0] = jnp.where(lane < HEAD_DIM, q2, 0.0).astype(BF16)
        qs[1] = jnp.where(lane >= HEAD_DIM, q2, 0.0).astype(BF16)
        m_scr[...] = jnp.full_like(m_scr, NEG)
        l_scr[...] = jnp.zeros_like(l_scr)
        acc[...] = jnp.zeros_like(acc)

    def block(diagonal):
        kb = k_ref[...].astype(BF16)
        vb = v_ref[...].astype(BF16)
        r0 = (2 * hp) % 8
        for a in range(2):
            crow = c_ref[pl.ds(r0 + a, 1), :] * LOG2E
            s = _dot_nt(qs[a], kb) - crow
            if diagonal:
                row = lax.broadcasted_iota(jnp.int32, (tq, tk), 0)
                col = lax.broadcasted_iota(jnp.int32, (tq, tk), 1) + (kv * tk - qi * tq)
                s = jnp.where(col <= row, s, NEG)
            cols = [s[:, c * LANES:(c + 1) * LANES] for c in range(tk // LANES)]
            mx = cols[0]
            for sc in cols[1:]:
                mx = jnp.maximum(mx, sc)
            m_prev = m_scr[a]
            m_new = jnp.maximum(m_prev, jnp.max(mx, axis=1, keepdims=True))
            alpha = jnp.exp2(m_prev - m_new)
            ps = [jnp.exp2(sc - m_new) for sc in cols]
            lsum = ps[0]
            for pc in ps[1:]:
                lsum = lsum + pc
            l_scr[a] = alpha * l_scr[a] + lsum
            p = jnp.concatenate(ps, axis=1).astype(BF16)
            acc[a] = alpha * acc[a] + _dot(p, vb)
            m_scr[a] = m_new

    @pl.when(kv < qi * ratio)
    def _():
        block(False)

    @pl.when(kv >= qi * ratio)
    def _():
        block(True)

    @pl.when(kv == (qi + 1) * ratio - 1)
    def _():
        l0 = jnp.sum(l_scr[0], axis=1, keepdims=True)
        l1 = jnp.sum(l_scr[1], axis=1, keepdims=True)
        o = jnp.where(lane < HEAD_DIM, acc[0] / l0, acc[1] / l1)
        o_ref[...] = o.astype(o_ref.dtype)


def _fox_prompt(h, cT, B, L):
    tq, tk = FOX_TQ, FOX_TK
    ratio = tq // tk
    nq, nk = L // tq, L // tk
    npairs = HEADS // 2
    steps = [(qi, kv) for qi in range(nq) for kv in range((qi + 1) * ratio)]
    qt = jnp.asarray([p[0] for p in steps], jnp.int32)
    kt = jnp.asarray([p[1] for p in steps], jnp.int32)
    grid_spec = pltpu.PrefetchScalarGridSpec(
        num_scalar_prefetch=2,
        grid=(B, npairs, len(steps)),
        in_specs=[
            pl.BlockSpec((tq, LANES), lambda b, hp, t, qt, kt: (b * nq + qt[t], hp)),
            pl.BlockSpec((tk, LANES), lambda b, hp, t, qt, kt: (b * nk + kt[t], npairs + hp)),
            pl.BlockSpec((tk, LANES), lambda b, hp, t, qt, kt: (b * nk + kt[t], 2 * npairs + hp)),
            pl.BlockSpec((8, tk), lambda b, hp, t, qt, kt: (hp // 4, b * nk + kt[t])),
        ],
        out_specs=pl.BlockSpec((tq, LANES), lambda b, hp, t, qt, kt: (b * nq + qt[t], hp)),
        scratch_shapes=[pltpu.VMEM((2, tq, LANES), BF16), pltpu.VMEM((2, tq, LANES), F32),
                        pltpu.VMEM((2, tq, LANES), F32), pltpu.VMEM((2, tq, LANES), F32)],
    )
    return pl.pallas_call(
        functools.partial(_fox_prompt_kernel, ratio=ratio),
        grid_spec=grid_spec,
        out_shape=jax.ShapeDtypeStruct((B * L, HEADS * HEAD_DIM), BF16),
        compiler_params=_cp(("parallel", "parallel", "arbitrary")),
        name="fox_prompt",
    )(qt, kt, h, h, h, cT)


def _fox_sample_kernel(pt_ref, *refs, n_pages):
    k_refs = refs[:n_pages]
    v_refs = refs[n_pages:2 * n_pages]
    lf_refs = refs[2 * n_pages:3 * n_pages]
    qbd_ref, qe_ref, kn_ref, vn_ref, bn_ref, o_ref = refs[3 * n_pages:]
    nq = qe_ref.shape[0] // HEADS
    nrow = nq * HEADS
    width = HEADS * HEAD_DIM

    xs = jnp.concatenate([r[...] for r in lf_refs], axis=0)
    hi, mid, lo = _split3(xs)
    after = (lax.broadcasted_iota(jnp.int32, (PAGE, PAGE), 0)
             > lax.broadcasted_iota(jnp.int32, (PAGE, PAGE), 1)).astype(BF16)
    within = _dot(hi, after) + _dot(mid, after) + _dot(lo, after)
    n = n_pages * HEADS
    pr = lax.broadcasted_iota(jnp.int32, (n, n), 0)
    pc = lax.broadcasted_iota(jnp.int32, (n, n), 1)
    later = ((pc // HEADS > pr // HEADS) & (pc % HEADS == pr % HEADS)).astype(BF16)
    beyond = jnp.sum(_dot(later, hi) + _dot(later, mid) + _dot(later, lo), axis=1, keepdims=True)
    bias_rows = within + beyond
    bias = jnp.concatenate([bias_rows[j * HEADS:(j + 1) * HEADS] for j in range(n_pages)], axis=1)
    bias = jnp.concatenate([bias] * nq, axis=0)

    kt = jnp.concatenate([r[...].astype(BF16) for r in k_refs], axis=1)
    s = _dot(qbd_ref[...], kt) + bias

    q_of_row = lax.broadcasted_iota(jnp.int32, (nrow, 1), 0) // HEADS
    qe = qe_ref[...]
    bn = bn_ref[...]
    sn = []
    for j in range(nq):
        sj = jnp.sum(qe * kn_ref[j], axis=1, keepdims=True) - bn[:, j:j + 1]
        sn.append(jnp.where(q_of_row >= j, sj, NEG))
    m = jnp.max(s, axis=1, keepdims=True)
    for sj in sn:
        m = jnp.maximum(m, sj)
    p = jnp.exp(s - m)
    l = jnp.sum(p, axis=1, keepdims=True)

    vt = jnp.concatenate([r[...].astype(BF16) for r in v_refs], axis=1)
    o_all = _dot_nt(p.astype(BF16), vt)
    rh = lax.broadcasted_iota(jnp.int32, (nrow, width), 0) % HEADS
    ch = lax.broadcasted_iota(jnp.int32, (nrow, width), 1) // HEAD_DIM
    o_hi, o_mid, o_lo = _split3(jnp.where(rh == ch, o_all, 0.0))
    fold = (lax.broadcasted_iota(jnp.int32, (width, HEAD_DIM), 0) % HEAD_DIM
            == lax.broadcasted_iota(jnp.int32, (width, HEAD_DIM), 1)).astype(BF16)
    o = _dot(o_hi, fold) + _dot(o_mid, fold) + _dot(o_lo, fold)
    for j in range(nq):
        pj = jnp.exp(sn[j] - m)
        l = l + pj
        o = o + pj * vn_ref[j]
    o_ref[...] = o / l


def _fox_sample(page_table, ckt, cvt, clft, qbd, qe, kn, vn, bn):
    Bd, n_pages = page_table.shape
    nrow = qe.shape[1]
    nq = nrow // HEADS
    width = HEADS * HEAD_DIM

    def page_map(j):
        return lambda b, pt: (pt[b * n_pages + j], 0, 0)

    per_row3 = lambda b, pt: (b, 0, 0)
    per_row4 = lambda b, pt: (b, 0, 0, 0)
    grid_spec = pltpu.PrefetchScalarGridSpec(
        num_scalar_prefetch=1,
        grid=(Bd,),
        in_specs=([pl.BlockSpec((None, width, PAGE), page_map(j)) for j in range(n_pages)]
                  + [pl.BlockSpec((None, width, PAGE), page_map(j)) for j in range(n_pages)]
                  + [pl.BlockSpec((None, HEADS, PAGE), page_map(j)) for j in range(n_pages)]
                  + [pl.BlockSpec((None, nrow, width), per_row3),
                     pl.BlockSpec((None, nrow, HEAD_DIM), per_row3),
                     pl.BlockSpec((None, nq, nrow, HEAD_DIM), per_row4),
                     pl.BlockSpec((None, nq, nrow, HEAD_DIM), per_row4),
                     pl.BlockSpec((None, nrow, LANES), per_row3)]),
        out_specs=pl.BlockSpec((None, nrow, HEAD_DIM), per_row3),
    )
    return pl.pallas_call(
        functools.partial(_fox_sample_kernel, n_pages=n_pages),
        grid_spec=grid_spec,
        out_shape=jax.ShapeDtypeStruct((Bd, nrow, HEAD_DIM), F32),
        compiler_params=_cp(("parallel",)),
        name="fox_sample",
    )(page_table.reshape(-1), *([ckt] * n_pages), *([cvt] * n_pages), *([clft] * n_pages),
      qbd, qe, kn, vn, bn)


def _s5_param_kernel(ar_ref, ai_ref, ldt_ref, br_ref, bi_ref, lr_ref, li_ref, bbr_ref, bbi_ref):
    ar, ai = ar_ref[...], ai_ref[...]
    dt = jnp.exp(ldt_ref[...])
    mag = jnp.exp(ar * dt)
    lr = mag * jnp.cos(ai * dt)
    li = mag * jnp.sin(ai * dt)
    den = ar * ar + ai * ai
    zr = ((lr - 1.0) * ar + li * ai) / den
    zi = (li * ar - (lr - 1.0) * ai) / den
    lr_ref[...] = lr
    li_ref[...] = li
    br, bi = br_ref[...], bi_ref[...]
    zr3, zi3 = zr[:, None, :], zi[:, None, :]
    bbr_ref[...] = zr3 * br - zi3 * bi
    bbi_ref[...] = zr3 * bi + zi3 * br


def _s5_params(a_re, a_im, log_dt, bt_re, bt_im):
    G, N = a_re.shape
    C = bt_re.shape[1]
    return pl.pallas_call(
        _s5_param_kernel,
        out_shape=[jax.ShapeDtypeStruct((G, N), F32), jax.ShapeDtypeStruct((G, N), F32),
                   jax.ShapeDtypeStruct((G, C, N), F32), jax.ShapeDtypeStruct((G, C, N), F32)],
        name="s5_params",
    )(a_re, a_im, log_dt, bt_re, bt_im)


def _s5_kernel(u_ref, h0r_ref, h0i_ref, lr_ref, li_ref, wb_ref, wcr_ref, wci_ref, d_ref, wg_ref, bg_ref,
               o_ref, sr_ref, si_ref, bur, bui, y_scr, cr, ci, *, nseq, slen):
    c_id = pl.program_id(1)
    R = u_ref.shape[0]
    u = u_ref[...]
    ub = u.astype(BF16)
    rs = [_dot(ub[:, gc * S5_CHUNK_IN:(gc + 1) * S5_CHUNK_IN], wb_ref[gc]) for gc in range(S5_CHUNKS)]
    for lb in range(S5_LANE_BLOCKS):
        lo = lb * LANES
        xr = jnp.stack([r[:, lo:lo + LANES] for r in rs], axis=0)
        xi = jnp.stack([r[:, S5_CHUNK_STATE + lo:S5_CHUNK_STATE + lo + LANES] for r in rs], axis=0)
        bur[lb] = jnp.swapaxes(xr, 0, 1)
        bui[lb] = jnp.swapaxes(xi, 0, 1)

    @pl.when(c_id == 0)
    def _():
        cr[...] = h0r_ref[...]
        ci[...] = h0i_ref[...]

    lr = lr_ref[...]
    li = li_ref[...]

    def seq_body(q, _):
        base = q * slen

        def step(t, hc):
            hr, hi = hc
            br = jnp.concatenate([bur[lb, base + t] for lb in range(S5_LANE_BLOCKS)], axis=1)
            bi = jnp.concatenate([bui[lb, base + t] for lb in range(S5_LANE_BLOCKS)], axis=1)
            nr = lr * hr - li * hi + br
            ni = lr * hi + li * hr + bi
            for lb in range(S5_LANE_BLOCKS):
                bur[lb, base + t] = nr[:, lb * LANES:(lb + 1) * LANES]
                bui[lb, base + t] = ni[:, lb * LANES:(lb + 1) * LANES]
            return nr, ni

        hr, hi = lax.fori_loop(0, slen, step, (cr[q], ci[q]), unroll=4)
        cr[q] = hr
        ci[q] = hi
        return 0

    lax.fori_loop(0, nseq, seq_body, 0)
    sr_ref[...] = cr[...]
    si_ref[...] = ci[...]

    hr_t = [jnp.swapaxes(bur[lb], 0, 1) for lb in range(S5_LANE_BLOCKS)]
    hi_t = [jnp.swapaxes(bui[lb], 0, 1) for lb in range(S5_LANE_BLOCKS)]
    for gc in range(S5_CHUNKS):
        hrb = jnp.concatenate([x[gc] for x in hr_t], axis=1).astype(BF16)
        hib = jnp.concatenate([x[gc] for x in hi_t], axis=1).astype(BF16)
        y_scr[:, gc * S5_CHUNK_IN:(gc + 1) * S5_CHUNK_IN] = _dot(hrb, wcr_ref[gc]) - _dot(hib, wci_ref[gc])
    y = y_scr[...] + d_ref[...] * u
    z = jax.nn.gelu(y)
    o_ref[...] = (z * jax.nn.sigmoid(_dot(z.astype(BF16), wg_ref[...]) + bg_ref[...])).astype(o_ref.dtype)


def _s5(h, col_block, h0r, h0i, lam_r, lam_i, wb, wcr, wci, d_skip, w_glu, b_glu, *, nseq, slen, chunks):
    T = h.shape[0]
    W = S5_GROUPS * S5_GROUP
    R = nseq * slen
    nsb = T // (R * chunks)
    n_seq_total = h0r.shape[0]
    st_spec = pl.BlockSpec((nseq, S5_CHUNKS, S5_CHUNK_STATE), lambda sb, c: (sb, 0, 0))
    full3 = lambda sb, c: (0, 0, 0)
    full2 = lambda sb, c: (0, 0)
    kern = functools.partial(_s5_kernel, nseq=nseq, slen=slen)
    return pl.pallas_call(
        kern,
        grid=(nsb, chunks),
        in_specs=[pl.BlockSpec((R, W), lambda sb, c: (sb * chunks + c, col_block)),
                  st_spec, st_spec,
                  pl.BlockSpec((S5_CHUNKS, S5_CHUNK_STATE), full2),
                  pl.BlockSpec((S5_CHUNKS, S5_CHUNK_STATE), full2),
                  pl.BlockSpec((S5_CHUNKS, S5_CHUNK_IN, 2 * S5_CHUNK_STATE), full3),
                  pl.BlockSpec((S5_CHUNKS, S5_CHUNK_STATE, S5_CHUNK_IN), full3),
                  pl.BlockSpec((S5_CHUNKS, S5_CHUNK_STATE, S5_CHUNK_IN), full3),
                  pl.BlockSpec((1, W), full2),
                  pl.BlockSpec((W, W), full2),
                  pl.BlockSpec((1, W), full2)],
        out_specs=[pl.BlockSpec((R, W), lambda sb, c: (sb * chunks + c, 0)), st_spec, st_spec],
        out_shape=[jax.ShapeDtypeStruct((T, W), BF16),
                   jax.ShapeDtypeStruct((n_seq_total, S5_CHUNKS, S5_CHUNK_STATE), F32),
                   jax.ShapeDtypeStruct((n_seq_total, S5_CHUNKS, S5_CHUNK_STATE), F32)],
        scratch_shapes=[pltpu.VMEM((S5_LANE_BLOCKS, R, S5_CHUNKS, LANES), F32),
                        pltpu.VMEM((S5_LANE_BLOCKS, R, S5_CHUNKS, LANES), F32),
                        pltpu.VMEM((R, W), F32),
                        pltpu.VMEM((nseq, S5_CHUNKS, S5_CHUNK_STATE), F32),
                        pltpu.VMEM((nseq, S5_CHUNKS, S5_CHUNK_STATE), F32)],
        compiler_params=_cp(("arbitrary", "arbitrary")),
        name="s5",
    )(h, h0r, h0i, lam_r, lam_i, wb, wcr, wci, d_skip, w_glu, b_glu)


def _mix_gate_kernel(att_ref, ssm_ref, ga_ref, gb_ref, pa_ref, pb_ref, o_ref):
    m = (jax.nn.sigmoid(ga_ref[...]) * _dot(att_ref[...], pa_ref[...])
         + jax.nn.sigmoid(gb_ref[...]) * _dot(ssm_ref[...], pb_ref[...]))
    o_ref[...] = m.astype(o_ref.dtype)


def _mix_gate(att, ssm, h, p_a, p_b):
    T, W = att.shape
    D = p_a.shape[1]
    tm = min(256, T)
    return pl.pallas_call(
        _mix_gate_kernel,
        grid=(T // tm,),
        in_specs=[pl.BlockSpec((tm, W), lambda i: (i, 0)),
                  pl.BlockSpec((tm, W), lambda i: (i, 0)),
                  pl.BlockSpec((tm, D), lambda i: (i, 2)),
                  pl.BlockSpec((tm, D), lambda i: (i, 3)),
                  pl.BlockSpec((W, D), lambda i: (0, 0)),
                  pl.BlockSpec((W, D), lambda i: (0, 0))],
        out_specs=pl.BlockSpec((tm, D), lambda i: (i, 0)),
        out_shape=jax.ShapeDtypeStruct((T, D), BF16),
        compiler_params=_cp(("parallel",)),
        name="mix_gate",
    )(att, ssm, h, h, p_a, p_b)


def _mix_out_kernel(m_ref, x_ref, g1_ref, sh2_ref, sc2_ref, wo_ref, lg_ref, lb_ref, wr_ref, br_ref,
                    x1_ref, u2_ref, ri_ref, rw_ref, cnt_ref, cnt_scr):
    out = _dot(m_ref[...], wo_ref[...])
    x1 = _layernorm(DN_ALPHA * x_ref[...] + g1_ref[...] * out, lg_ref[...], lb_ref[...])
    x1_ref[...] = x1
    u2 = x1 * (1.0 + sc2_ref[...]) + sh2_ref[...]
    u2_ref[...] = u2
    u_hi = u2.astype(BF16)
    u_lo = (u2 - u_hi.astype(F32)).astype(BF16)
    wr = wr_ref[...]
    w_hi = wr.astype(BF16)
    w_lo = (wr - w_hi.astype(F32)).astype(BF16)
    logits = _dot(u_hi, w_hi) + _dot(u_hi, w_lo) + _dot(u_lo, w_hi) + br_ref[...]
    lane = lax.broadcasted_iota(jnp.int32, logits.shape, 1)
    cur = logits
    vals, idxs = [], []
    for _ in range(TOP_K):
        mk = jnp.max(cur, axis=1, keepdims=True)
        ik = jnp.min(jnp.where(cur == mk, lane, LANES), axis=1, keepdims=True)
        vals.append(mk)
        idxs.append(ik)
        cur = jnp.where(lane == ik, -jnp.inf, cur)
    es = [jnp.exp(v - vals[0]) for v in vals]
    den = es[0] + es[1] + es[2] + es[3]
    @pl.when(pl.program_id(0) == 0)
    def _():
        cnt_scr[...] = jnp.zeros_like(cnt_scr)

    tm = logits.shape[0]
    picked = jnp.zeros(logits.shape, F32)
    for k in range(TOP_K):
        picked = picked + (lane == idxs[k]).astype(F32)
    earlier = (lax.broadcasted_iota(jnp.int32, (tm, tm), 1)
               < lax.broadcasted_iota(jnp.int32, (tm, tm), 0)).astype(BF16)
    before = _dot(earlier, picked.astype(BF16)) + cnt_scr[...]
    ri = jnp.zeros(logits.shape, jnp.int32)
    rw = jnp.zeros(logits.shape, F32)
    for k in range(TOP_K):
        rank_k = jnp.sum(jnp.where(lane == idxs[k], before, 0.0), axis=1, keepdims=True)
        ri = jnp.where(lane == k, idxs[k], ri)
        ri = jnp.where(lane == TOP_K + k, rank_k.astype(jnp.int32), ri)
        rw = jnp.where(lane == k, es[k] / den, rw)
    ri_ref[...] = ri
    rw_ref[...] = rw
    cnt_scr[...] += jnp.sum(picked, axis=0, keepdims=True)
    cnt_ref[...] = jnp.broadcast_to(cnt_scr[...], cnt_ref.shape)


def _mix_out(m, x, g1, sh2, sc2, w_o, ln_g, ln_b, w_r, b_r, *, seq_len):
    T, D = x.shape
    tm = min(256, T)
    if seq_len > 0:
        seq_tiles = seq_len // tm
        mod_spec = pl.BlockSpec((None, 1, D), lambda i: (i // seq_tiles, 0, 0))
    else:
        mod_spec = pl.BlockSpec((tm, D), lambda i: (i, 0))
    row = pl.BlockSpec((tm, D), lambda i: (i, 0))
    vec = pl.BlockSpec((1, D), lambda i: (0, 0))
    small = pl.BlockSpec((tm, LANES), lambda i: (i, 0))
    return pl.pallas_call(
        _mix_out_kernel,
        grid=(T // tm,),
        in_specs=[row, row, mod_spec, mod_spec, mod_spec,
                  pl.BlockSpec((D, D), lambda i: (0, 0)), vec, vec,
                  pl.BlockSpec((D, LANES), lambda i: (0, 0)),
                  pl.BlockSpec((1, LANES), lambda i: (0, 0))],
        out_specs=[row, row, small, small, pl.BlockSpec((8, LANES), lambda i: (0, 0))],
        out_shape=[jax.ShapeDtypeStruct((T, D), F32), jax.ShapeDtypeStruct((T, D), F32),
                   jax.ShapeDtypeStruct((T, LANES), jnp.int32), jax.ShapeDtypeStruct((T, LANES), F32),
                   jax.ShapeDtypeStruct((8, LANES), F32)],
        scratch_shapes=[pltpu.VMEM((1, LANES), F32)],
        compiler_params=_cp(("arbitrary",)),
        name="mix_out",
    )(m, x, g1, sh2, sc2, w_o, ln_g, ln_b, w_r, b_r)


def _gather_kernel(cnt_ref, blk_ref, idx_ref, idxn_ref, x_hbm, o_ref, buf, sem):
    j = pl.program_id(0)
    nb = pl.num_programs(0)
    slot = j % 2
    jn = jnp.minimum(j + 1, nb - 1)

    def rows8(step):
        return pl.multiple_of((cnt_ref[step] + 7) // 8 * 8, 8)

    def issue_block(idx_r, n8, sl):
        def issue(g, _):
            for u in range(8):
                r = g * 8 + u
                tok = idx_r[0, 0, r]
                pltpu.make_async_copy(x_hbm.at[pl.ds(tok, 1), :], buf.at[sl, pl.ds(r, 1), :], sem.at[sl]).start()
            return 0

        lax.fori_loop(0, n8 // 8, issue, 0)

    @pl.when(j == 0)
    def _():
        buf[...] = jnp.zeros_like(buf)
        issue_block(idx_ref, rows8(0), 0)

    @pl.when((j + 1 < nb) & (cnt_ref[jn] > 0))
    def _():
        issue_block(idxn_ref, rows8(jn), 1 - slot)

    @pl.when(cnt_ref[j] > 0)
    def _():
        n8 = rows8(j)
        pltpu.make_async_copy(x_hbm.at[pl.ds(0, n8), :], buf.at[slot, pl.ds(0, n8), :], sem.at[slot]).wait()
        o_ref[...] = buf[slot].astype(o_ref.dtype)


def _moe_gather(x, idx, cnt, blk):
    T, D = x.shape
    P = idx.shape[0]
    gb = MOE_SUB
    nb = P // gb
    grid_spec = pltpu.PrefetchScalarGridSpec(
        num_scalar_prefetch=2,
        grid=(nb,),
        in_specs=[pl.BlockSpec((1, 1, gb), lambda j, cnt, blk: (blk[j], 0, 0), memory_space=pltpu.SMEM),
                  pl.BlockSpec((1, 1, gb), lambda j, cnt, blk: (blk[jnp.minimum(j + 1, nb - 1)], 0, 0),
                               memory_space=pltpu.SMEM),
                  pl.BlockSpec(memory_space=pl.ANY)],
        out_specs=pl.BlockSpec((gb, D), lambda j, cnt, blk: (blk[j], 0)),
        scratch_shapes=[pltpu.VMEM((2, gb, D), F32), pltpu.SemaphoreType.DMA((2,))],
    )
    idx3 = idx.reshape(nb, 1, gb)
    return pl.pallas_call(
        _gather_kernel,
        grid_spec=grid_spec,
        out_shape=jax.ShapeDtypeStruct((P, D), BF16),
        compiler_params=pltpu.CompilerParams(dimension_semantics=("arbitrary",), vmem_limit_bytes=VMEM_LIMIT,
                                             disable_bounds_checks=True),
        name="moe_gather",
    )(cnt, blk, idx3, idx3, x)


def _moe_kernel(be_ref, nu_ref, nv_ref, x_ref, wg_ref, wu_ref, wd_ref, bg_ref, bu_ref, bd_ref, y_ref,
                a_scr):
    i = pl.program_id(0)
    s = pl.program_id(1)
    nf = pl.num_programs(1) // 2
    valid = nv_ref[i]

    @pl.when((s < nf) & (valid > 0))
    def _():
        for sbk in range(MOE_BLOCK // MOE_SUB):
            @pl.when(valid > sbk * MOE_SUB)
            def _():
                rows = pl.ds(sbk * MOE_SUB, MOE_SUB)
                xs = x_ref[rows, :].astype(F32)
                gate = jnp.minimum(_dot(xs, wg_ref[...]) + bg_ref[...], SWIGLU_LIMIT)
                up = jnp.clip(_dot(xs, wu_ref[...]) + bu_ref[...], -SWIGLU_LIMIT, SWIGLU_LIMIT)
                a_scr[s, rows, :] = ((up + 1.0) * (gate * jax.nn.sigmoid(SWIGLU_ALPHA * gate))).astype(BF16)

    @pl.when((s >= nf) & (valid > 0))
    def _():
        for sbk in range(MOE_BLOCK // MOE_SUB):
            rows = pl.ds(sbk * MOE_SUB, MOE_SUB)

            @pl.when(valid > sbk * MOE_SUB)
            def _():
                a = jnp.concatenate([a_scr[f, rows, :] for f in range(a_scr.shape[0])], axis=1)
                y_ref[rows, :] = _dot(a.astype(F32), wd_ref[...]) + bd_ref[...]

            @pl.when(valid <= sbk * MOE_SUB)
            def _():
                y_ref[rows, :] = jnp.zeros((MOE_SUB, y_ref.shape[1]), F32)


def _moe_experts(xs, blk_e, nused, nvalid, w_gu, b_gu, w_dn, b_dn):
    P, D = xs.shape
    F = w_dn.shape[2]
    NB = P // MOE_BLOCK
    NF = F // MOE_FT
    ND = D // MOE_FT

    def eff(i, nu):
        return jnp.minimum(i, nu[0] - 1)

    def fa(i, s, nu):
        return jnp.where(i < nu[0], jnp.minimum(s, NF - 1), NF - 1)

    def fb(i, s, nu):
        return jnp.where(i < nu[0], jnp.maximum(s - NF, 0), ND - 1)

    grid_spec = pltpu.PrefetchScalarGridSpec(
        num_scalar_prefetch=3,
        grid=(NB, NF + ND),
        in_specs=[
            pl.BlockSpec((MOE_BLOCK, D), lambda i, s, be, nu, nv: (eff(i, nu), 0)),
            pl.BlockSpec((None, None, D, MOE_FT), lambda i, s, be, nu, nv: (0, be[eff(i, nu)], 0, fa(i, s, nu))),
            pl.BlockSpec((None, None, D, MOE_FT),
                         lambda i, s, be, nu, nv: (0, be[eff(i, nu)], 0, NF + fa(i, s, nu))),
            pl.BlockSpec((None, None, F, MOE_FT), lambda i, s, be, nu, nv: (0, be[eff(i, nu)], 0, fb(i, s, nu))),
            pl.BlockSpec((None, 1, MOE_FT), lambda i, s, be, nu, nv: (be[eff(i, nu)], 0, fa(i, s, nu))),
            pl.BlockSpec((None, 1, MOE_FT), lambda i, s, be, nu, nv: (be[eff(i, nu)], 0, NF + fa(i, s, nu))),
            pl.BlockSpec((None, 1, MOE_FT), lambda i, s, be, nu, nv: (be[eff(i, nu)], 0, fb(i, s, nu))),
        ],
        out_specs=pl.BlockSpec((MOE_BLOCK, MOE_FT), lambda i, s, be, nu, nv: (eff(i, nu), fb(i, s, nu))),
        scratch_shapes=[pltpu.VMEM((NF, MOE_BLOCK, MOE_FT), BF16)],
    )
    assert NF == ND
    return pl.pallas_call(
        _moe_kernel,
        grid_spec=grid_spec,
        out_shape=jax.ShapeDtypeStruct((P, D), F32),
        compiler_params=_cp(("arbitrary", "arbitrary"), MOE_VMEM_LIMIT),
        name="moe_experts",
    )(blk_e, nused, nvalid, xs, w_gu, w_gu, w_dn, b_gu, b_gu, b_dn)


def _combine_kernel(idx_ref, idxn_ref, y_hbm, x1_ref, g2_ref, rw_ref, lg_ref, lb_ref, o_ref, buf, sem):
    i = pl.program_id(0)
    nt = pl.num_programs(0)
    slot = i % 2
    tm = x1_ref.shape[0]

    def issue_tile(idx_r, sl):
        def issue(r, _):
            for k in range(TOP_K):
                d = idx_r[0, 0, k * tm + r]
                pltpu.make_async_copy(y_hbm.at[pl.ds(d, 1), :], buf.at[sl, k, pl.ds(r, 1), :], sem.at[sl]).start()
            return 0

        lax.fori_loop(0, tm, issue, 0, unroll=2)

    @pl.when(i == 0)
    def _():
        issue_tile(idx_ref, 0)

    @pl.when(i + 1 < nt)
    def _():
        issue_tile(idxn_ref, 1 - slot)

    for k in range(TOP_K):
        pltpu.make_async_copy(y_hbm.at[pl.ds(0, tm), :], buf.at[slot, k], sem.at[slot]).wait()
    rw = rw_ref[...]
    moe = rw[:, 0:1] * buf[slot, 0]
    for k in range(1, TOP_K):
        moe = moe + rw[:, k:k + 1] * buf[slot, k]
    o_ref[...] = _layernorm(DN_ALPHA * x1_ref[...] + g2_ref[...] * moe, lg_ref[...], lb_ref[...])


def _moe_combine(y, dest, x1, g2, rw, ln_g, ln_b, *, seq_len):
    T, D = x1.shape
    tm = 128
    nt = T // tm
    idx = dest.reshape(nt, tm, TOP_K).transpose(0, 2, 1).reshape(nt, 1, TOP_K * tm)
    if seq_len > 0:
        seq_tiles = seq_len // tm
        mod_spec = pl.BlockSpec((None, 1, D), lambda i: (i // seq_tiles, 0, 0))
    else:
        mod_spec = pl.BlockSpec((tm, D), lambda i: (i, 0))
    row = pl.BlockSpec((tm, D), lambda i: (i, 0))
    vec = pl.BlockSpec((1, D), lambda i: (0, 0))
    return pl.pallas_call(
        _combine_kernel,
        grid=(nt,),
        in_specs=[pl.BlockSpec((1, 1, TOP_K * tm), lambda i: (i, 0, 0), memory_space=pltpu.SMEM),
                  pl.BlockSpec((1, 1, TOP_K * tm), lambda i: (jnp.minimum(i + 1, nt - 1), 0, 0),
                               memory_space=pltpu.SMEM),
                  pl.BlockSpec(memory_space=pl.ANY),
                  row, mod_spec, pl.BlockSpec((tm, LANES), lambda i: (i, 0)), vec, vec],
        out_specs=row,
        out_shape=jax.ShapeDtypeStruct((T, D), F32),
        scratch_shapes=[pltpu.VMEM((2, TOP_K, tm, D), F32), pltpu.SemaphoreType.DMA((2,))],
        compiler_params=pltpu.CompilerParams(dimension_semantics=("arbitrary",), vmem_limit_bytes=VMEM_LIMIT,
                                             disable_bounds_checks=True),
        name="moe_combine",
    )(idx, idx, y, x1, g2, rw, ln_g, ln_b)


def _route(experts, ranks, counts):
    T = experts.shape[0]
    A = T * TOP_K
    flat_e = experts.reshape(A)
    rank = ranks.reshape(A)
    nblk = (counts + MOE_BLOCK - 1) // MOE_BLOCK
    blk_end = jnp.cumsum(nblk)
    blk_start = blk_end - nblk
    dest = blk_start[flat_e] * MOE_BLOCK + rank
    NB = -(-A // MOE_BLOCK) + N_EXPERTS
    idx = jnp.zeros((NB * MOE_BLOCK,), jnp.int32).at[dest].set(jnp.arange(A, dtype=jnp.int32) // TOP_K)
    blk_e = jnp.minimum(jnp.searchsorted(blk_end, jnp.arange(NB, dtype=jnp.int32), side='right'),
                        N_EXPERTS - 1).astype(jnp.int32)
    nused = blk_end[-1:].astype(jnp.int32)
    blk = jnp.arange(NB, dtype=jnp.int32)
    nvalid = jnp.clip(counts[blk_e] - (blk - blk_start[blk_e]) * MOE_BLOCK, 0, MOE_BLOCK)
    nvalid = jnp.where(blk < nused[0], nvalid, 0).astype(jnp.int32)
    per = MOE_BLOCK // MOE_SUB
    gblk = jnp.arange(NB * per, dtype=jnp.int32)
    gcnt = jnp.clip(jnp.repeat(nvalid, per) - (gblk % per) * MOE_SUB, 0, MOE_SUB).astype(jnp.int32)
    order = jnp.argsort(gcnt == 0, stable=True).astype(jnp.int32)
    nnz = jnp.sum(gcnt > 0)
    gvisit = jnp.where(gblk < nnz, order, order[jnp.maximum(nnz - 1, 0)])
    gcnt = jnp.where(gblk < nnz, gcnt[order], 0)
    return dest.reshape(T, TOP_K), idx, blk_e, nused, nvalid, gcnt, gvisit


def _block_diag_in(bb):
    gpc = S5_GROUPS // S5_CHUNKS
    x = bb.reshape(S5_CHUNKS, gpc, S5_GROUP, S5_STATE)
    eye = jnp.eye(gpc, dtype=bb.dtype)
    return jnp.einsum('agcn,gh->agchn', x, eye).reshape(S5_CHUNKS, S5_CHUNK_IN, S5_CHUNK_STATE)


def _block_diag_out(cc):
    gpc = S5_GROUPS // S5_CHUNKS
    x = cc.reshape(S5_CHUNKS, gpc, S5_GROUP, S5_STATE)
    eye = jnp.eye(gpc, dtype=cc.dtype)
    return jnp.einsum('agcn,gh->ahngc', x, eye).reshape(S5_CHUNKS, S5_CHUNK_STATE, S5_CHUNK_IN)


def kernel(x_prompt, x_sample, c_prompt, c_sample, cache_k, cache_v, cache_logf, state_s5_re, state_s5_im, page_table, w_ada, b_ada, w_in, fox_b_f, s5_a_re, s5_a_im, s5_log_dt, s5_b_re, s5_b_im, s5_c_re, s5_c_im, s5_d, w_glu, b_glu, p_a, p_b, w_o, ln1_g, ln1_b, w_router, b_router, w_gu, b_gu, w_dn, b_dn, ln2_g, ln2_b):
    B, L, D = x_prompt.shape
    Bd, Ld, _ = x_sample.shape
    Tp, Ts = B * L, Bd * Ld
    W = HEADS * HEAD_DIM
    n_phys = cache_k.shape[1]

    n_c = B + Bd
    c_all = jnp.concatenate([c_prompt, c_sample, jnp.zeros((-n_c % 8, D), F32)], axis=0)
    mod = _adaln(c_all, w_ada[0], b_ada)
    mp = [mod[:B, i * D:(i + 1) * D].reshape(B, 1, D) for i in range(6)]
    ms = [jnp.repeat(mod[B:n_c, i * D:(i + 1) * D], Ld, axis=0) for i in range(6)]

    wi = w_in[0]
    f_lo = 3 * W
    wm = jnp.concatenate([wi[:, :f_lo], wi[:, f_lo + HEADS:]], axis=1).astype(BF16)
    wf = jnp.pad(wi[:, f_lo:f_lo + HEADS], ((0, 0), (0, LANES - HEADS))).astype(BF16)
    bf = jnp.pad(fox_b_f, ((0, 0), (0, LANES - HEADS)))
    hp, logf_p, cT_p = _in_proj(x_prompt.reshape(Tp, D), mp[0], mp[1], wm, wf, wf.T, bf, bf.T,
                                seq_len=L, seg=0)
    hs, logf_s, c_s = _in_proj(x_sample.reshape(Ts, D), ms[0], ms[1], wm, wf, wf.T, bf, bf.T,
                               seq_len=0, seg=Ld)

    lam_r, lam_i, bb_r, bb_i = _s5_params(s5_a_re[0], s5_a_im[0], s5_log_dt[0][:, None],
                                          s5_b_re[0].transpose(0, 2, 1), s5_b_im[0].transpose(0, 2, 1))
    lam_r = lam_r.reshape(S5_CHUNKS, S5_CHUNK_STATE)
    lam_i = lam_i.reshape(S5_CHUNKS, S5_CHUNK_STATE)
    wb = jnp.concatenate([_block_diag_in(bb_r), _block_diag_in(bb_i)], axis=2).astype(BF16)
    wcr = _block_diag_out(s5_c_re[0]).astype(BF16)
    wci = _block_diag_out(s5_c_im[0]).astype(BF16)
    wgl = w_glu[0].astype(BF16)
    s5_args = (lam_r, lam_i, wb, wcr, wci, s5_d, wgl, b_glu)
    zeros_state = jnp.zeros((B, S5_CHUNKS, S5_CHUNK_STATE), F32)
    s5_t = 256
    ssm_p, sr_p, si_p = _s5(hp, 3, zeros_state, zeros_state, *s5_args, nseq=1, slen=s5_t, chunks=L // s5_t)
    ssm_s, sr_s, si_s = _s5(hs, 3, state_s5_re[0].reshape(Bd, S5_CHUNKS, S5_CHUNK_STATE),
                            state_s5_im[0].reshape(Bd, S5_CHUNKS, S5_CHUNK_STATE), *s5_args,
                            nseq=32, slen=Ld, chunks=1)

    att_p = _fox_prompt(hp, cT_p, B, L)
    scale = HEAD_DIM ** -0.5
    q_s = hs[:, :W].reshape(Bd, Ld, HEADS, HEAD_DIM) * scale
    k_s = hs[:, W:2 * W].reshape(Bd, Ld, HEADS, HEAD_DIM)
    v_s = hs[:, 2 * W:3 * W].reshape(Bd, Ld, HEADS, HEAD_DIM)
    qe = q_s.reshape(Bd, Ld * HEADS, HEAD_DIM)
    qbd = jnp.einsum('bqgd,gh->bqghd', q_s, jnp.eye(HEADS, dtype=F32)).reshape(Bd, Ld * HEADS, W).astype(BF16)
    kn = jnp.broadcast_to(k_s[:, :, None], (Bd, Ld, Ld, HEADS, HEAD_DIM)).reshape(Bd, Ld, Ld * HEADS, HEAD_DIM)
    vn = jnp.broadcast_to(v_s[:, :, None], (Bd, Ld, Ld, HEADS, HEAD_DIM)).reshape(Bd, Ld, Ld * HEADS, HEAD_DIM)
    cn = c_s[:, :HEADS].reshape(Bd, Ld, HEADS)
    bn = jnp.broadcast_to(cn.transpose(0, 2, 1)[:, None], (Bd, Ld, HEADS, Ld)).reshape(Bd, Ld * HEADS, Ld)
    bn = jnp.pad(bn, ((0, 0), (0, 0), (0, LANES - Ld)))
    ckt = jnp.transpose(cache_k[0], (0, 2, 3, 1)).reshape(n_phys, W, PAGE)
    cvt = jnp.transpose(cache_v[0], (0, 2, 3, 1)).reshape(n_phys, W, PAGE)
    clft = jnp.transpose(cache_logf[0], (0, 2, 1))
    att_s = _fox_sample(page_table, ckt, cvt, clft, qbd, qe, kn, vn, bn)
    att_s = att_s.reshape(Ts, W).astype(BF16)

    pab, pbb, wob = p_a[0].astype(BF16), p_b[0].astype(BF16), w_o[0].astype(BF16)
    wr = jnp.pad(w_router[0], ((0, 0), (0, LANES - N_EXPERTS)))
    br = jnp.pad(b_router, ((0, 0), (0, LANES - N_EXPERTS)), constant_values=NEG)
    m_p = _mix_gate(att_p, ssm_p, hp, pab, pbb)
    m_s = _mix_gate(att_s, ssm_s, hs, pab, pbb)
    x1_p, u2_p, ri_p, rw_p, cnt_p = _mix_out(m_p, x_prompt.reshape(Tp, D), mp[2], mp[3], mp[4], wob, ln1_g, ln1_b,
                                             wr, br, seq_len=L)
    x1_s, u2_s, ri_s, rw_s, cnt_s = _mix_out(m_s, x_sample.reshape(Ts, D), ms[2], ms[3], ms[4], wob, ln1_g, ln1_b,
                                             wr, br, seq_len=0)

    u2 = jnp.concatenate([u2_p, u2_s], axis=0)
    cnt_p = cnt_p[0, :N_EXPERTS].astype(jnp.int32)
    cnt_s = cnt_s[0, :N_EXPERTS].astype(jnp.int32)
    ex_p, ex_s = ri_p[:, :TOP_K], ri_s[:, :TOP_K]
    experts = jnp.concatenate([ex_p, ex_s], axis=0)
    ranks = jnp.concatenate([ri_p[:, TOP_K:2 * TOP_K], ri_s[:, TOP_K:2 * TOP_K] + cnt_p[ex_s]], axis=0)
    dest, idx, blk_e, nused, nvalid, gcnt, gvisit = _route(experts, ranks, cnt_p + cnt_s)
    xs = _moe_gather(u2, idx, gcnt, gvisit)
    y = _moe_experts(xs, blk_e, nused, nvalid, w_gu, b_gu[0][:, None, :], w_dn, b_dn[0][:, None, :])
    x2_p = _moe_combine(y, dest[:Tp], x1_p, mp[5], rw_p, ln2_g, ln2_b, seq_len=L)
    x2_s = _moe_combine(y, dest[Tp:], x1_s, ms[5], rw_s, ln2_g, ln2_b, seq_len=0)

    def heads(t, n, l):
        return t.reshape(1, n, l, HEADS, HEAD_DIM)

    def state(s, n):
        return s.reshape(1, n, S5_GROUPS, S5_STATE)

    return (x2_p.reshape(B, L, D), x2_s.reshape(Bd, Ld, D),
            heads(hp[:, W:2 * W], B, L), heads(hp[:, 2 * W:3 * W], B, L),
            logf_p[:, :HEADS].reshape(1, B, L, HEADS), state(sr_p, B), state(si_p, B),
            heads(hs[:, W:2 * W], Bd, Ld), heads(hs[:, 2 * W:3 * W], Bd, Ld),
            logf_s[:, :HEADS].reshape(1, Bd, Ld, HEADS), state(sr_s, Bd), state(si_s, Bd))
```

```python
import functools
import math

import jax
import jax.numpy as jnp
from jax import lax
from jax.experimental import pallas as pl
from jax.experimental.pallas import tpu as pltpu

F32 = jnp.float32
BF16 = jnp.bfloat16

HEADS = 16
HEAD_DIM = 64
PAGE = 128
S5_GROUPS = 64
S5_GROUP = 16
S5_STATE = 64
N_EXPERTS = 32
TOP_K = 4
SWIGLU_LIMIT = 7.0
SWIGLU_ALPHA = 1.702
DN_ALPHA = 2.0 ** 0.25
LN_EPS = 1e-5
NEG = -1e30
LOG2E = math.log2(math.e)

LANES = 128
S5_CHUNKS = 8
S5_CHUNK_STATE = S5_GROUPS * S5_STATE // S5_CHUNKS
S5_CHUNK_IN = S5_GROUPS * S5_GROUP // S5_CHUNKS
S5_LANE_BLOCKS = S5_CHUNK_STATE // LANES
FOX_TQ = 1024
FOX_TK = 1024
MOE_BLOCK = 1536
MOE_SUB = 384
MOE_TAIL = 128
MOE_FT = 512
VMEM_LIMIT = 56 * 1024 * 1024
MOE_VMEM_LIMIT = 60 * 1024 * 1024


def _cp(sem, vmem=VMEM_LIMIT):
    return pltpu.CompilerParams(dimension_semantics=sem, vmem_limit_bytes=vmem)


def _dot(a, b):
    return jnp.dot(a, b, preferred_element_type=F32)


def _dot_nt(a, b):
    return lax.dot_general(a, b, (((1,), (1,)), ((), ())), preferred_element_type=F32)


def _log_sigmoid(x):
    return jnp.minimum(x, 0.0) - jnp.log1p(jnp.exp(-jnp.abs(x)))


def _split3(v):
    hi = v.astype(BF16)
    r = v - hi.astype(F32)
    mid = r.astype(BF16)
    lo = (r - mid.astype(F32)).astype(BF16)
    return hi, mid, lo


def _layernorm(v, g, b):
    mu = jnp.mean(v, axis=-1, keepdims=True)
    d = v - mu
    var = jnp.mean(d * d, axis=-1, keepdims=True)
    return d * lax.rsqrt(var + LN_EPS) * g + b


def _ada_kernel(c_ref, w_ref, b_ref, o_ref):
    c = c_ref[...]
    a = (c * jax.nn.sigmoid(c)).astype(BF16)
    o_ref[...] = _dot(a, w_ref[...].astype(BF16)) + b_ref[...]


def _adaln(c_all, w_ada, b_ada):
    R, D = c_all.shape
    N = w_ada.shape[1]
    tn = 1024
    return pl.pallas_call(
        _ada_kernel,
        grid=(N // tn,),
        in_specs=[pl.BlockSpec((R, D), lambda j: (0, 0)),
                  pl.BlockSpec((D, tn), lambda j: (0, j)),
                  pl.BlockSpec((1, tn), lambda j: (0, j))],
        out_specs=pl.BlockSpec((R, tn), lambda j: (0, j)),
        out_shape=jax.ShapeDtypeStruct((R, N), F32),
        compiler_params=_cp(("arbitrary",)),
        name="adaln",
    )(c_all, w_ada, b_ada)


def _in_kernel(x_ref, sh_ref, sc_ref, wm_ref, wf_ref, wft_ref, bf_ref, bft_ref,
               h_ref, logf_ref, cum_ref, u_scr, carry_scr, *, seq_tiles, seg):
    i = pl.program_id(0)
    j = pl.program_id(1)
    tm = x_ref.shape[0]

    @pl.when(j == 0)
    def _():
        u = x_ref[...] * (1.0 + sc_ref[...]) + sh_ref[...]
        ub = u.astype(BF16)
        u_scr[...] = ub
        lf = _log_sigmoid(_dot(ub, wf_ref[...]) + bf_ref[...])
        logf_ref[...] = lf
        r = lax.broadcasted_iota(jnp.int32, (tm, tm), 0)
        c = lax.broadcasted_iota(jnp.int32, (tm, tm), 1)
        if seq_tiles > 0:
            lft = _log_sigmoid(_dot_nt(wft_ref[...], ub) + bft_ref[...])
            tri = (r <= c).astype(BF16)
            hi, mid, lo = _split3(lft)
            cs = _dot(hi, tri) + _dot(mid, tri) + _dot(lo, tri)

            @pl.when(i % seq_tiles == 0)
            def _():
                carry_scr[...] = jnp.zeros_like(carry_scr)

            cs = cs + carry_scr[...]
            cum_ref[...] = cs
            carry_scr[...] = cs[:, tm - 1:tm]
        else:
            tri = ((c <= r) & (c // seg == r // seg)).astype(BF16)
            hi, mid, lo = _split3(lf)
            cum_ref[...] = _dot(tri, hi) + _dot(tri, mid) + _dot(tri, lo)

    h_ref[...] = _dot_nt(u_scr[...], wm_ref[...])


def _in_proj(x, shift, scale, wmt, wf, wft, bf, bft, *, seq_len, seg):
    T, D = x.shape
    N = wmt.shape[0]
    tm, tn = min(1024, T), 1024
    nt = T // tm
    if seq_len > 0:
        seq_tiles = seq_len // tm
        mod_spec = pl.BlockSpec((None, 1, D), lambda i, j: (i // seq_tiles, 0, 0))
        cum_shape, cum_spec = (LANES, T), pl.BlockSpec((LANES, tm), lambda i, j: (0, i))
    else:
        seq_tiles = 0
        mod_spec = pl.BlockSpec((tm, D), lambda i, j: (i, 0))
        cum_shape, cum_spec = (T, LANES), pl.BlockSpec((tm, LANES), lambda i, j: (i, 0))
    kern = functools.partial(_in_kernel, seq_tiles=seq_tiles, seg=seg)
    return pl.pallas_call(
        kern,
        grid=(nt, N // tn),
        in_specs=[pl.BlockSpec((tm, D), lambda i, j: (i, 0)), mod_spec, mod_spec,
                  pl.BlockSpec((tn, D), lambda i, j: (j, 0)),
                  pl.BlockSpec((D, LANES), lambda i, j: (0, 0)),
                  pl.BlockSpec((LANES, D), lambda i, j: (0, 0)),
                  pl.BlockSpec((1, LANES), lambda i, j: (0, 0)),
                  pl.BlockSpec((LANES, 1), lambda i, j: (0, 0))],
        out_specs=[pl.BlockSpec((tm, tn), lambda i, j: (i, j)),
                   pl.BlockSpec((tm, LANES), lambda i, j: (i, 0)),
                   cum_spec],
        out_shape=[jax.ShapeDtypeStruct((T, N), F32),
                   jax.ShapeDtypeStruct((T, LANES), F32),
                   jax.ShapeDtypeStruct(cum_shape, F32)],
        scratch_shapes=[pltpu.VMEM((tm, D), BF16), pltpu.VMEM((LANES, 1), F32)],
        compiler_params=_cp(("arbitrary", "arbitrary")),
        name="in_proj",
    )(x, shift, scale, wmt, wf, wft, bf, bft)


def _fox_prompt_kernel(qt_ref, kt_ref, q_ref, k_ref, v_ref, c_ref, o_ref, qs, m_scr, l_scr, acc, *, ratio):
    hp = pl.program_id(1)
    t = pl.program_id(2)
    qi = qt_ref[t]
    kv = kt_ref[t]
    tq, tk = q_ref.shape[0], k_ref.shape[0]
    lane = lax.broadcasted_iota(jnp.int32, (tq, LANES), 1)

    @pl.when(kv == 0)
    def _():
        q2 = q_ref[...] * (HEAD_DIM ** -0.5 * LOG2E)
        qs[0] = jnp.where(lane < HEAD_DIM, q2, 0.0).astype(BF16)
        qs[1] = jnp.where(lane >= HEAD_DIM, q2, 0.0).astype(BF16)
        m_scr[...] = jnp.full_like(m_scr, NEG)
        l_scr[...] = jnp.zeros_like(l_scr)
        acc[...] = jnp.zeros_like(acc)

    def block(diagonal):
        kb = k_ref[...].astype(BF16)
        vb = v_ref[...].astype(BF16)
        r0 = (2 * hp) % 8
        for a in range(2):
            crow = c_ref[pl.ds(r0 + a, 1), :] * LOG2E
            s = _dot_nt(qs[a], kb) - crow
            if diagonal:
                row = lax.broadcasted_iota(jnp.int32, (tq, tk), 0)
                col = lax.broadcasted_iota(jnp.int32, (tq, tk), 1) + (kv * tk - qi * tq)
                s = jnp.where(col <= row, s, NEG)
            cols = [s[:, c * LANES:(c + 1) * LANES] for c in range(tk // LANES)]
            mx = cols[0]
            for sc in cols[1:]:
                mx = jnp.maximum(mx, sc)
            m_prev = m_scr[a]
            m_new = jnp.maximum(m_prev, jnp.max(mx, axis=1, keepdims=True))
            alpha = jnp.exp2(m_prev - m_new)
            ps = [jnp.exp2(sc - m_new) for sc in cols]
            lsum = ps[0]
            for pc in ps[1:]:
                lsum = lsum + pc
            l_scr[a] = alpha * l_scr[a] + lsum
            p = jnp.concatenate(ps, axis=1).astype(BF16)
            acc[a] = alpha * acc[a] + _dot(p, vb)
            m_scr[a] = m_new

    @pl.when(kv < qi * ratio)
    def _():
        block(False)

    @pl.when(kv >= qi * ratio)
    def _():
        block(True)

    @pl.when(kv == (qi + 1) * ratio - 1)
    def _():
        l0 = jnp.sum(l_scr[0], axis=1, keepdims=True)
        l1 = jnp.sum(l_scr[1], axis=1, keepdims=True)
        o = jnp.where(lane < HEAD_DIM, acc[0] / l0, acc[1] / l1)
        o_ref[...] = o.astype(o_ref.dtype)


def _fox_prompt(h, cT, B, L):
    tq, tk = FOX_TQ, FOX_TK
    ratio = tq // tk
    nq, nk = L // tq, L // tk
    npairs = HEADS // 2
    steps = [(qi, kv) for qi in range(nq) for kv in range((qi + 1) * ratio)]
    qt = jnp.asarray([p[0] for p in steps], jnp.int32)
    kt = jnp.asarray([p[1] for p in steps], jnp.int32)
    grid_spec = pltpu.PrefetchScalarGridSpec(
        num_scalar_prefetch=2,
        grid=(B, npairs, len(steps)),
        in_specs=[
            pl.BlockSpec((tq, LANES), lambda b, hp, t, qt, kt: (b * nq + qt[t], hp)),
            pl.BlockSpec((tk, LANES), lambda b, hp, t, qt, kt: (b * nk + kt[t], npairs + hp)),
            pl.BlockSpec((tk, LANES), lambda b, hp, t, qt, kt: (b * nk + kt[t], 2 * npairs + hp)),
            pl.BlockSpec((8, tk), lambda b, hp, t, qt, kt: (hp // 4, b * nk + kt[t])),
        ],
        out_specs=pl.BlockSpec((tq, LANES), lambda b, hp, t, qt, kt: (b * nq + qt[t], hp)),
        scratch_shapes=[pltpu.VMEM((2, tq, LANES), BF16), pltpu.VMEM((2, tq, LANES), F32),
                        pltpu.VMEM((2, tq, LANES), F32), pltpu.VMEM((2, tq, LANES), F32)],
    )
    return pl.pallas_call(
        functools.partial(_fox_prompt_kernel, ratio=ratio),
        grid_spec=grid_spec,
        out_shape=jax.ShapeDtypeStruct((B * L, HEADS * HEAD_DIM), BF16),
        compiler_params=_cp(("parallel", "parallel", "arbitrary")),
        name="fox_prompt",
    )(qt, kt, h, h, h, cT)


def _fox_sample_kernel(pt_ref, *refs, n_pages):
    k_refs = refs[:n_pages]
    v_refs = refs[n_pages:2 * n_pages]
    lf_refs = refs[2 * n_pages:3 * n_pages]
    qbd_ref, qe_ref, kn_ref, vn_ref, bn_ref, o_ref = refs[3 * n_pages:]
    nq = qe_ref.shape[0] // HEADS
    nrow = nq * HEADS
    width = HEADS * HEAD_DIM

    xs = jnp.concatenate([r[...] for r in lf_refs], axis=0)
    hi, mid, lo = _split3(xs)
    after = (lax.broadcasted_iota(jnp.int32, (PAGE, PAGE), 0)
             > lax.broadcasted_iota(jnp.int32, (PAGE, PAGE), 1)).astype(BF16)
    within = _dot(hi, after) + _dot(mid, after) + _dot(lo, after)
    n = n_pages * HEADS
    pr = lax.broadcasted_iota(jnp.int32, (n, n), 0)
    pc = lax.broadcasted_iota(jnp.int32, (n, n), 1)
    later = ((pc // HEADS > pr // HEADS) & (pc % HEADS == pr % HEADS)).astype(BF16)
    beyond = jnp.sum(_dot(later, hi) + _dot(later, mid) + _dot(later, lo), axis=1, keepdims=True)
    bias_rows = within + beyond
    bias = jnp.concatenate([bias_rows[j * HEADS:(j + 1) * HEADS] for j in range(n_pages)], axis=1)
    bias = jnp.concatenate([bias] * nq, axis=0)

    kt = jnp.concatenate([r[...].astype(BF16) for r in k_refs], axis=1)
    s = _dot(qbd_ref[...], kt) + bias

    q_of_row = lax.broadcasted_iota(jnp.int32, (nrow, 1), 0) // HEADS
    qe = qe_ref[...]
    bn = bn_ref[...]
    sn = []
    for j in range(nq):
        sj = jnp.sum(qe * kn_ref[j], axis=1, keepdims=True) - bn[:, j:j + 1]
        sn.append(jnp.where(q_of_row >= j, sj, NEG))
    m = jnp.max(s, axis=1, keepdims=True)
    for sj in sn:
        m = jnp.maximum(m, sj)
    p = jnp.exp(s - m)
    l = jnp.sum(p, axis=1, keepdims=True)

    vt = jnp.concatenate([r[...].astype(BF16) for r in v_refs], axis=1)
    o_all = _dot_nt(p.astype(BF16), vt)
    rh = lax.broadcasted_iota(jnp.int32, (nrow, width), 0) % HEADS
    ch = lax.broadcasted_iota(jnp.int32, (nrow, width), 1) // HEAD_DIM
    o_hi, o_mid, o_lo = _split3(jnp.where(rh == ch, o_all, 0.0))
    fold = (lax.broadcasted_iota(jnp.int32, (width, HEAD_DIM), 0) % HEAD_DIM
            == lax.broadcasted_iota(jnp.int32, (width, HEAD_DIM), 1)).astype(BF16)
    o = _dot(o_hi, fold) + _dot(o_mid, fold) + _dot(o_lo, fold)
    for j in range(nq):
        pj = jnp.exp(sn[j] - m)
        l = l + pj
        o = o + pj * vn_ref[j]
    o_ref[...] = o / l


def _fox_sample(page_table, ckt, cvt, clft, qbd, qe, kn, vn, bn):
    Bd, n_pages = page_table.shape
    nrow = qe.shape[1]
    nq = nrow // HEADS
    width = HEADS * HEAD_DIM

    def page_map(j):
        return lambda b, pt: (pt[b * n_pages + j], 0, 0)

    per_row3 = lambda b, pt: (b, 0, 0)
    per_row4 = lambda b, pt: (b, 0, 0, 0)
    grid_spec = pltpu.PrefetchScalarGridSpec(
        num_scalar_prefetch=1,
        grid=(Bd,),
        in_specs=([pl.BlockSpec((None, width, PAGE), page_map(j)) for j in range(n_pages)]
                  + [pl.BlockSpec((None, width, PAGE), page_map(j)) for j in range(n_pages)]
                  + [pl.BlockSpec((None, HEADS, PAGE), page_map(j)) for j in range(n_pages)]
                  + [pl.BlockSpec((None, nrow, width), per_row3),
                     pl.BlockSpec((None, nrow, HEAD_DIM), per_row3),
                     pl.BlockSpec((None, nq, nrow, HEAD_DIM), per_row4),
                     pl.BlockSpec((None, nq, nrow, HEAD_DIM), per_row4),
                     pl.BlockSpec((None, nrow, LANES), per_row3)]),
        out_specs=pl.BlockSpec((None, nrow, HEAD_DIM), per_row3),
    )
    return pl.pallas_call(
        functools.partial(_fox_sample_kernel, n_pages=n_pages),
        grid_spec=grid_spec,
        out_shape=jax.ShapeDtypeStruct((Bd, nrow, HEAD_DIM), F32),
        compiler_params=_cp(("parallel",)),
        name="fox_sample",
    )(page_table.reshape(-1), *([ckt] * n_pages), *([cvt] * n_pages), *([clft] * n_pages),
      qbd, qe, kn, vn, bn)


def _s5_param_kernel(ar_ref, ai_ref, ldt_ref, br_ref, bi_ref, lr_ref, li_ref, bbr_ref, bbi_ref):
    ar, ai = ar_ref[...], ai_ref[...]
    dt = jnp.exp(ldt_ref[...])
    mag = jnp.exp(ar * dt)
    lr = mag * jnp.cos(ai * dt)
    li = mag * jnp.sin(ai * dt)
    den = ar * ar + ai * ai
    zr = ((lr - 1.0) * ar + li * ai) / den
    zi = (li * ar - (lr - 1.0) * ai) / den
    lr_ref[...] = lr
    li_ref[...] = li
    br, bi = br_ref[...], bi_ref[...]
    zr3, zi3 = zr[:, None, :], zi[:, None, :]
    bbr_ref[...] = zr3 * br - zi3 * bi
    bbi_ref[...] = zr3 * bi + zi3 * br


def _s5_params(a_re, a_im, log_dt, bt_re, bt_im):
    G, N = a_re.shape
    C = bt_re.shape[1]
    return pl.pallas_call(
        _s5_param_kernel,
        out_shape=[jax.ShapeDtypeStruct((G, N), F32), jax.ShapeDtypeStruct((G, N), F32),
                   jax.ShapeDtypeStruct((G, C, N), F32), jax.ShapeDtypeStruct((G, C, N), F32)],
        name="s5_params",
    )(a_re, a_im, log_dt, bt_re, bt_im)


def _s5_kernel(u_ref, h0r_ref, h0i_ref, lr_ref, li_ref, wb_ref, wcr_ref, wci_ref, d_ref, wg_ref, bg_ref,
               o_ref, sr_ref, si_ref, bur, bui, y_scr, cr, ci, *, nseq, slen):
    c_id = pl.program_id(1)
    R = u_ref.shape[0]
    u = u_ref[...]
    ub = u.astype(BF16)
    rs = [_dot(ub[:, gc * S5_CHUNK_IN:(gc + 1) * S5_CHUNK_IN], wb_ref[gc]) for gc in range(S5_CHUNKS)]
    for lb in range(S5_LANE_BLOCKS):
        lo = lb * LANES
        xr = jnp.stack([r[:, lo:lo + LANES] for r in rs], axis=0)
        xi = jnp.stack([r[:, S5_CHUNK_STATE + lo:S5_CHUNK_STATE + lo + LANES] for r in rs], axis=0)
        bur[lb] = jnp.swapaxes(xr, 0, 1)
        bui[lb] = jnp.swapaxes(xi, 0, 1)

    @pl.when(c_id == 0)
    def _():
        cr[...] = h0r_ref[...]
        ci[...] = h0i_ref[...]

    lr = lr_ref[...]
    li = li_ref[...]

    def seq_body(q, _):
        base = q * slen

        def step(t, hc):
            hr, hi = hc
            br = jnp.concatenate([bur[lb, base + t] for lb in range(S5_LANE_BLOCKS)], axis=1)
            bi = jnp.concatenate([bui[lb, base + t] for lb in range(S5_LANE_BLOCKS)], axis=1)
            nr = lr * hr - li * hi + br
            ni = lr * hi + li * hr + bi
            for lb in range(S5_LANE_BLOCKS):
                bur[lb, base + t] = nr[:, lb * LANES:(lb + 1) * LANES]
                bui[lb, base + t] = ni[:, lb * LANES:(lb + 1) * LANES]
            return nr, ni

        hr, hi = lax.fori_loop(0, slen, step, (cr[q], ci[q]), unroll=4)
        cr[q] = hr
        ci[q] = hi
        return 0

    lax.fori_loop(0, nseq, seq_body, 0)
    sr_ref[...] = cr[...]
    si_ref[...] = ci[...]

    hr_t = [jnp.swapaxes(bur[lb], 0, 1) for lb in range(S5_LANE_BLOCKS)]
    hi_t = [jnp.swapaxes(bui[lb], 0, 1) for lb in range(S5_LANE_BLOCKS)]
    for gc in range(S5_CHUNKS):
        hrb = jnp.concatenate([x[gc] for x in hr_t], axis=1).astype(BF16)
        hib = jnp.concatenate([x[gc] for x in hi_t], axis=1).astype(BF16)
        y_scr[:, gc * S5_CHUNK_IN:(gc + 1) * S5_CHUNK_IN] = _dot(hrb, wcr_ref[gc]) - _dot(hib, wci_ref[gc])
    y = y_scr[...] + d_ref[...] * u
    z = jax.nn.gelu(y)
    o_ref[...] = (z * jax.nn.sigmoid(_dot(z.astype(BF16), wg_ref[...]) + bg_ref[...])).astype(o_ref.dtype)


def _s5(h, col_block, h0r, h0i, lam_r, lam_i, wb, wcr, wci, d_skip, w_glu, b_glu, *, nseq, slen, chunks):
    T = h.shape[0]
    W = S5_GROUPS * S5_GROUP
    R = nseq * slen
    nsb = T // (R * chunks)
    n_seq_total = h0r.shape[0]
    st_spec = pl.BlockSpec((nseq, S5_CHUNKS, S5_CHUNK_STATE), lambda sb, c: (sb, 0, 0))
    full3 = lambda sb, c: (0, 0, 0)
    full2 = lambda sb, c: (0, 0)
    kern = functools.partial(_s5_kernel, nseq=nseq, slen=slen)
    return pl.pallas_call(
        kern,
        grid=(nsb, chunks),
        in_specs=[pl.BlockSpec((R, W), lambda sb, c: (sb * chunks + c, col_block)),
                  st_spec, st_spec,
                  pl.BlockSpec((S5_CHUNKS, S5_CHUNK_STATE), full2),
                  pl.BlockSpec((S5_CHUNKS, S5_CHUNK_STATE), full2),
                  pl.BlockSpec((S5_CHUNKS, S5_CHUNK_IN, 2 * S5_CHUNK_STATE), full3),
                  pl.BlockSpec((S5_CHUNKS, S5_CHUNK_STATE, S5_CHUNK_IN), full3),
                  pl.BlockSpec((S5_CHUNKS, S5_CHUNK_STATE, S5_CHUNK_IN), full3),
                  pl.BlockSpec((1, W), full2),
                  pl.BlockSpec((W, W), full2),
                  pl.BlockSpec((1, W), full2)],
        out_specs=[pl.BlockSpec((R, W), lambda sb, c: (sb * chunks + c, 0)), st_spec, st_spec],
        out_shape=[jax.ShapeDtypeStruct((T, W), BF16),
                   jax.ShapeDtypeStruct((n_seq_total, S5_CHUNKS, S5_CHUNK_STATE), F32),
                   jax.ShapeDtypeStruct((n_seq_total, S5_CHUNKS, S5_CHUNK_STATE), F32)],
        scratch_shapes=[pltpu.VMEM((S5_LANE_BLOCKS, R, S5_CHUNKS, LANES), F32),
                        pltpu.VMEM((S5_LANE_BLOCKS, R, S5_CHUNKS, LANES), F32),
                        pltpu.VMEM((R, W), F32),
                        pltpu.VMEM((nseq, S5_CHUNKS, S5_CHUNK_STATE), F32),
                        pltpu.VMEM((nseq, S5_CHUNKS, S5_CHUNK_STATE), F32)],
        compiler_params=_cp(("arbitrary", "arbitrary")),
        name="s5",
    )(h, h0r, h0i, lam_r, lam_i, wb, wcr, wci, d_skip, w_glu, b_glu)


def _mix_gate_kernel(att_ref, ssm_ref, ga_ref, gb_ref, pa_ref, pb_ref, o_ref):
    m = (jax.nn.sigmoid(ga_ref[...]) * _dot(att_ref[...], pa_ref[...])
         + jax.nn.sigmoid(gb_ref[...]) * _dot(ssm_ref[...], pb_ref[...]))
    o_ref[...] = m.astype(o_ref.dtype)


def _mix_gate(att, ssm, h, p_a, p_b):
    T, W = att.shape
    D = p_a.shape[1]
    tm = min(256, T)
    return pl.pallas_call(
        _mix_gate_kernel,
        grid=(T // tm,),
        in_specs=[pl.BlockSpec((tm, W), lambda i: (i, 0)),
                  pl.BlockSpec((tm, W), lambda i: (i, 0)),
                  pl.BlockSpec((tm, D), lambda i: (i, 2)),
                  pl.BlockSpec((tm, D), lambda i: (i, 3)),
                  pl.BlockSpec((W, D), lambda i: (0, 0)),
                  pl.BlockSpec((W, D), lambda i: (0, 0))],
        out_specs=pl.BlockSpec((tm, D), lambda i: (i, 0)),
        out_shape=jax.ShapeDtypeStruct((T, D), BF16),
        compiler_params=_cp(("parallel",)),
        name="mix_gate",
    )(att, ssm, h, h, p_a, p_b)


def _mix_out_kernel(m_ref, x_ref, g1_ref, sh2_ref, sc2_ref, wo_ref, lg_ref, lb_ref, wr_ref, br_ref,
                    x1_ref, u2_ref, ri_ref, rw_ref, cnt_ref, cnt_scr):
    out = _dot(m_ref[...], wo_ref[...])
    x1 = _layernorm(DN_ALPHA * x_ref[...] + g1_ref[...] * out, lg_ref[...], lb_ref[...])
    x1_ref[...] = x1
    u2 = x1 * (1.0 + sc2_ref[...]) + sh2_ref[...]
    u2_ref[...] = u2
    u_hi = u2.astype(BF16)
    u_lo = (u2 - u_hi.astype(F32)).astype(BF16)
    wr = wr_ref[...]
    w_hi = wr.astype(BF16)
    w_lo = (wr - w_hi.astype(F32)).astype(BF16)
    logits = _dot(u_hi, w_hi) + _dot(u_hi, w_lo) + _dot(u_lo, w_hi) + br_ref[...]
    lane = lax.broadcasted_iota(jnp.int32, logits.shape, 1)
    cur = logits
    vals, idxs = [], []
    for _ in range(TOP_K):
        mk = jnp.max(cur, axis=1, keepdims=True)
        ik = jnp.min(jnp.where(cur == mk, lane, LANES), axis=1, keepdims=True)
        vals.append(mk)
        idxs.append(ik)
        cur = jnp.where(lane == ik, -jnp.inf, cur)
    es = [jnp.exp(v - vals[0]) for v in vals]
    den = es[0] + es[1] + es[2] + es[3]
    @pl.when(pl.program_id(0) == 0)
    def _():
        cnt_scr[...] = jnp.zeros_like(cnt_scr)

    tm = logits.shape[0]
    picked = jnp.zeros(logits.shape, F32)
    for k in range(TOP_K):
        picked = picked + (lane == idxs[k]).astype(F32)
    earlier = (lax.broadcasted_iota(jnp.int32, (tm, tm), 1)
               < lax.broadcasted_iota(jnp.int32, (tm, tm), 0)).astype(BF16)
    before = _dot(earlier, picked.astype(BF16)) + cnt_scr[...]
    ri = jnp.zeros(logits.shape, jnp.int32)
    rw = jnp.zeros(logits.shape, F32)
    for k in range(TOP_K):
        rank_k = jnp.sum(jnp.where(lane == idxs[k], before, 0.0), axis=1, keepdims=True)
        ri = jnp.where(lane == k, idxs[k], ri)
        ri = jnp.where(lane == TOP_K + k, rank_k.astype(jnp.int32), ri)
        rw = jnp.where(lane == k, es[k] / den, rw)
    ri_ref[...] = ri
    rw_ref[...] = rw
    cnt_scr[...] += jnp.sum(picked, axis=0, keepdims=True)
    cnt_ref[...] = jnp.broadcast_to(cnt_scr[...], cnt_ref.shape)


def _mix_out(m, x, g1, sh2, sc2, w_o, ln_g, ln_b, w_r, b_r, *, seq_len):
    T, D = x.shape
    tm = min(256, T)
    if seq_len > 0:
        seq_tiles = seq_len // tm
        mod_spec = pl.BlockSpec((None, 1, D), lambda i: (i // seq_tiles, 0, 0))
    else:
        mod_spec = pl.BlockSpec((tm, D), lambda i: (i, 0))
    row = pl.BlockSpec((tm, D), lambda i: (i, 0))
    vec = pl.BlockSpec((1, D), lambda i: (0, 0))
    small = pl.BlockSpec((tm, LANES), lambda i: (i, 0))
    return pl.pallas_call(
        _mix_out_kernel,
        grid=(T // tm,),
        in_specs=[row, row, mod_spec, mod_spec, mod_spec,
                  pl.BlockSpec((D, D), lambda i: (0, 0)), vec, vec,
                  pl.BlockSpec((D, LANES), lambda i: (0, 0)),
                  pl.BlockSpec((1, LANES), lambda i: (0, 0))],
        out_specs=[row, row, small, small, pl.BlockSpec((8, LANES), lambda i: (0, 0))],
        out_shape=[jax.ShapeDtypeStruct((T, D), F32), jax.ShapeDtypeStruct((T, D), F32),
                   jax.ShapeDtypeStruct((T, LANES), jnp.int32), jax.ShapeDtypeStruct((T, LANES), F32),
                   jax.ShapeDtypeStruct((8, LANES), F32)],
        scratch_shapes=[pltpu.VMEM((1, LANES), F32)],
        compiler_params=_cp(("arbitrary",)),
        name="mix_out",
    )(m, x, g1, sh2, sc2, w_o, ln_g, ln_b, w_r, b_r)


def _gather_kernel(cnt_ref, blk_ref, idx_ref, idxn_ref, x_hbm, o_ref, buf, sem):
    j = pl.program_id(0)
    nb = pl.num_programs(0)
    slot = j % 2
    jn = jnp.minimum(j + 1, nb - 1)

    def rows8(step):
        return pl.multiple_of((cnt_ref[step] + 7) // 8 * 8, 8)

    def issue_block(idx_r, n8, sl):
        def issue(g, _):
            for u in range(8):
                r = g * 8 + u
                tok = idx_r[0, 0, r]
                pltpu.make_async_copy(x_hbm.at[pl.ds(tok, 1), :], buf.at[sl, pl.ds(r, 1), :], sem.at[sl]).start()
            return 0

        lax.fori_loop(0, n8 // 8, issue, 0)

    @pl.when(j == 0)
    def _():
        buf[...] = jnp.zeros_like(buf)
        issue_block(idx_ref, rows8(0), 0)

    @pl.when((j + 1 < nb) & (cnt_ref[jn] > 0))
    def _():
        issue_block(idxn_ref, rows8(jn), 1 - slot)

    @pl.when(cnt_ref[j] > 0)
    def _():
        n8 = rows8(j)
        pltpu.make_async_copy(x_hbm.at[pl.ds(0, n8), :], buf.at[slot, pl.ds(0, n8), :], sem.at[slot]).wait()
        o_ref[...] = buf[slot].astype(o_ref.dtype)


def _moe_gather(x, idx, cnt, blk):
    T, D = x.shape
    P = idx.shape[0]
    gb = MOE_SUB
    nb = P // gb
    grid_spec = pltpu.PrefetchScalarGridSpec(
        num_scalar_prefetch=2,
        grid=(nb,),
        in_specs=[pl.BlockSpec((1, 1, gb), lambda j, cnt, blk: (blk[j], 0, 0), memory_space=pltpu.SMEM),
                  pl.BlockSpec((1, 1, gb), lambda j, cnt, blk: (blk[jnp.minimum(j + 1, nb - 1)], 0, 0),
                               memory_space=pltpu.SMEM),
                  pl.BlockSpec(memory_space=pl.ANY)],
        out_specs=pl.BlockSpec((gb, D), lambda j, cnt, blk: (blk[j], 0)),
        scratch_shapes=[pltpu.VMEM((2, gb, D), F32), pltpu.SemaphoreType.DMA((2,))],
    )
    idx3 = idx.reshape(nb, 1, gb)
    return pl.pallas_call(
        _gather_kernel,
        grid_spec=grid_spec,
        out_shape=jax.ShapeDtypeStruct((P, D), BF16),
        compiler_params=pltpu.CompilerParams(dimension_semantics=("arbitrary",), vmem_limit_bytes=VMEM_LIMIT,
                                             disable_bounds_checks=True),
        name="moe_gather",
    )(cnt, blk, idx3, idx3, x)


def _moe_kernel(be_ref, nu_ref, nv_ref, x_ref, wg_ref, wu_ref, wd_ref, bg_ref, bu_ref, bd_ref, y_ref,
                a_scr):
    i = pl.program_id(0)
    s = pl.program_id(1)
    nf = pl.num_programs(1) // 2
    valid = nv_ref[i]

    @pl.when((s < nf) & (valid > 0))
    def _():
        def build(base, m):
            rows = pl.ds(base, m)
            xs = x_ref[rows, :].astype(F32)
            gate = jnp.minimum(_dot(xs, wg_ref[...]) + bg_ref[...], SWIGLU_LIMIT)
            up = jnp.clip(_dot(xs, wu_ref[...]) + bu_ref[...], -SWIGLU_LIMIT, SWIGLU_LIMIT)
            a_scr[s, rows, :] = ((up + 1.0) * (gate * jax.nn.sigmoid(SWIGLU_ALPHA * gate))).astype(BF16)

        _for_sub_blocks(valid, build, None)

    @pl.when((s >= nf) & (valid > 0))
    def _():
        def write(base, m):
            rows = pl.ds(base, m)
            a = jnp.concatenate([a_scr[f, rows, :] for f in range(a_scr.shape[0])], axis=1)
            y_ref[rows, :] = _dot(a.astype(F32), wd_ref[...]) + bd_ref[...]

        def clear(base, m):
            y_ref[pl.ds(base, m), :] = jnp.zeros((m, y_ref.shape[1]), F32)

        _for_sub_blocks(valid, write, clear)


def _for_sub_blocks(valid, work, rest):
    for sbk in range(MOE_BLOCK // MOE_SUB):
        base = sbk * MOE_SUB
        left = valid - base
        for m in range(MOE_TAIL, MOE_SUB + 1, MOE_TAIL):
            @pl.when((left > m - MOE_TAIL) & ((left <= m) if m < MOE_SUB else (left > m - MOE_TAIL)))
            def _(m=m):
                work(base, m)
                if rest is not None and m < MOE_SUB:
                    rest(base + m, MOE_SUB - m)

        if rest is not None:
            @pl.when(left <= 0)
            def _():
                rest(base, MOE_SUB)


def _moe_experts(xs, blk_e, nused, nvalid, w_gu, b_gu, w_dn, b_dn):
    P, D = xs.shape
    F = w_dn.shape[2]
    NB = P // MOE_BLOCK
    NF = F // MOE_FT
    ND = D // MOE_FT

    def eff(i, nu):
        return jnp.minimum(i, nu[0] - 1)

    def fa(i, s, nu):
        return jnp.where(i < nu[0], jnp.minimum(s, NF - 1), NF - 1)

    def fb(i, s, nu):
        return jnp.where(i < nu[0], jnp.maximum(s - NF, 0), ND - 1)

    grid_spec = pltpu.PrefetchScalarGridSpec(
        num_scalar_prefetch=3,
        grid=(NB, NF + ND),
        in_specs=[
            pl.BlockSpec((MOE_BLOCK, D), lambda i, s, be, nu, nv: (eff(i, nu), 0)),
            pl.BlockSpec((None, None, D, MOE_FT), lambda i, s, be, nu, nv: (0, be[eff(i, nu)], 0, fa(i, s, nu))),
            pl.BlockSpec((None, None, D, MOE_FT),
                         lambda i, s, be, nu, nv: (0, be[eff(i, nu)], 0, NF + fa(i, s, nu))),
            pl.BlockSpec((None, None, F, MOE_FT), lambda i, s, be, nu, nv: (0, be[eff(i, nu)], 0, fb(i, s, nu))),
            pl.BlockSpec((None, 1, MOE_FT), lambda i, s, be, nu, nv: (be[eff(i, nu)], 0, fa(i, s, nu))),
            pl.BlockSpec((None, 1, MOE_FT), lambda i, s, be, nu, nv: (be[eff(i, nu)], 0, NF + fa(i, s, nu))),
            pl.BlockSpec((None, 1, MOE_FT), lambda i, s, be, nu, nv: (be[eff(i, nu)], 0, fb(i, s, nu))),
        ],
        out_specs=pl.BlockSpec((MOE_BLOCK, MOE_FT), lambda i, s, be, nu, nv: (eff(i, nu), fb(i, s, nu))),
        scratch_shapes=[pltpu.VMEM((NF, MOE_BLOCK, MOE_FT), BF16)],
    )
    assert NF == ND
    return pl.pallas_call(
        _moe_kernel,
        grid_spec=grid_spec,
        out_shape=jax.ShapeDtypeStruct((P, D), F32),
        compiler_params=_cp(("arbitrary", "arbitrary"), MOE_VMEM_LIMIT),
        name="moe_experts",
    )(blk_e, nused, nvalid, xs, w_gu, w_gu, w_dn, b_gu, b_gu, b_dn)


def _combine_kernel(idx_ref, idxn_ref, y_hbm, x1_ref, g2_ref, rw_ref, lg_ref, lb_ref, o_ref, buf, sem):
    i = pl.program_id(0)
    nt = pl.num_programs(0)
    slot = i % 2
    tm = x1_ref.shape[0]

    def issue_tile(idx_r, sl):
        def issue(r, _):
            for k in range(TOP_K):
                d = idx_r[0, 0, k * tm + r]
                pltpu.make_async_copy(y_hbm.at[pl.ds(d, 1), :], buf.at[sl, k, pl.ds(r, 1), :], sem.at[sl]).start()
            return 0

        lax.fori_loop(0, tm, issue, 0, unroll=2)

    @pl.when(i == 0)
    def _():
        issue_tile(idx_ref, 0)

    @pl.when(i + 1 < nt)
    def _():
        issue_tile(idxn_ref, 1 - slot)

    for k in range(TOP_K):
        pltpu.make_async_copy(y_hbm.at[pl.ds(0, tm), :], buf.at[slot, k], sem.at[slot]).wait()
    rw = rw_ref[...]
    moe = rw[:, 0:1] * buf[slot, 0]
    for k in range(1, TOP_K):
        moe = moe + rw[:, k:k + 1] * buf[slot, k]
    o_ref[...] = _layernorm(DN_ALPHA * x1_ref[...] + g2_ref[...] * moe, lg_ref[...], lb_ref[...])


def _moe_combine(y, dest, x1, g2, rw, ln_g, ln_b, *, seq_len):
    T, D = x1.shape
    tm = 128
    nt = T // tm
    idx = dest.reshape(nt, tm, TOP_K).transpose(0, 2, 1).reshape(nt, 1, TOP_K * tm)
    if seq_len > 0:
        seq_tiles = seq_len // tm
        mod_spec = pl.BlockSpec((None, 1, D), lambda i: (i // seq_tiles, 0, 0))
    else:
        mod_spec = pl.BlockSpec((tm, D), lambda i: (i, 0))
    row = pl.BlockSpec((tm, D), lambda i: (i, 0))
    vec = pl.BlockSpec((1, D), lambda i: (0, 0))
    return pl.pallas_call(
        _combine_kernel,
        grid=(nt,),
        in_specs=[pl.BlockSpec((1, 1, TOP_K * tm), lambda i: (i, 0, 0), memory_space=pltpu.SMEM),
                  pl.BlockSpec((1, 1, TOP_K * tm), lambda i: (jnp.minimum(i + 1, nt - 1), 0, 0),
                               memory_space=pltpu.SMEM),
                  pl.BlockSpec(memory_space=pl.ANY),
                  row, mod_spec, pl.BlockSpec((tm, LANES), lambda i: (i, 0)), vec, vec],
        out_specs=row,
        out_shape=jax.ShapeDtypeStruct((T, D), F32),
        scratch_shapes=[pltpu.VMEM((2, TOP_K, tm, D), F32), pltpu.SemaphoreType.DMA((2,))],
        compiler_params=pltpu.CompilerParams(dimension_semantics=("arbitrary",), vmem_limit_bytes=VMEM_LIMIT,
                                             disable_bounds_checks=True),
        name="moe_combine",
    )(idx, idx, y, x1, g2, rw, ln_g, ln_b)


def _lookup(table, idx):
    hit = idx[..., None] == jnp.arange(N_EXPERTS, dtype=jnp.int32)
    return jnp.sum(jnp.where(hit, table, 0), axis=-1)


def _route(experts, ranks, counts):
    T = experts.shape[0]
    A = T * TOP_K
    flat_e = experts.reshape(A)
    rank = ranks.reshape(A)
    nblk = (counts + MOE_BLOCK - 1) // MOE_BLOCK
    blk_end = jnp.cumsum(nblk)
    blk_start = blk_end - nblk
    dest = _lookup(blk_start, flat_e) * MOE_BLOCK + rank
    NB = -(-A // MOE_BLOCK) + N_EXPERTS
    idx = jnp.zeros((NB * MOE_BLOCK,), jnp.int32).at[dest].set(
        jnp.arange(A, dtype=jnp.int32) // TOP_K, unique_indices=True, mode='promise_in_bounds')
    blk_e = jnp.minimum(jnp.searchsorted(blk_end, jnp.arange(NB, dtype=jnp.int32), side='right'),
                        N_EXPERTS - 1).astype(jnp.int32)
    nused = blk_end[-1:].astype(jnp.int32)
    blk = jnp.arange(NB, dtype=jnp.int32)
    nvalid = jnp.clip(counts[blk_e] - (blk - blk_start[blk_e]) * MOE_BLOCK, 0, MOE_BLOCK)
    nvalid = jnp.where(blk < nused[0], nvalid, 0).astype(jnp.int32)
    per = MOE_BLOCK // MOE_SUB
    gblk = jnp.arange(NB * per, dtype=jnp.int32)
    gcnt = jnp.clip(jnp.repeat(nvalid, per) - (gblk % per) * MOE_SUB, 0, MOE_SUB).astype(jnp.int32)
    order = jnp.argsort(gcnt == 0, stable=True).astype(jnp.int32)
    nnz = jnp.sum(gcnt > 0)
    gvisit = jnp.where(gblk < nnz, order, order[jnp.maximum(nnz - 1, 0)])
    gcnt = jnp.where(gblk < nnz, gcnt[order], 0)
    return dest.reshape(T, TOP_K), idx, blk_e, nused, nvalid, gcnt, gvisit


def _block_diag_in(bb):
    gpc = S5_GROUPS // S5_CHUNKS
    x = bb.reshape(S5_CHUNKS, gpc, S5_GROUP, S5_STATE)
    eye = jnp.eye(gpc, dtype=bb.dtype)
    return jnp.einsum('agcn,gh->agchn', x, eye).reshape(S5_CHUNKS, S5_CHUNK_IN, S5_CHUNK_STATE)


def _block_diag_out(cc):
    gpc = S5_GROUPS // S5_CHUNKS
    x = cc.reshape(S5_CHUNKS, gpc, S5_GROUP, S5_STATE)
    eye = jnp.eye(gpc, dtype=cc.dtype)
    return jnp.einsum('agcn,gh->ahngc', x, eye).reshape(S5_CHUNKS, S5_CHUNK_STATE, S5_CHUNK_IN)


def kernel(x_prompt, x_sample, c_prompt, c_sample, cache_k, cache_v, cache_logf, state_s5_re, state_s5_im, page_table, w_ada, b_ada, w_in, fox_b_f, s5_a_re, s5_a_im, s5_log_dt, s5_b_re, s5_b_im, s5_c_re, s5_c_im, s5_d, w_glu, b_glu, p_a, p_b, w_o, ln1_g, ln1_b, w_router, b_router, w_gu, b_gu, w_dn, b_dn, ln2_g, ln2_b):
    B, L, D = x_prompt.shape
    Bd, Ld, _ = x_sample.shape
    Tp, Ts = B * L, Bd * Ld
    W = HEADS * HEAD_DIM
    n_phys = cache_k.shape[1]

    n_c = B + Bd
    c_all = jnp.concatenate([c_prompt, c_sample, jnp.zeros((-n_c % 8, D), F32)], axis=0)
    mod = _adaln(c_all, w_ada[0], b_ada)
    mp = [mod[:B, i * D:(i + 1) * D].reshape(B, 1, D) for i in range(6)]
    ms = [jnp.repeat(mod[B:n_c, i * D:(i + 1) * D], Ld, axis=0) for i in range(6)]

    wit = w_in[0].T
    f_lo = 3 * W
    wmt = jnp.concatenate([wit[:f_lo], wit[f_lo + HEADS:]], axis=0).astype(BF16)
    wft = jnp.pad(wit[f_lo:f_lo + HEADS], ((0, LANES - HEADS), (0, 0))).astype(BF16)
    bf = jnp.pad(fox_b_f, ((0, 0), (0, LANES - HEADS)))
    hp, logf_p, cT_p = _in_proj(x_prompt.reshape(Tp, D), mp[0], mp[1], wmt, wft.T, wft, bf, bf.T,
                                seq_len=L, seg=0)
    hs, logf_s, c_s = _in_proj(x_sample.reshape(Ts, D), ms[0], ms[1], wmt, wft.T, wft, bf, bf.T,
                               seq_len=0, seg=Ld)

    lam_r, lam_i, bb_r, bb_i = _s5_params(s5_a_re[0], s5_a_im[0], s5_log_dt[0][:, None],
                                          s5_b_re[0].transpose(0, 2, 1), s5_b_im[0].transpose(0, 2, 1))
    lam_r = lam_r.reshape(S5_CHUNKS, S5_CHUNK_STATE)
    lam_i = lam_i.reshape(S5_CHUNKS, S5_CHUNK_STATE)
    wb = jnp.concatenate([_block_diag_in(bb_r), _block_diag_in(bb_i)], axis=2).astype(BF16)
    wcr = _block_diag_out(s5_c_re[0]).astype(BF16)
    wci = _block_diag_out(s5_c_im[0]).astype(BF16)
    wgl = w_glu[0].astype(BF16)
    s5_args = (lam_r, lam_i, wb, wcr, wci, s5_d, wgl, b_glu)
    zeros_state = jnp.zeros((B, S5_CHUNKS, S5_CHUNK_STATE), F32)
    s5_t = 256
    ssm_p, sr_p, si_p = _s5(hp, 3, zeros_state, zeros_state, *s5_args, nseq=1, slen=s5_t, chunks=L // s5_t)
    ssm_s, sr_s, si_s = _s5(hs, 3, state_s5_re[0].reshape(Bd, S5_CHUNKS, S5_CHUNK_STATE),
                            state_s5_im[0].reshape(Bd, S5_CHUNKS, S5_CHUNK_STATE), *s5_args,
                            nseq=32, slen=Ld, chunks=1)

    att_p = _fox_prompt(hp, cT_p, B, L)
    scale = HEAD_DIM ** -0.5
    q_s = hs[:, :W].reshape(Bd, Ld, HEADS, HEAD_DIM) * scale
    k_s = hs[:, W:2 * W].reshape(Bd, Ld, HEADS, HEAD_DIM)
    v_s = hs[:, 2 * W:3 * W].reshape(Bd, Ld, HEADS, HEAD_DIM)
    qe = q_s.reshape(Bd, Ld * HEADS, HEAD_DIM)
    qbd = jnp.einsum('bqgd,gh->bqghd', q_s, jnp.eye(HEADS, dtype=F32)).reshape(Bd, Ld * HEADS, W).astype(BF16)
    kn = jnp.broadcast_to(k_s[:, :, None], (Bd, Ld, Ld, HEADS, HEAD_DIM)).reshape(Bd, Ld, Ld * HEADS, HEAD_DIM)
    vn = jnp.broadcast_to(v_s[:, :, None], (Bd, Ld, Ld, HEADS, HEAD_DIM)).reshape(Bd, Ld, Ld * HEADS, HEAD_DIM)
    cn = c_s[:, :HEADS].reshape(Bd, Ld, HEADS)
    bn = jnp.broadcast_to(cn.transpose(0, 2, 1)[:, None], (Bd, Ld, HEADS, Ld)).reshape(Bd, Ld * HEADS, Ld)
    bn = jnp.pad(bn, ((0, 0), (0, 0), (0, LANES - Ld)))
    ckt = jnp.transpose(cache_k[0], (0, 2, 3, 1)).reshape(n_phys, W, PAGE)
    cvt = jnp.transpose(cache_v[0], (0, 2, 3, 1)).reshape(n_phys, W, PAGE)
    clft = jnp.transpose(cache_logf[0], (0, 2, 1))
    att_s = _fox_sample(page_table, ckt, cvt, clft, qbd, qe, kn, vn, bn)
    att_s = att_s.reshape(Ts, W).astype(BF16)

    pab, pbb, wob = p_a[0].astype(BF16), p_b[0].astype(BF16), w_o[0].astype(BF16)
    wr = jnp.pad(w_router[0], ((0, 0), (0, LANES - N_EXPERTS)))
    br = jnp.pad(b_router, ((0, 0), (0, LANES - N_EXPERTS)), constant_values=NEG)
    m_p = _mix_gate(att_p, ssm_p, hp, pab, pbb)
    m_s = _mix_gate(att_s, ssm_s, hs, pab, pbb)
    x1_p, u2_p, ri_p, rw_p, cnt_p = _mix_out(m_p, x_prompt.reshape(Tp, D), mp[2], mp[3], mp[4], wob, ln1_g, ln1_b,
                                             wr, br, seq_len=L)
    x1_s, u2_s, ri_s, rw_s, cnt_s = _mix_out(m_s, x_sample.reshape(Ts, D), ms[2], ms[3], ms[4], wob, ln1_g, ln1_b,
                                             wr, br, seq_len=0)

    u2 = jnp.concatenate([u2_p, u2_s], axis=0)
    cnt_p = cnt_p[0, :N_EXPERTS].astype(jnp.int32)
    cnt_s = cnt_s[0, :N_EXPERTS].astype(jnp.int32)
    ex_p, ex_s = ri_p[:, :TOP_K], ri_s[:, :TOP_K]
    experts = jnp.concatenate([ex_p, ex_s], axis=0)
    ranks = jnp.concatenate([ri_p[:, TOP_K:2 * TOP_K], ri_s[:, TOP_K:2 * TOP_K] + _lookup(cnt_p, ex_s)], axis=0)
    dest, idx, blk_e, nused, nvalid, gcnt, gvisit = _route(experts, ranks, cnt_p + cnt_s)
    xs = _moe_gather(u2, idx, gcnt, gvisit)
    y = _moe_experts(xs, blk_e, nused, nvalid, w_gu, b_gu[0][:, None, :], w_dn, b_dn[0][:, None, :])
    x2_p = _moe_combine(y, dest[:Tp], x1_p, mp[5], rw_p, ln2_g, ln2_b, seq_len=L)
    x2_s = _moe_combine(y, dest[Tp:], x1_s, ms[5], rw_s, ln2_g, ln2_b, seq_len=0)

    def heads(t, n, l):
        return t.reshape(1, n, l, HEADS, HEAD_DIM)

    def state(s, n):
        return s.reshape(1, n, S5_GROUPS, S5_STATE)

    return (x2_p.reshape(B, L, D), x2_s.reshape(Bd, Ld, D),
            heads(hp[:, W:2 * W], B, L), heads(hp[:, 2 * W:3 * W], B, L),
            logf_p[:, :HEADS].reshape(1, B, L, HEADS), state(sr_p, B), state(si_p, B),
            heads(hs[:, W:2 * W], Bd, Ld), heads(hs[:, 2 * W:3 * W], Bd, Ld),
            logf_s[:, :HEADS].reshape(1, Bd, Ld, HEADS), state(sr_s, Bd), state(si_s, Bd))
```

```python
import functools
import math

import jax
import jax.numpy as jnp
from jax import lax
from jax.experimental import pallas as pl
from jax.experimental.pallas import tpu as pltpu

F32 = jnp.float32
BF16 = jnp.bfloat16

HEADS = 16
HEAD_DIM = 64
PAGE = 128
S5_GROUPS = 64
S5_GROUP = 16
S5_STATE = 64
N_EXPERTS = 32
TOP_K = 4
SWIGLU_LIMIT = 7.0
SWIGLU_ALPHA = 1.702
DN_ALPHA = 2.0 ** 0.25
LN_EPS = 1e-5
NEG = -1e30
LOG2E = math.log2(math.e)

LANES = 128
S5_CHUNKS = 8
S5_CHUNK_STATE = S5_GROUPS * S5_STATE // S5_CHUNKS
S5_CHUNK_IN = S5_GROUPS * S5_GROUP // S5_CHUNKS
S5_LANE_BLOCKS = S5_CHUNK_STATE // LANES
FOX_TQ = 1024
FOX_TK = 1024
MOE_BLOCK = 1536
MOE_SUB = 384
MOE_TAIL = 128
MOE_FT = 512
VMEM_LIMIT = 56 * 1024 * 1024
MOE_VMEM_LIMIT = 60 * 1024 * 1024


def _cp(sem, vmem=VMEM_LIMIT):
    return pltpu.CompilerParams(dimension_semantics=sem, vmem_limit_bytes=vmem)


def _dot(a, b):
    return jnp.dot(a, b, preferred_element_type=F32)


def _dot_nt(a, b):
    return lax.dot_general(a, b, (((1,), (1,)), ((), ())), preferred_element_type=F32)


def _log_sigmoid(x):
    return jnp.minimum(x, 0.0) - jnp.log1p(jnp.exp(-jnp.abs(x)))


def _split3(v):
    hi = v.astype(BF16)
    r = v - hi.astype(F32)
    mid = r.astype(BF16)
    lo = (r - mid.astype(F32)).astype(BF16)
    return hi, mid, lo


def _layernorm(v, g, b):
    mu = jnp.mean(v, axis=-1, keepdims=True)
    d = v - mu
    var = jnp.mean(d * d, axis=-1, keepdims=True)
    return d * lax.rsqrt(var + LN_EPS) * g + b


def _ada_kernel(c_ref, w_ref, b_ref, o_ref):
    c = c_ref[...]
    a = (c * jax.nn.sigmoid(c)).astype(BF16)
    o_ref[...] = _dot(a, w_ref[...].astype(BF16)) + b_ref[...]


def _adaln(c_all, w_ada, b_ada):
    R, D = c_all.shape
    N = w_ada.shape[1]
    tn = 1024
    return pl.pallas_call(
        _ada_kernel,
        grid=(N // tn,),
        in_specs=[pl.BlockSpec((R, D), lambda j: (0, 0)),
                  pl.BlockSpec((D, tn), lambda j: (0, j)),
                  pl.BlockSpec((1, tn), lambda j: (0, j))],
        out_specs=pl.BlockSpec((R, tn), lambda j: (0, j)),
        out_shape=jax.ShapeDtypeStruct((R, N), F32),
        compiler_params=_cp(("arbitrary",)),
        name="adaln",
    )(c_all, w_ada, b_ada)


def _in_kernel(x_ref, sh_ref, sc_ref, wm_ref, wf_ref, wft_ref, bf_ref, bft_ref,
               h_ref, logf_ref, cum_ref, u_scr, carry_scr, *, seq_tiles, seg):
    i = pl.program_id(0)
    j = pl.program_id(1)
    tm = x_ref.shape[0]

    @pl.when(j == 0)
    def _():
        u = x_ref[...] * (1.0 + sc_ref[...]) + sh_ref[...]
        ub = u.astype(BF16)
        u_scr[...] = ub
        lf = _log_sigmoid(_dot(ub, wf_ref[...]) + bf_ref[...])
        logf_ref[...] = lf
        r = lax.broadcasted_iota(jnp.int32, (tm, tm), 0)
        c = lax.broadcasted_iota(jnp.int32, (tm, tm), 1)
        if seq_tiles > 0:
            lft = _log_sigmoid(_dot_nt(wft_ref[...], ub) + bft_ref[...])
            tri = (r <= c).astype(BF16)
            hi, mid, lo = _split3(lft)
            cs = _dot(hi, tri) + _dot(mid, tri) + _dot(lo, tri)

            @pl.when(i % seq_tiles == 0)
            def _():
                carry_scr[...] = jnp.zeros_like(carry_scr)

            cs = cs + carry_scr[...]
            cum_ref[...] = cs
            carry_scr[...] = cs[:, tm - 1:tm]
        else:
            tri = ((c <= r) & (c // seg == r // seg)).astype(BF16)
            hi, mid, lo = _split3(lf)
            cum_ref[...] = _dot(tri, hi) + _dot(tri, mid) + _dot(tri, lo)

    h_ref[...] = _dot_nt(u_scr[...], wm_ref[...])


def _in_proj(x, shift, scale, wmt, wf, wft, bf, bft, *, seq_len, seg):
    T, D = x.shape
    N = wmt.shape[0]
    tm, tn = min(1024, T), 1024
    nt = T // tm
    if seq_len > 0:
        seq_tiles = seq_len // tm
        mod_spec = pl.BlockSpec((None, 1, D), lambda i, j: (i // seq_tiles, 0, 0))
        cum_shape, cum_spec = (LANES, T), pl.BlockSpec((LANES, tm), lambda i, j: (0, i))
    else:
        seq_tiles = 0
        mod_spec = pl.BlockSpec((tm, D), lambda i, j: (i, 0))
        cum_shape, cum_spec = (T, LANES), pl.BlockSpec((tm, LANES), lambda i, j: (i, 0))
    kern = functools.partial(_in_kernel, seq_tiles=seq_tiles, seg=seg)
    return pl.pallas_call(
        kern,
        grid=(nt, N // tn),
        in_specs=[pl.BlockSpec((tm, D), lambda i, j: (i, 0)), mod_spec, mod_spec,
                  pl.BlockSpec((tn, D), lambda i, j: (j, 0)),
                  pl.BlockSpec((D, LANES), lambda i, j: (0, 0)),
                  pl.BlockSpec((LANES, D), lambda i, j: (0, 0)),
                  pl.BlockSpec((1, LANES), lambda i, j: (0, 0)),
                  pl.BlockSpec((LANES, 1), lambda i, j: (0, 0))],
        out_specs=[pl.BlockSpec((tm, tn), lambda i, j: (i, j)),
                   pl.BlockSpec((tm, LANES), lambda i, j: (i, 0)),
                   cum_spec],
        out_shape=[jax.ShapeDtypeStruct((T, N), F32),
                   jax.ShapeDtypeStruct((T, LANES), F32),
                   jax.ShapeDtypeStruct(cum_shape, F32)],
        scratch_shapes=[pltpu.VMEM((tm, D), BF16), pltpu.VMEM((LANES, 1), F32)],
        compiler_params=_cp(("arbitrary", "arbitrary")),
        name="in_proj",
    )(x, shift, scale, wmt, wf, wft, bf, bft)


def _transpose_kernel(x_ref, o_ref):
    o_ref[...] = x_ref[...].T


def _heads_minor_time(h, col_block, B, L):
    W = HEADS * HEAD_DIM
    tm = min(512, L)
    nt = L // tm
    return pl.pallas_call(
        _transpose_kernel,
        grid=(B, nt),
        in_specs=[pl.BlockSpec((tm, W), lambda b, i: (b * nt + i, col_block))],
        out_specs=pl.BlockSpec((None, W, tm), lambda b, i: (b, 0, i)),
        out_shape=jax.ShapeDtypeStruct((B, W, L), F32),
        compiler_params=_cp(("parallel", "parallel")),
        name="heads_minor_time",
    )(h)


def _fox_prompt_kernel(qt_ref, kt_ref, q_ref, k_ref, v_ref, c_ref, o_ref, qs, m_scr, l_scr, acc, *, ratio):
    hp = pl.program_id(1)
    t = pl.program_id(2)
    qi = qt_ref[t]
    kv = kt_ref[t]
    tq, tk = q_ref.shape[0], k_ref.shape[0]
    lane = lax.broadcasted_iota(jnp.int32, (tq, LANES), 1)

    @pl.when(kv == 0)
    def _():
        q2 = q_ref[...] * (HEAD_DIM ** -0.5 * LOG2E)
        qs[0] = jnp.where(lane < HEAD_DIM, q2, 0.0).astype(BF16)
        qs[1] = jnp.where(lane >= HEAD_DIM, q2, 0.0).astype(BF16)
        m_scr[...] = jnp.full_like(m_scr, NEG)
        l_scr[...] = jnp.zeros_like(l_scr)
        acc[...] = jnp.zeros_like(acc)

    def block(diagonal):
        kb = k_ref[...].astype(BF16)
        vb = v_ref[...].astype(BF16)
        r0 = (2 * hp) % 8
        for a in range(2):
            crow = c_ref[pl.ds(r0 + a, 1), :] * LOG2E
            s = _dot_nt(qs[a], kb) - crow
            if diagonal:
                row = lax.broadcasted_iota(jnp.int32, (tq, tk), 0)
                col = lax.broadcasted_iota(jnp.int32, (tq, tk), 1) + (kv * tk - qi * tq)
                s = jnp.where(col <= row, s, NEG)
            cols = [s[:, c * LANES:(c + 1) * LANES] for c in range(tk // LANES)]
            mx = cols[0]
            for sc in cols[1:]:
                mx = jnp.maximum(mx, sc)
            m_prev = m_scr[a]
            m_new = jnp.maximum(m_prev, jnp.max(mx, axis=1, keepdims=True))
            alpha = jnp.exp2(m_prev - m_new)
            ps = [jnp.exp2(sc - m_new) for sc in cols]
            lsum = ps[0]
            for pc in ps[1:]:
                lsum = lsum + pc
            l_scr[a] = alpha * l_scr[a] + lsum
            p = jnp.concatenate(ps, axis=1).astype(BF16)
            acc[a] = alpha * acc[a] + _dot(p, vb)
            m_scr[a] = m_new

    @pl.when(kv < qi * ratio)
    def _():
        block(False)

    @pl.when(kv >= qi * ratio)
    def _():
        block(True)

    @pl.when(kv == (qi + 1) * ratio - 1)
    def _():
        l0 = jnp.sum(l_scr[0], axis=1, keepdims=True)
        l1 = jnp.sum(l_scr[1], axis=1, keepdims=True)
        o = jnp.where(lane < HEAD_DIM, acc[0] / l0, acc[1] / l1)
        o_ref[...] = o.astype(o_ref.dtype)


def _fox_prompt(h, cT, B, L):
    tq, tk = FOX_TQ, FOX_TK
    ratio = tq // tk
    nq, nk = L // tq, L // tk
    npairs = HEADS // 2
    steps = [(qi, kv) for qi in range(nq) for kv in range((qi + 1) * ratio)]
    qt = jnp.asarray([p[0] for p in steps], jnp.int32)
    kt = jnp.asarray([p[1] for p in steps], jnp.int32)
    grid_spec = pltpu.PrefetchScalarGridSpec(
        num_scalar_prefetch=2,
        grid=(B, npairs, len(steps)),
        in_specs=[
            pl.BlockSpec((tq, LANES), lambda b, hp, t, qt, kt: (b * nq + qt[t], hp)),
            pl.BlockSpec((tk, LANES), lambda b, hp, t, qt, kt: (b * nk + kt[t], npairs + hp)),
            pl.BlockSpec((tk, LANES), lambda b, hp, t, qt, kt: (b * nk + kt[t], 2 * npairs + hp)),
            pl.BlockSpec((8, tk), lambda b, hp, t, qt, kt: (hp // 4, b * nk + kt[t])),
        ],
        out_specs=pl.BlockSpec((tq, LANES), lambda b, hp, t, qt, kt: (b * nq + qt[t], hp)),
        scratch_shapes=[pltpu.VMEM((2, tq, LANES), BF16), pltpu.VMEM((2, tq, LANES), F32),
                        pltpu.VMEM((2, tq, LANES), F32), pltpu.VMEM((2, tq, LANES), F32)],
    )
    return pl.pallas_call(
        functools.partial(_fox_prompt_kernel, ratio=ratio),
        grid_spec=grid_spec,
        out_shape=jax.ShapeDtypeStruct((B * L, HEADS * HEAD_DIM), BF16),
        compiler_params=_cp(("parallel", "parallel", "arbitrary")),
        name="fox_prompt",
    )(qt, kt, h, h, h, cT)


def _fox_sample_kernel(pt_ref, *refs, n_pages):
    k_refs = refs[:n_pages]
    v_refs = refs[n_pages:2 * n_pages]
    lf_refs = refs[2 * n_pages:3 * n_pages]
    qe_ref, kn_ref, vn_ref, bn_ref, o_ref = refs[3 * n_pages:]
    nq = qe_ref.shape[0] // HEADS
    nrow = nq * HEADS
    width = HEADS * HEAD_DIM

    xs = jnp.concatenate([r[...] for r in lf_refs], axis=0)
    hi, mid, lo = _split3(xs)
    after = (lax.broadcasted_iota(jnp.int32, (PAGE, PAGE), 0)
             > lax.broadcasted_iota(jnp.int32, (PAGE, PAGE), 1)).astype(BF16)
    within = _dot(hi, after) + _dot(mid, after) + _dot(lo, after)
    n = n_pages * HEADS
    pr = lax.broadcasted_iota(jnp.int32, (n, n), 0)
    pc = lax.broadcasted_iota(jnp.int32, (n, n), 1)
    later = ((pc // HEADS > pr // HEADS) & (pc % HEADS == pr % HEADS)).astype(BF16)
    beyond = jnp.sum(_dot(later, hi) + _dot(later, mid) + _dot(later, lo), axis=1, keepdims=True)
    bias_rows = within + beyond
    bias = jnp.concatenate([bias_rows[j * HEADS:(j + 1) * HEADS] for j in range(n_pages)], axis=1)
    bias = jnp.concatenate([bias] * nq, axis=0)

    qe = qe_ref[...]
    rh = lax.broadcasted_iota(jnp.int32, (nrow, width), 0) % HEADS
    ch = lax.broadcasted_iota(jnp.int32, (nrow, width), 1) // HEAD_DIM
    tile = (lax.broadcasted_iota(jnp.int32, (HEAD_DIM, width), 0)
            == lax.broadcasted_iota(jnp.int32, (HEAD_DIM, width), 1) % HEAD_DIM).astype(BF16)
    qbd = jnp.where(rh == ch, _dot(qe.astype(BF16), tile), 0.0).astype(BF16)

    kt = jnp.concatenate([r[...].astype(BF16) for r in k_refs], axis=1)
    s = _dot(qbd, kt) + bias

    q_of_row = lax.broadcasted_iota(jnp.int32, (nrow, 1), 0) // HEADS
    bn = bn_ref[...]
    sn = []
    for j in range(nq):
        sj = jnp.sum(qe * kn_ref[j], axis=1, keepdims=True) - bn[:, j:j + 1]
        sn.append(jnp.where(q_of_row >= j, sj, NEG))
    m = jnp.max(s, axis=1, keepdims=True)
    for sj in sn:
        m = jnp.maximum(m, sj)
    p = jnp.exp(s - m)
    l = jnp.sum(p, axis=1, keepdims=True)

    vt = jnp.concatenate([r[...].astype(BF16) for r in v_refs], axis=1)
    o_all = _dot_nt(p.astype(BF16), vt)
    o_hi, o_mid, o_lo = _split3(jnp.where(rh == ch, o_all, 0.0))
    fold = (lax.broadcasted_iota(jnp.int32, (width, HEAD_DIM), 0) % HEAD_DIM
            == lax.broadcasted_iota(jnp.int32, (width, HEAD_DIM), 1)).astype(BF16)
    o = _dot(o_hi, fold) + _dot(o_mid, fold) + _dot(o_lo, fold)
    for j in range(nq):
        pj = jnp.exp(sn[j] - m)
        l = l + pj
        o = o + pj * vn_ref[j]
    o_ref[...] = o / l


def _fox_sample(page_table, ckt, cvt, clft, qe, kn, vn, bn):
    Bd, n_pages = page_table.shape
    nrow = qe.shape[1]
    nq = nrow // HEADS
    width = HEADS * HEAD_DIM

    def page_map(j):
        return lambda b, pt: (pt[b * n_pages + j], 0, 0)

    per_row3 = lambda b, pt: (b, 0, 0)
    per_row4 = lambda b, pt: (b, 0, 0, 0)
    grid_spec = pltpu.PrefetchScalarGridSpec(
        num_scalar_prefetch=1,
        grid=(Bd,),
        in_specs=([pl.BlockSpec((None, width, PAGE), page_map(j)) for j in range(n_pages)]
                  + [pl.BlockSpec((None, width, PAGE), page_map(j)) for j in range(n_pages)]
                  + [pl.BlockSpec((None, HEADS, PAGE), page_map(j)) for j in range(n_pages)]
                  + [pl.BlockSpec((None, nrow, HEAD_DIM), per_row3),
                     pl.BlockSpec((None, nq, nrow, HEAD_DIM), per_row4),
                     pl.BlockSpec((None, nq, nrow, HEAD_DIM), per_row4),
                     pl.BlockSpec((None, nrow, LANES), per_row3)]),
        out_specs=pl.BlockSpec((None, nrow, HEAD_DIM), per_row3),
    )
    return pl.pallas_call(
        functools.partial(_fox_sample_kernel, n_pages=n_pages),
        grid_spec=grid_spec,
        out_shape=jax.ShapeDtypeStruct((Bd, nrow, HEAD_DIM), F32),
        compiler_params=_cp(("parallel",)),
        name="fox_sample",
    )(page_table.reshape(-1), *([ckt] * n_pages), *([cvt] * n_pages), *([clft] * n_pages),
      qe, kn, vn, bn)


def _s5_param_kernel(ar_ref, ai_ref, ldt_ref, br_ref, bi_ref, lr_ref, li_ref, bbr_ref, bbi_ref):
    ar, ai = ar_ref[...], ai_ref[...]
    dt = jnp.exp(ldt_ref[...])
    mag = jnp.exp(ar * dt)
    lr = mag * jnp.cos(ai * dt)
    li = mag * jnp.sin(ai * dt)
    den = ar * ar + ai * ai
    zr = ((lr - 1.0) * ar + li * ai) / den
    zi = (li * ar - (lr - 1.0) * ai) / den
    lr_ref[...] = lr
    li_ref[...] = li
    br, bi = br_ref[...], bi_ref[...]
    zr3, zi3 = zr[:, None, :], zi[:, None, :]
    bbr_ref[...] = zr3 * br - zi3 * bi
    bbi_ref[...] = zr3 * bi + zi3 * br


def _s5_params(a_re, a_im, log_dt, bt_re, bt_im):
    G, N = a_re.shape
    C = bt_re.shape[1]
    return pl.pallas_call(
        _s5_param_kernel,
        out_shape=[jax.ShapeDtypeStruct((G, N), F32), jax.ShapeDtypeStruct((G, N), F32),
                   jax.ShapeDtypeStruct((G, C, N), F32), jax.ShapeDtypeStruct((G, C, N), F32)],
        name="s5_params",
    )(a_re, a_im, log_dt, bt_re, bt_im)


def _s5_kernel(u_ref, h0r_ref, h0i_ref, lr_ref, li_ref, wb_ref, wcr_ref, wci_ref, d_ref, wg_ref, bg_ref,
               o_ref, sr_ref, si_ref, bur, bui, y_scr, cr, ci, *, nseq, slen):
    c_id = pl.program_id(1)
    R = u_ref.shape[0]
    u = u_ref[...]
    ub = u.astype(BF16)
    rs = [_dot(ub[:, gc * S5_CHUNK_IN:(gc + 1) * S5_CHUNK_IN], wb_ref[gc]) for gc in range(S5_CHUNKS)]
    for lb in range(S5_LANE_BLOCKS):
        lo = lb * LANES
        xr = jnp.stack([r[:, lo:lo + LANES] for r in rs], axis=0)
        xi = jnp.stack([r[:, S5_CHUNK_STATE + lo:S5_CHUNK_STATE + lo + LANES] for r in rs], axis=0)
        bur[lb] = jnp.swapaxes(xr, 0, 1)
        bui[lb] = jnp.swapaxes(xi, 0, 1)

    @pl.when(c_id == 0)
    def _():
        cr[...] = h0r_ref[...]
        ci[...] = h0i_ref[...]

    lr = lr_ref[...]
    li = li_ref[...]

    def seq_body(q, _):
        base = q * slen

        def step(t, hc):
            hr, hi = hc
            br = jnp.concatenate([bur[lb, base + t] for lb in range(S5_LANE_BLOCKS)], axis=1)
            bi = jnp.concatenate([bui[lb, base + t] for lb in range(S5_LANE_BLOCKS)], axis=1)
            nr = lr * hr - li * hi + br
            ni = lr * hi + li * hr + bi
            for lb in range(S5_LANE_BLOCKS):
                bur[lb, base + t] = nr[:, lb * LANES:(lb + 1) * LANES]
                bui[lb, base + t] = ni[:, lb * LANES:(lb + 1) * LANES]
            return nr, ni

        hr, hi = lax.fori_loop(0, slen, step, (cr[q], ci[q]), unroll=4)
        cr[q] = hr
        ci[q] = hi
        return 0

    lax.fori_loop(0, nseq, seq_body, 0)
    sr_ref[...] = cr[...]
    si_ref[...] = ci[...]

    hr_t = [jnp.swapaxes(bur[lb], 0, 1) for lb in range(S5_LANE_BLOCKS)]
    hi_t = [jnp.swapaxes(bui[lb], 0, 1) for lb in range(S5_LANE_BLOCKS)]
    for gc in range(S5_CHUNKS):
        hrb = jnp.concatenate([x[gc] for x in hr_t], axis=1).astype(BF16)
        hib = jnp.concatenate([x[gc] for x in hi_t], axis=1).astype(BF16)
        y_scr[:, gc * S5_CHUNK_IN:(gc + 1) * S5_CHUNK_IN] = _dot(hrb, wcr_ref[gc]) - _dot(hib, wci_ref[gc])
    y = y_scr[...] + d_ref[...] * u
    z = jax.nn.gelu(y)
    o_ref[...] = (z * jax.nn.sigmoid(_dot(z.astype(BF16), wg_ref[...]) + bg_ref[...])).astype(o_ref.dtype)


def _s5(h, col_block, h0r, h0i, lam_r, lam_i, wb, wcr, wci, d_skip, w_glu, b_glu, *, nseq, slen, chunks):
    T = h.shape[0]
    W = S5_GROUPS * S5_GROUP
    R = nseq * slen
    nsb = T // (R * chunks)
    n_seq_total = h0r.shape[0]
    st_spec = pl.BlockSpec((nseq, S5_CHUNKS, S5_CHUNK_STATE), lambda sb, c: (sb, 0, 0))
    full3 = lambda sb, c: (0, 0, 0)
    full2 = lambda sb, c: (0, 0)
    kern = functools.partial(_s5_kernel, nseq=nseq, slen=slen)
    return pl.pallas_call(
        kern,
        grid=(nsb, chunks),
        in_specs=[pl.BlockSpec((R, W), lambda sb, c: (sb * chunks + c, col_block)),
                  st_spec, st_spec,
                  pl.BlockSpec((S5_CHUNKS, S5_CHUNK_STATE), full2),
                  pl.BlockSpec((S5_CHUNKS, S5_CHUNK_STATE), full2),
                  pl.BlockSpec((S5_CHUNKS, S5_CHUNK_IN, 2 * S5_CHUNK_STATE), full3),
                  pl.BlockSpec((S5_CHUNKS, S5_CHUNK_STATE, S5_CHUNK_IN), full3),
                  pl.BlockSpec((S5_CHUNKS, S5_CHUNK_STATE, S5_CHUNK_IN), full3),
                  pl.BlockSpec((1, W), full2),
                  pl.BlockSpec((W, W), full2),
                  pl.BlockSpec((1, W), full2)],
        out_specs=[pl.BlockSpec((R, W), lambda sb, c: (sb * chunks + c, 0)), st_spec, st_spec],
        out_shape=[jax.ShapeDtypeStruct((T, W), BF16),
                   jax.ShapeDtypeStruct((n_seq_total, S5_CHUNKS, S5_CHUNK_STATE), F32),
                   jax.ShapeDtypeStruct((n_seq_total, S5_CHUNKS, S5_CHUNK_STATE), F32)],
        scratch_shapes=[pltpu.VMEM((S5_LANE_BLOCKS, R, S5_CHUNKS, LANES), F32),
                        pltpu.VMEM((S5_LANE_BLOCKS, R, S5_CHUNKS, LANES), F32),
                        pltpu.VMEM((R, W), F32),
                        pltpu.VMEM((nseq, S5_CHUNKS, S5_CHUNK_STATE), F32),
                        pltpu.VMEM((nseq, S5_CHUNKS, S5_CHUNK_STATE), F32)],
        compiler_params=_cp(("arbitrary", "arbitrary")),
        name="s5",
    )(h, h0r, h0i, lam_r, lam_i, wb, wcr, wci, d_skip, w_glu, b_glu)


def _mix_gate_kernel(att_ref, ssm_ref, ga_ref, gb_ref, pa_ref, pb_ref, o_ref):
    m = (jax.nn.sigmoid(ga_ref[...]) * _dot(att_ref[...], pa_ref[...])
         + jax.nn.sigmoid(gb_ref[...]) * _dot(ssm_ref[...], pb_ref[...]))
    o_ref[...] = m.astype(o_ref.dtype)


def _mix_gate(att, ssm, h, p_a, p_b):
    T, W = att.shape
    D = p_a.shape[1]
    tm = min(256, T)
    return pl.pallas_call(
        _mix_gate_kernel,
        grid=(T // tm,),
        in_specs=[pl.BlockSpec((tm, W), lambda i: (i, 0)),
                  pl.BlockSpec((tm, W), lambda i: (i, 0)),
                  pl.BlockSpec((tm, D), lambda i: (i, 2)),
                  pl.BlockSpec((tm, D), lambda i: (i, 3)),
                  pl.BlockSpec((W, D), lambda i: (0, 0)),
                  pl.BlockSpec((W, D), lambda i: (0, 0))],
        out_specs=pl.BlockSpec((tm, D), lambda i: (i, 0)),
        out_shape=jax.ShapeDtypeStruct((T, D), BF16),
        compiler_params=_cp(("parallel",)),
        name="mix_gate",
    )(att, ssm, h, h, p_a, p_b)


def _mix_out_kernel(m_ref, x_ref, g1_ref, sh2_ref, sc2_ref, wo_ref, lg_ref, lb_ref, wr_ref, br_ref,
                    x1_ref, u2_ref, ri_ref, rw_ref, cnt_ref, cnt_scr):
    out = _dot(m_ref[...], wo_ref[...])
    x1 = _layernorm(DN_ALPHA * x_ref[...] + g1_ref[...] * out, lg_ref[...], lb_ref[...])
    x1_ref[...] = x1
    u2 = x1 * (1.0 + sc2_ref[...]) + sh2_ref[...]
    u2_ref[...] = u2
    u_hi = u2.astype(BF16)
    u_lo = (u2 - u_hi.astype(F32)).astype(BF16)
    wr = wr_ref[...]
    w_hi = wr.astype(BF16)
    w_lo = (wr - w_hi.astype(F32)).astype(BF16)
    logits = _dot(u_hi, w_hi) + _dot(u_hi, w_lo) + _dot(u_lo, w_hi) + br_ref[...]
    lane = lax.broadcasted_iota(jnp.int32, logits.shape, 1)
    cur = logits
    vals, idxs = [], []
    for _ in range(TOP_K):
        mk = jnp.max(cur, axis=1, keepdims=True)
        ik = jnp.min(jnp.where(cur == mk, lane, LANES), axis=1, keepdims=True)
        vals.append(mk)
        idxs.append(ik)
        cur = jnp.where(lane == ik, -jnp.inf, cur)
    es = [jnp.exp(v - vals[0]) for v in vals]
    den = es[0] + es[1] + es[2] + es[3]
    @pl.when(pl.program_id(0) == 0)
    def _():
        cnt_scr[...] = jnp.zeros_like(cnt_scr)

    tm = logits.shape[0]
    picked = jnp.zeros(logits.shape, F32)
    for k in range(TOP_K):
        picked = picked + (lane == idxs[k]).astype(F32)
    earlier = (lax.broadcasted_iota(jnp.int32, (tm, tm), 1)
               < lax.broadcasted_iota(jnp.int32, (tm, tm), 0)).astype(BF16)
    before = _dot(earlier, picked.astype(BF16)) + cnt_scr[...]
    ri = jnp.zeros(logits.shape, jnp.int32)
    rw = jnp.zeros(logits.shape, F32)
    for k in range(TOP_K):
        rank_k = jnp.sum(jnp.where(lane == idxs[k], before, 0.0), axis=1, keepdims=True)
        ri = jnp.where(lane == k, idxs[k], ri)
        ri = jnp.where(lane == TOP_K + k, rank_k.astype(jnp.int32), ri)
        rw = jnp.where(lane == k, es[k] / den, rw)
    ri_ref[...] = ri
    rw_ref[...] = rw
    cnt_scr[...] += jnp.sum(picked, axis=0, keepdims=True)
    cnt_ref[...] = jnp.broadcast_to(cnt_scr[...], cnt_ref.shape)


def _mix_out(m, x, g1, sh2, sc2, w_o, ln_g, ln_b, w_r, b_r, *, seq_len):
    T, D = x.shape
    tm = min(256, T)
    if seq_len > 0:
        seq_tiles = seq_len // tm
        mod_spec = pl.BlockSpec((None, 1, D), lambda i: (i // seq_tiles, 0, 0))
    else:
        mod_spec = pl.BlockSpec((tm, D), lambda i: (i, 0))
    row = pl.BlockSpec((tm, D), lambda i: (i, 0))
    vec = pl.BlockSpec((1, D), lambda i: (0, 0))
    small = pl.BlockSpec((tm, LANES), lambda i: (i, 0))
    return pl.pallas_call(
        _mix_out_kernel,
        grid=(T // tm,),
        in_specs=[row, row, mod_spec, mod_spec, mod_spec,
                  pl.BlockSpec((D, D), lambda i: (0, 0)), vec, vec,
                  pl.BlockSpec((D, LANES), lambda i: (0, 0)),
                  pl.BlockSpec((1, LANES), lambda i: (0, 0))],
        out_specs=[row, row, small, small, pl.BlockSpec((8, LANES), lambda i: (0, 0))],
        out_shape=[jax.ShapeDtypeStruct((T, D), F32), jax.ShapeDtypeStruct((T, D), F32),
                   jax.ShapeDtypeStruct((T, LANES), jnp.int32), jax.ShapeDtypeStruct((T, LANES), F32),
                   jax.ShapeDtypeStruct((8, LANES), F32)],
        scratch_shapes=[pltpu.VMEM((1, LANES), F32)],
        compiler_params=_cp(("arbitrary",)),
        name="mix_out",
    )(m, x, g1, sh2, sc2, w_o, ln_g, ln_b, w_r, b_r)


def _gather_kernel(cnt_ref, blk_ref, idx_ref, idxn_ref, x_hbm, o_ref, buf, sem):
    j = pl.program_id(0)
    nb = pl.num_programs(0)
    slot = j % 2
    jn = jnp.minimum(j + 1, nb - 1)

    def rows8(step):
        return pl.multiple_of((cnt_ref[step] + 7) // 8 * 8, 8)

    def issue_block(idx_r, n8, sl):
        def issue(g, _):
            for u in range(8):
                r = g * 8 + u
                tok = idx_r[0, 0, r]
                pltpu.make_async_copy(x_hbm.at[pl.ds(tok, 1), :], buf.at[sl, pl.ds(r, 1), :], sem.at[sl]).start()
            return 0

        lax.fori_loop(0, n8 // 8, issue, 0)

    @pl.when(j == 0)
    def _():
        buf[...] = jnp.zeros_like(buf)
        issue_block(idx_ref, rows8(0), 0)

    @pl.when((j + 1 < nb) & (cnt_ref[jn] > 0))
    def _():
        issue_block(idxn_ref, rows8(jn), 1 - slot)

    @pl.when(cnt_ref[j] > 0)
    def _():
        n8 = rows8(j)
        pltpu.make_async_copy(x_hbm.at[pl.ds(0, n8), :], buf.at[slot, pl.ds(0, n8), :], sem.at[slot]).wait()
        o_ref[...] = buf[slot].astype(o_ref.dtype)


def _moe_gather(x, idx, cnt, blk):
    T, D = x.shape
    P = idx.shape[0]
    gb = MOE_SUB
    nb = P // gb
    grid_spec = pltpu.PrefetchScalarGridSpec(
        num_scalar_prefetch=2,
        grid=(nb,),
        in_specs=[pl.BlockSpec((1, 1, gb), lambda j, cnt, blk: (blk[j], 0, 0), memory_space=pltpu.SMEM),
                  pl.BlockSpec((1, 1, gb), lambda j, cnt, blk: (blk[jnp.minimum(j + 1, nb - 1)], 0, 0),
                               memory_space=pltpu.SMEM),
                  pl.BlockSpec(memory_space=pl.ANY)],
        out_specs=pl.BlockSpec((gb, D), lambda j, cnt, blk: (blk[j], 0)),
        scratch_shapes=[pltpu.VMEM((2, gb, D), F32), pltpu.SemaphoreType.DMA((2,))],
    )
    idx3 = idx.reshape(nb, 1, gb)
    return pl.pallas_call(
        _gather_kernel,
        grid_spec=grid_spec,
        out_shape=jax.ShapeDtypeStruct((P, D), BF16),
        compiler_params=pltpu.CompilerParams(dimension_semantics=("arbitrary",), vmem_limit_bytes=VMEM_LIMIT,
                                             disable_bounds_checks=True),
        name="moe_gather",
    )(cnt, blk, idx3, idx3, x)


def _moe_kernel(be_ref, nu_ref, nv_ref, x_ref, wg_ref, wu_ref, wd_ref, bg_ref, bu_ref, bd_ref, y_ref,
                a_scr):
    i = pl.program_id(0)
    s = pl.program_id(1)
    nf = pl.num_programs(1) // 2
    valid = nv_ref[i]

    @pl.when((s < nf) & (valid > 0))
    def _():
        def build(base, m):
            rows = pl.ds(base, m)
            xs = x_ref[rows, :].astype(F32)
            gate = jnp.minimum(_dot(xs, wg_ref[...]) + bg_ref[...], SWIGLU_LIMIT)
            up = jnp.clip(_dot(xs, wu_ref[...]) + bu_ref[...], -SWIGLU_LIMIT, SWIGLU_LIMIT)
            a_scr[s, rows, :] = ((up + 1.0) * (gate * jax.nn.sigmoid(SWIGLU_ALPHA * gate))).astype(BF16)

        _for_sub_blocks(valid, build, None)

    @pl.when((s >= nf) & (valid > 0))
    def _():
        def write(base, m):
            rows = pl.ds(base, m)
            a = jnp.concatenate([a_scr[f, rows, :] for f in range(a_scr.shape[0])], axis=1)
            y_ref[rows, :] = _dot(a.astype(F32), wd_ref[...]) + bd_ref[...]

        def clear(base, m):
            y_ref[pl.ds(base, m), :] = jnp.zeros((m, y_ref.shape[1]), F32)

        _for_sub_blocks(valid, write, clear)


def _for_sub_blocks(valid, work, rest):
    for sbk in range(MOE_BLOCK // MOE_SUB):
        base = sbk * MOE_SUB
        left = valid - base
        for m in range(MOE_TAIL, MOE_SUB + 1, MOE_TAIL):
            @pl.when((left > m - MOE_TAIL) & ((left <= m) if m < MOE_SUB else (left > m - MOE_TAIL)))
            def _(m=m):
                work(base, m)
                if rest is not None and m < MOE_SUB:
                    rest(base + m, MOE_SUB - m)

        if rest is not None:
            @pl.when(left <= 0)
            def _():
                rest(base, MOE_SUB)


def _moe_experts(xs, blk_e, nused, nvalid, w_gu, b_gu, w_dn, b_dn):
    P, D = xs.shape
    F = w_dn.shape[2]
    NB = P // MOE_BLOCK
    NF = F // MOE_FT
    ND = D // MOE_FT

    def eff(i, nu):
        return jnp.minimum(i, nu[0] - 1)

    def fa(i, s, nu):
        return jnp.where(i < nu[0], jnp.minimum(s, NF - 1), NF - 1)

    def fb(i, s, nu):
        return jnp.where(i < nu[0], jnp.maximum(s - NF, 0), ND - 1)

    grid_spec = pltpu.PrefetchScalarGridSpec(
        num_scalar_prefetch=3,
        grid=(NB, NF + ND),
        in_specs=[
            pl.BlockSpec((MOE_BLOCK, D), lambda i, s, be, nu, nv: (eff(i, nu), 0)),
            pl.BlockSpec((None, None, D, MOE_FT), lambda i, s, be, nu, nv: (0, be[eff(i, nu)], 0, fa(i, s, nu))),
            pl.BlockSpec((None, None, D, MOE_FT),
                         lambda i, s, be, nu, nv: (0, be[eff(i, nu)], 0, NF + fa(i, s, nu))),
            pl.BlockSpec((None, None, F, MOE_FT), lambda i, s, be, nu, nv: (0, be[eff(i, nu)], 0, fb(i, s, nu))),
            pl.BlockSpec((None, 1, MOE_FT), lambda i, s, be, nu, nv: (be[eff(i, nu)], 0, fa(i, s, nu))),
            pl.BlockSpec((None, 1, MOE_FT), lambda i, s, be, nu, nv: (be[eff(i, nu)], 0, NF + fa(i, s, nu))),
            pl.BlockSpec((None, 1, MOE_FT), lambda i, s, be, nu, nv: (be[eff(i, nu)], 0, fb(i, s, nu))),
        ],
        out_specs=pl.BlockSpec((MOE_BLOCK, MOE_FT), lambda i, s, be, nu, nv: (eff(i, nu), fb(i, s, nu))),
        scratch_shapes=[pltpu.VMEM((NF, MOE_BLOCK, MOE_FT), BF16)],
    )
    assert NF == ND
    return pl.pallas_call(
        _moe_kernel,
        grid_spec=grid_spec,
        out_shape=jax.ShapeDtypeStruct((P, D), F32),
        compiler_params=_cp(("arbitrary", "arbitrary"), MOE_VMEM_LIMIT),
        name="moe_experts",
    )(blk_e, nused, nvalid, xs, w_gu, w_gu, w_dn, b_gu, b_gu, b_dn)


def _combine_kernel(idx_ref, idxn_ref, y_hbm, x1_ref, g2_ref, rw_ref, lg_ref, lb_ref, o_ref, buf, sem):
    i = pl.program_id(0)
    nt = pl.num_programs(0)
    slot = i % 2
    tm = x1_ref.shape[0]

    def issue_tile(idx_r, sl):
        def issue(r, _):
            for k in range(TOP_K):
                d = idx_r[0, 0, k * tm + r]
                pltpu.make_async_copy(y_hbm.at[pl.ds(d, 1), :], buf.at[sl, k, pl.ds(r, 1), :], sem.at[sl]).start()
            return 0

        lax.fori_loop(0, tm, issue, 0, unroll=2)

    @pl.when(i == 0)
    def _():
        issue_tile(idx_ref, 0)

    @pl.when(i + 1 < nt)
    def _():
        issue_tile(idxn_ref, 1 - slot)

    for k in range(TOP_K):
        pltpu.make_async_copy(y_hbm.at[pl.ds(0, tm), :], buf.at[slot, k], sem.at[slot]).wait()
    rw = rw_ref[...]
    moe = rw[:, 0:1] * buf[slot, 0]
    for k in range(1, TOP_K):
        moe = moe + rw[:, k:k + 1] * buf[slot, k]
    o_ref[...] = _layernorm(DN_ALPHA * x1_ref[...] + g2_ref[...] * moe, lg_ref[...], lb_ref[...])


def _moe_combine(y, dest, x1, g2, rw, ln_g, ln_b, *, seq_len):
    T, D = x1.shape
    tm = 128
    nt = T // tm
    idx = dest.reshape(nt, tm, TOP_K).transpose(0, 2, 1).reshape(nt, 1, TOP_K * tm)
    if seq_len > 0:
        seq_tiles = seq_len // tm
        mod_spec = pl.BlockSpec((None, 1, D), lambda i: (i // seq_tiles, 0, 0))
    else:
        mod_spec = pl.BlockSpec((tm, D), lambda i: (i, 0))
    row = pl.BlockSpec((tm, D), lambda i: (i, 0))
    vec = pl.BlockSpec((1, D), lambda i: (0, 0))
    return pl.pallas_call(
        _combine_kernel,
        grid=(nt,),
        in_specs=[pl.BlockSpec((1, 1, TOP_K * tm), lambda i: (i, 0, 0), memory_space=pltpu.SMEM),
                  pl.BlockSpec((1, 1, TOP_K * tm), lambda i: (jnp.minimum(i + 1, nt - 1), 0, 0),
                               memory_space=pltpu.SMEM),
                  pl.BlockSpec(memory_space=pl.ANY),
                  row, mod_spec, pl.BlockSpec((tm, LANES), lambda i: (i, 0)), vec, vec],
        out_specs=row,
        out_shape=jax.ShapeDtypeStruct((T, D), F32),
        scratch_shapes=[pltpu.VMEM((2, TOP_K, tm, D), F32), pltpu.SemaphoreType.DMA((2,))],
        compiler_params=pltpu.CompilerParams(dimension_semantics=("arbitrary",), vmem_limit_bytes=VMEM_LIMIT,
                                             disable_bounds_checks=True),
        name="moe_combine",
    )(idx, idx, y, x1, g2, rw, ln_g, ln_b)


def _lookup(table, idx):
    hit = idx[..., None] == jnp.arange(N_EXPERTS, dtype=jnp.int32)
    return jnp.sum(jnp.where(hit, table, 0), axis=-1)


def _route(experts, ranks, counts):
    T = experts.shape[0]
    A = T * TOP_K
    flat_e = experts.reshape(A)
    rank = ranks.reshape(A)
    nblk = (counts + MOE_BLOCK - 1) // MOE_BLOCK
    blk_end = jnp.cumsum(nblk)
    blk_start = blk_end - nblk
    dest = _lookup(blk_start, flat_e) * MOE_BLOCK + rank
    NB = -(-A // MOE_BLOCK) + N_EXPERTS
    idx = jnp.zeros((NB * MOE_BLOCK,), jnp.int32).at[dest].set(
        jnp.arange(A, dtype=jnp.int32) // TOP_K, unique_indices=True, mode='promise_in_bounds')
    blk_e = jnp.minimum(jnp.searchsorted(blk_end, jnp.arange(NB, dtype=jnp.int32), side='right'),
                        N_EXPERTS - 1).astype(jnp.int32)
    nused = blk_end[-1:].astype(jnp.int32)
    blk = jnp.arange(NB, dtype=jnp.int32)
    nvalid = jnp.clip(counts[blk_e] - (blk - blk_start[blk_e]) * MOE_BLOCK, 0, MOE_BLOCK)
    nvalid = jnp.where(blk < nused[0], nvalid, 0).astype(jnp.int32)
    per = MOE_BLOCK // MOE_SUB
    gblk = jnp.arange(NB * per, dtype=jnp.int32)
    gcnt = jnp.clip(jnp.repeat(nvalid, per) - (gblk % per) * MOE_SUB, 0, MOE_SUB).astype(jnp.int32)
    order = jnp.argsort(gcnt == 0, stable=True).astype(jnp.int32)
    nnz = jnp.sum(gcnt > 0)
    gvisit = jnp.where(gblk < nnz, order, order[jnp.maximum(nnz - 1, 0)])
    gcnt = jnp.where(gblk < nnz, gcnt[order], 0)
    return dest.reshape(T, TOP_K), idx, blk_e, nused, nvalid, gcnt, gvisit


def _block_diag_in(bb):
    gpc = S5_GROUPS // S5_CHUNKS
    x = bb.reshape(S5_CHUNKS, gpc, S5_GROUP, S5_STATE)
    eye = jnp.eye(gpc, dtype=bb.dtype)
    return jnp.einsum('agcn,gh->agchn', x, eye).reshape(S5_CHUNKS, S5_CHUNK_IN, S5_CHUNK_STATE)


def _block_diag_out(cc):
    gpc = S5_GROUPS // S5_CHUNKS
    x = cc.reshape(S5_CHUNKS, gpc, S5_GROUP, S5_STATE)
    eye = jnp.eye(gpc, dtype=cc.dtype)
    return jnp.einsum('agcn,gh->ahngc', x, eye).reshape(S5_CHUNKS, S5_CHUNK_STATE, S5_CHUNK_IN)


def kernel(x_prompt, x_sample, c_prompt, c_sample, cache_k, cache_v, cache_logf, state_s5_re, state_s5_im, page_table, w_ada, b_ada, w_in, fox_b_f, s5_a_re, s5_a_im, s5_log_dt, s5_b_re, s5_b_im, s5_c_re, s5_c_im, s5_d, w_glu, b_glu, p_a, p_b, w_o, ln1_g, ln1_b, w_router, b_router, w_gu, b_gu, w_dn, b_dn, ln2_g, ln2_b):
    B, L, D = x_prompt.shape
    Bd, Ld, _ = x_sample.shape
    Tp, Ts = B * L, Bd * Ld
    W = HEADS * HEAD_DIM
    n_phys = cache_k.shape[1]

    n_c = B + Bd
    c_all = jnp.concatenate([c_prompt, c_sample, jnp.zeros((-n_c % 8, D), F32)], axis=0)
    mod = _adaln(c_all, w_ada[0], b_ada)
    mp = [mod[:B, i * D:(i + 1) * D].reshape(B, 1, D) for i in range(6)]
    ms = [jnp.repeat(mod[B:n_c, i * D:(i + 1) * D], Ld, axis=0) for i in range(6)]

    wit = w_in[0].T
    f_lo = 3 * W
    wmt = jnp.concatenate([wit[:f_lo], wit[f_lo + HEADS:]], axis=0).astype(BF16)
    wft = jnp.pad(wit[f_lo:f_lo + HEADS], ((0, LANES - HEADS), (0, 0))).astype(BF16)
    bf = jnp.pad(fox_b_f, ((0, 0), (0, LANES - HEADS)))
    hp, logf_p, cT_p = _in_proj(x_prompt.reshape(Tp, D), mp[0], mp[1], wmt, wft.T, wft, bf, bf.T,
                                seq_len=L, seg=0)
    hs, logf_s, c_s = _in_proj(x_sample.reshape(Ts, D), ms[0], ms[1], wmt, wft.T, wft, bf, bf.T,
                               seq_len=0, seg=Ld)

    lam_r, lam_i, bb_r, bb_i = _s5_params(s5_a_re[0], s5_a_im[0], s5_log_dt[0][:, None],
                                          s5_b_re[0].transpose(0, 2, 1), s5_b_im[0].transpose(0, 2, 1))
    lam_r = lam_r.reshape(S5_CHUNKS, S5_CHUNK_STATE)
    lam_i = lam_i.reshape(S5_CHUNKS, S5_CHUNK_STATE)
    wb = jnp.concatenate([_block_diag_in(bb_r), _block_diag_in(bb_i)], axis=2).astype(BF16)
    wcr = _block_diag_out(s5_c_re[0]).astype(BF16)
    wci = _block_diag_out(s5_c_im[0]).astype(BF16)
    wgl = w_glu[0].astype(BF16)
    s5_args = (lam_r, lam_i, wb, wcr, wci, s5_d, wgl, b_glu)
    zeros_state = jnp.zeros((B, S5_CHUNKS, S5_CHUNK_STATE), F32)
    s5_t = 256
    ssm_p, sr_p, si_p = _s5(hp, 3, zeros_state, zeros_state, *s5_args, nseq=1, slen=s5_t, chunks=L // s5_t)
    ssm_s, sr_s, si_s = _s5(hs, 3, state_s5_re[0].reshape(Bd, S5_CHUNKS, S5_CHUNK_STATE),
                            state_s5_im[0].reshape(Bd, S5_CHUNKS, S5_CHUNK_STATE), *s5_args,
                            nseq=32, slen=Ld, chunks=1)

    att_p = _fox_prompt(hp, cT_p, B, L)
    scale = HEAD_DIM ** -0.5
    q_s = hs[:, :W].reshape(Bd, Ld, HEADS, HEAD_DIM) * scale
    k_s = hs[:, W:2 * W].reshape(Bd, Ld, HEADS, HEAD_DIM)
    v_s = hs[:, 2 * W:3 * W].reshape(Bd, Ld, HEADS, HEAD_DIM)
    qe = q_s.reshape(Bd, Ld * HEADS, HEAD_DIM)
    kn = jnp.broadcast_to(k_s[:, :, None], (Bd, Ld, Ld, HEADS, HEAD_DIM)).reshape(Bd, Ld, Ld * HEADS, HEAD_DIM)
    vn = jnp.broadcast_to(v_s[:, :, None], (Bd, Ld, Ld, HEADS, HEAD_DIM)).reshape(Bd, Ld, Ld * HEADS, HEAD_DIM)
    cn = c_s[:, :HEADS].reshape(Bd, Ld, HEADS)
    bn = jnp.broadcast_to(cn.transpose(0, 2, 1)[:, None], (Bd, Ld, HEADS, Ld)).reshape(Bd, Ld * HEADS, Ld)
    bn = jnp.pad(bn, ((0, 0), (0, 0), (0, LANES - Ld)))
    ckt = jnp.transpose(cache_k[0], (0, 2, 3, 1)).reshape(n_phys, W, PAGE)
    cvt = jnp.transpose(cache_v[0], (0, 2, 3, 1)).reshape(n_phys, W, PAGE)
    clft = jnp.transpose(cache_logf[0], (0, 2, 1))
    att_s = _fox_sample(page_table, ckt, cvt, clft, qe, kn, vn, bn)
    att_s = att_s.reshape(Ts, W).astype(BF16)

    pab, pbb, wob = p_a[0].astype(BF16), p_b[0].astype(BF16), w_o[0].astype(BF16)
    wr = jnp.pad(w_router[0], ((0, 0), (0, LANES - N_EXPERTS)))
    br = jnp.pad(b_router, ((0, 0), (0, LANES - N_EXPERTS)), constant_values=NEG)
    m_p = _mix_gate(att_p, ssm_p, hp, pab, pbb)
    m_s = _mix_gate(att_s, ssm_s, hs, pab, pbb)
    x1_p, u2_p, ri_p, rw_p, cnt_p = _mix_out(m_p, x_prompt.reshape(Tp, D), mp[2], mp[3], mp[4], wob, ln1_g, ln1_b,
                                             wr, br, seq_len=L)
    x1_s, u2_s, ri_s, rw_s, cnt_s = _mix_out(m_s, x_sample.reshape(Ts, D), ms[2], ms[3], ms[4], wob, ln1_g, ln1_b,
                                             wr, br, seq_len=0)

    u2 = jnp.concatenate([u2_p, u2_s], axis=0)
    cnt_p = cnt_p[0, :N_EXPERTS].astype(jnp.int32)
    cnt_s = cnt_s[0, :N_EXPERTS].astype(jnp.int32)
    ex_p, ex_s = ri_p[:, :TOP_K], ri_s[:, :TOP_K]
    experts = jnp.concatenate([ex_p, ex_s], axis=0)
    ranks = jnp.concatenate([ri_p[:, TOP_K:2 * TOP_K], ri_s[:, TOP_K:2 * TOP_K] + _lookup(cnt_p, ex_s)], axis=0)
    dest, idx, blk_e, nused, nvalid, gcnt, gvisit = _route(experts, ranks, cnt_p + cnt_s)
    xs = _moe_gather(u2, idx, gcnt, gvisit)
    y = _moe_experts(xs, blk_e, nused, nvalid, w_gu, b_gu[0][:, None, :], w_dn, b_dn[0][:, None, :])
    x2_p = _moe_combine(y, dest[:Tp], x1_p, mp[5], rw_p, ln2_g, ln2_b, seq_len=L)
    x2_s = _moe_combine(y, dest[Tp:], x1_s, ms[5], rw_s, ln2_g, ln2_b, seq_len=0)

    def kv_out(t, n, l):
        return t.reshape(n, HEADS, HEAD_DIM, l).transpose(0, 3, 1, 2)[None]

    def heads(t, n, l):
        return t.reshape(1, n, l, HEADS, HEAD_DIM)

    def state(s, n):
        return s.reshape(1, n, S5_GROUPS, S5_STATE)

    return (x2_p.reshape(B, L, D), x2_s.reshape(Bd, Ld, D),
            kv_out(_heads_minor_time(hp, 1, B, L), B, L), kv_out(_heads_minor_time(hp, 2, B, L), B, L),
            logf_p[:, :HEADS].reshape(1, B, L, HEADS), state(sr_p, B), state(si_p, B),
            heads(hs[:, W:2 * W], Bd, Ld), heads(hs[:, 2 * W:3 * W], Bd, Ld),
            logf_s[:, :HEADS].reshape(1, Bd, Ld, HEADS), state(sr_s, Bd), state(si_s, Bd))
```

```python
import functools
import math

import jax
import jax.numpy as jnp
from jax import lax
from jax.experimental import pallas as pl
from jax.experimental.pallas import tpu as pltpu

F32 = jnp.float32
BF16 = jnp.bfloat16

HEADS = 16
HEAD_DIM = 64
PAGE = 128
S5_GROUPS = 64
S5_GROUP = 16
S5_STATE = 64
N_EXPERTS = 32
TOP_K = 4
SWIGLU_LIMIT = 7.0
SWIGLU_ALPHA = 1.702
DN_ALPHA = 2.0 ** 0.25
LN_EPS = 1e-5
NEG = -1e30
LOG2E = math.log2(math.e)

LANES = 128
S5_CHUNKS = 8
S5_CHUNK_STATE = S5_GROUPS * S5_STATE // S5_CHUNKS
S5_CHUNK_IN = S5_GROUPS * S5_GROUP // S5_CHUNKS
S5_LANE_BLOCKS = S5_CHUNK_STATE // LANES
FOX_TQ = 1024
FOX_TK = 1024
MOE_BLOCK = 1536
MOE_SUB = 384
MOE_TAIL = 128
MOE_FT = 512
VMEM_LIMIT = 56 * 1024 * 1024
MOE_VMEM_LIMIT = 60 * 1024 * 1024


def _cp(sem, vmem=VMEM_LIMIT):
    return pltpu.CompilerParams(dimension_semantics=sem, vmem_limit_bytes=vmem)


def _dot(a, b):
    return jnp.dot(a, b, preferred_element_type=F32)


def _dot_nt(a, b):
    return lax.dot_general(a, b, (((1,), (1,)), ((), ())), preferred_element_type=F32)


def _log_sigmoid(x):
    return jnp.minimum(x, 0.0) - jnp.log1p(jnp.exp(-jnp.abs(x)))


def _split3(v):
    hi = v.astype(BF16)
    r = v - hi.astype(F32)
    mid = r.astype(BF16)
    lo = (r - mid.astype(F32)).astype(BF16)
    return hi, mid, lo


def _layernorm(v, g, b):
    mu = jnp.mean(v, axis=-1, keepdims=True)
    d = v - mu
    var = jnp.mean(d * d, axis=-1, keepdims=True)
    return d * lax.rsqrt(var + LN_EPS) * g + b


def _ada_kernel(c_ref, w_ref, b_ref, o_ref):
    c = c_ref[...]
    a = (c * jax.nn.sigmoid(c)).astype(BF16)
    o_ref[...] = _dot(a, w_ref[...].astype(BF16)) + b_ref[...]


def _adaln(c_all, w_ada, b_ada):
    R, D = c_all.shape
    N = w_ada.shape[1]
    tn = 1024
    return pl.pallas_call(
        _ada_kernel,
        grid=(N // tn,),
        in_specs=[pl.BlockSpec((R, D), lambda j: (0, 0)),
                  pl.BlockSpec((D, tn), lambda j: (0, j)),
                  pl.BlockSpec((1, tn), lambda j: (0, j))],
        out_specs=pl.BlockSpec((R, tn), lambda j: (0, j)),
        out_shape=jax.ShapeDtypeStruct((R, N), F32),
        compiler_params=_cp(("arbitrary",)),
        name="adaln",
    )(c_all, w_ada, b_ada)


def _in_kernel(x_ref, sh_ref, sc_ref, wm_ref, wf_ref, wft_ref, bf_ref, bft_ref,
               h_ref, logf_ref, cum_ref, u_scr, carry_scr, *, seq_tiles, seg):
    i = pl.program_id(0)
    j = pl.program_id(1)
    tm = x_ref.shape[0]

    @pl.when(j == 0)
    def _():
        u = x_ref[...] * (1.0 + sc_ref[...]) + sh_ref[...]
        ub = u.astype(BF16)
        u_scr[...] = ub
        lf = _log_sigmoid(_dot(ub, wf_ref[...]) + bf_ref[...])
        logf_ref[...] = lf
        r = lax.broadcasted_iota(jnp.int32, (tm, tm), 0)
        c = lax.broadcasted_iota(jnp.int32, (tm, tm), 1)
        if seq_tiles > 0:
            lft = _log_sigmoid(_dot_nt(wft_ref[...], ub) + bft_ref[...])
            tri = (r <= c).astype(BF16)
            hi, mid, lo = _split3(lft)
            cs = _dot(hi, tri) + _dot(mid, tri) + _dot(lo, tri)

            @pl.when(i % seq_tiles == 0)
            def _():
                carry_scr[...] = jnp.zeros_like(carry_scr)

            cs = cs + carry_scr[...]
            cum_ref[...] = cs
            carry_scr[...] = cs[:, tm - 1:tm]
        else:
            tri = ((c <= r) & (c // seg == r // seg)).astype(BF16)
            hi, mid, lo = _split3(lf)
            cum_ref[...] = _dot(tri, hi) + _dot(tri, mid) + _dot(tri, lo)

    h_ref[...] = _dot_nt(u_scr[...], wm_ref[...])


def _in_proj(x, shift, scale, wmt, wf, wft, bf, bft, *, seq_len, seg):
    T, D = x.shape
    N = wmt.shape[0]
    tm, tn = min(1024, T), 1024
    nt = T // tm
    if seq_len > 0:
        seq_tiles = seq_len // tm
        mod_spec = pl.BlockSpec((None, 1, D), lambda i, j: (i // seq_tiles, 0, 0))
        cum_shape, cum_spec = (LANES, T), pl.BlockSpec((LANES, tm), lambda i, j: (0, i))
    else:
        seq_tiles = 0
        mod_spec = pl.BlockSpec((tm, D), lambda i, j: (i, 0))
        cum_shape, cum_spec = (T, LANES), pl.BlockSpec((tm, LANES), lambda i, j: (i, 0))
    kern = functools.partial(_in_kernel, seq_tiles=seq_tiles, seg=seg)
    return pl.pallas_call(
        kern,
        grid=(nt, N // tn),
        in_specs=[pl.BlockSpec((tm, D), lambda i, j: (i, 0)), mod_spec, mod_spec,
                  pl.BlockSpec((tn, D), lambda i, j: (j, 0)),
                  pl.BlockSpec((D, LANES), lambda i, j: (0, 0)),
                  pl.BlockSpec((LANES, D), lambda i, j: (0, 0)),
                  pl.BlockSpec((1, LANES), lambda i, j: (0, 0)),
                  pl.BlockSpec((LANES, 1), lambda i, j: (0, 0))],
        out_specs=[pl.BlockSpec((tm, tn), lambda i, j: (i, j)),
                   pl.BlockSpec((tm, LANES), lambda i, j: (i, 0)),
                   cum_spec],
        out_shape=[jax.ShapeDtypeStruct((T, N), F32),
                   jax.ShapeDtypeStruct((T, LANES), F32),
                   jax.ShapeDtypeStruct(cum_shape, F32)],
        scratch_shapes=[pltpu.VMEM((tm, D), BF16), pltpu.VMEM((LANES, 1), F32)],
        compiler_params=_cp(("arbitrary", "arbitrary")),
        name="in_proj",
    )(x, shift, scale, wmt, wf, wft, bf, bft)


def _transpose_kernel(x_ref, o_ref):
    o_ref[...] = x_ref[...].T


def _heads_minor_time(h, col_block, B, L):
    W = HEADS * HEAD_DIM
    tm = min(512, L)
    nt = L // tm
    return pl.pallas_call(
        _transpose_kernel,
        grid=(B, nt),
        in_specs=[pl.BlockSpec((tm, W), lambda b, i: (b * nt + i, col_block))],
        out_specs=pl.BlockSpec((None, W, tm), lambda b, i: (b, 0, i)),
        out_shape=jax.ShapeDtypeStruct((B, W, L), F32),
        compiler_params=_cp(("parallel", "parallel")),
        name="heads_minor_time",
    )(h)


def _fox_prompt_kernel(qt_ref, kt_ref, q_ref, k_ref, v_ref, c_ref, o_ref, qs, m_scr, l_scr, acc, *, ratio):
    hp = pl.program_id(1)
    t = pl.program_id(2)
    qi = qt_ref[t]
    kv = kt_ref[t]
    tq, tk = q_ref.shape[0], k_ref.shape[0]
    lane = lax.broadcasted_iota(jnp.int32, (tq, LANES), 1)

    @pl.when(kv == 0)
    def _():
        q2 = q_ref[...] * (HEAD_DIM ** -0.5 * LOG2E)
        qs[0] = jnp.where(lane < HEAD_DIM, q2, 0.0).astype(BF16)
        qs[1] = jnp.where(lane >= HEAD_DIM, q2, 0.0).astype(BF16)
        m_scr[...] = jnp.full_like(m_scr, NEG)
        l_scr[...] = jnp.zeros_like(l_scr)
        acc[...] = jnp.zeros_like(acc)

    def block(diagonal):
        kb = k_ref[...].astype(BF16)
        vb = v_ref[...].astype(BF16)
        r0 = (2 * hp) % 8
        for a in range(2):
            crow = c_ref[pl.ds(r0 + a, 1), :] * LOG2E
            s = _dot_nt(qs[a], kb) - crow
            if diagonal:
                row = lax.broadcasted_iota(jnp.int32, (tq, tk), 0)
                col = lax.broadcasted_iota(jnp.int32, (tq, tk), 1) + (kv * tk - qi * tq)
                s = jnp.where(col <= row, s, NEG)
            cols = [s[:, c * LANES:(c + 1) * LANES] for c in range(tk // LANES)]
            mx = cols[0]
            for sc in cols[1:]:
                mx = jnp.maximum(mx, sc)
            m_prev = m_scr[a]
            m_new = jnp.maximum(m_prev, jnp.max(mx, axis=1, keepdims=True))
            alpha = jnp.exp2(m_prev - m_new)
            ps = [jnp.exp2(sc - m_new) for sc in cols]
            lsum = ps[0]
            for pc in ps[1:]:
                lsum = lsum + pc
            l_scr[a] = alpha * l_scr[a] + lsum
            p = jnp.concatenate(ps, axis=1).astype(BF16)
            acc[a] = alpha * acc[a] + _dot(p, vb)
            m_scr[a] = m_new

    @pl.when(kv < qi * ratio)
    def _():
        block(False)

    @pl.when(kv >= qi * ratio)
    def _():
        block(True)

    @pl.when(kv == (qi + 1) * ratio - 1)
    def _():
        l0 = jnp.sum(l_scr[0], axis=1, keepdims=True)
        l1 = jnp.sum(l_scr[1], axis=1, keepdims=True)
        o = jnp.where(lane < HEAD_DIM, acc[0] / l0, acc[1] / l1)
        o_ref[...] = o.astype(o_ref.dtype)


def _fox_prompt(h, cT, B, L):
    tq, tk = FOX_TQ, FOX_TK
    ratio = tq // tk
    nq, nk = L // tq, L // tk
    npairs = HEADS // 2
    steps = [(qi, kv) for qi in range(nq) for kv in range((qi + 1) * ratio)]
    qt = jnp.asarray([p[0] for p in steps], jnp.int32)
    kt = jnp.asarray([p[1] for p in steps], jnp.int32)
    grid_spec = pltpu.PrefetchScalarGridSpec(
        num_scalar_prefetch=2,
        grid=(B, npairs, len(steps)),
        in_specs=[
            pl.BlockSpec((tq, LANES), lambda b, hp, t, qt, kt: (b * nq + qt[t], hp)),
            pl.BlockSpec((tk, LANES), lambda b, hp, t, qt, kt: (b * nk + kt[t], npairs + hp)),
            pl.BlockSpec((tk, LANES), lambda b, hp, t, qt, kt: (b * nk + kt[t], 2 * npairs + hp)),
            pl.BlockSpec((8, tk), lambda b, hp, t, qt, kt: (hp // 4, b * nk + kt[t])),
        ],
        out_specs=pl.BlockSpec((tq, LANES), lambda b, hp, t, qt, kt: (b * nq + qt[t], hp)),
        scratch_shapes=[pltpu.VMEM((2, tq, LANES), BF16), pltpu.VMEM((2, tq, LANES), F32),
                        pltpu.VMEM((2, tq, LANES), F32), pltpu.VMEM((2, tq, LANES), F32)],
    )
    return pl.pallas_call(
        functools.partial(_fox_prompt_kernel, ratio=ratio),
        grid_spec=grid_spec,
        out_shape=jax.ShapeDtypeStruct((B * L, HEADS * HEAD_DIM), BF16),
        compiler_params=_cp(("parallel", "parallel", "arbitrary")),
        name="fox_prompt",
    )(qt, kt, h, h, h, cT)


def _fox_sample_kernel(pt_ref, *refs, n_pages):
    k_refs = refs[:n_pages]
    v_refs = refs[n_pages:2 * n_pages]
    lf_refs = refs[2 * n_pages:3 * n_pages]
    qe_ref, kn_ref, vn_ref, bn_ref, o_ref = refs[3 * n_pages:]
    nq = qe_ref.shape[0] // HEADS
    nrow = nq * HEADS
    width = HEADS * HEAD_DIM

    xs = jnp.concatenate([r[...] for r in lf_refs], axis=0)
    hi, mid, lo = _split3(xs)
    after = (lax.broadcasted_iota(jnp.int32, (PAGE, PAGE), 0)
             > lax.broadcasted_iota(jnp.int32, (PAGE, PAGE), 1)).astype(BF16)
    within = _dot(hi, after) + _dot(mid, after) + _dot(lo, after)
    n = n_pages * HEADS
    pr = lax.broadcasted_iota(jnp.int32, (n, n), 0)
    pc = lax.broadcasted_iota(jnp.int32, (n, n), 1)
    later = ((pc // HEADS > pr // HEADS) & (pc % HEADS == pr % HEADS)).astype(BF16)
    beyond = jnp.sum(_dot(later, hi) + _dot(later, mid) + _dot(later, lo), axis=1, keepdims=True)
    bias_rows = within + beyond
    bias = jnp.concatenate([bias_rows[j * HEADS:(j + 1) * HEADS] for j in range(n_pages)], axis=1)
    bias = jnp.concatenate([bias] * nq, axis=0)

    qe = qe_ref[...]
    rh = lax.broadcasted_iota(jnp.int32, (nrow, width), 0) % HEADS
    ch = lax.broadcasted_iota(jnp.int32, (nrow, width), 1) // HEAD_DIM
    tile = (lax.broadcasted_iota(jnp.int32, (HEAD_DIM, width), 0)
            == lax.broadcasted_iota(jnp.int32, (HEAD_DIM, width), 1) % HEAD_DIM).astype(BF16)
    qbd = jnp.where(rh == ch, _dot(qe.astype(BF16), tile), 0.0).astype(BF16)

    kt = jnp.concatenate([r[...].astype(BF16) for r in k_refs], axis=1)
    s = _dot(qbd, kt) + bias

    q_of_row = lax.broadcasted_iota(jnp.int32, (nrow, 1), 0) // HEADS
    bn = bn_ref[...]
    sn = []
    for j in range(nq):
        sj = jnp.sum(qe * kn_ref[j], axis=1, keepdims=True) - bn[:, j:j + 1]
        sn.append(jnp.where(q_of_row >= j, sj, NEG))
    m = jnp.max(s, axis=1, keepdims=True)
    for sj in sn:
        m = jnp.maximum(m, sj)
    p = jnp.exp(s - m)
    l = jnp.sum(p, axis=1, keepdims=True)

    vt = jnp.concatenate([r[...].astype(BF16) for r in v_refs], axis=1)
    o_all = _dot_nt(p.astype(BF16), vt)
    o_hi, o_mid, o_lo = _split3(jnp.where(rh == ch, o_all, 0.0))
    fold = (lax.broadcasted_iota(jnp.int32, (width, HEAD_DIM), 0) % HEAD_DIM
            == lax.broadcasted_iota(jnp.int32, (width, HEAD_DIM), 1)).astype(BF16)
    o = _dot(o_hi, fold) + _dot(o_mid, fold) + _dot(o_lo, fold)
    for j in range(nq):
        pj = jnp.exp(sn[j] - m)
        l = l + pj
        o = o + pj * vn_ref[j]
    o_ref[...] = o / l


def _fox_sample(page_table, ckt, cvt, clft, qe, kn, vn, bn):
    Bd, n_pages = page_table.shape
    nrow = qe.shape[1]
    nq = nrow // HEADS
    width = HEADS * HEAD_DIM

    def page_map(j):
        return lambda b, pt: (pt[b * n_pages + j], 0, 0)

    per_row3 = lambda b, pt: (b, 0, 0)
    per_row4 = lambda b, pt: (b, 0, 0, 0)
    grid_spec = pltpu.PrefetchScalarGridSpec(
        num_scalar_prefetch=1,
        grid=(Bd,),
        in_specs=([pl.BlockSpec((None, width, PAGE), page_map(j)) for j in range(n_pages)]
                  + [pl.BlockSpec((None, width, PAGE), page_map(j)) for j in range(n_pages)]
                  + [pl.BlockSpec((None, HEADS, PAGE), page_map(j)) for j in range(n_pages)]
                  + [pl.BlockSpec((None, nrow, HEAD_DIM), per_row3),
                     pl.BlockSpec((None, nq, nrow, HEAD_DIM), per_row4),
                     pl.BlockSpec((None, nq, nrow, HEAD_DIM), per_row4),
                     pl.BlockSpec((None, nrow, LANES), per_row3)]),
        out_specs=pl.BlockSpec((None, nrow, HEAD_DIM), per_row3),
    )
    return pl.pallas_call(
        functools.partial(_fox_sample_kernel, n_pages=n_pages),
        grid_spec=grid_spec,
        out_shape=jax.ShapeDtypeStruct((Bd, nrow, HEAD_DIM), F32),
        compiler_params=_cp(("parallel",)),
        name="fox_sample",
    )(page_table.reshape(-1), *([ckt] * n_pages), *([cvt] * n_pages), *([clft] * n_pages),
      qe, kn, vn, bn)


def _s5_param_kernel(ar_ref, ai_ref, ldt_ref, br_ref, bi_ref, lr_ref, li_ref, bbr_ref, bbi_ref):
    ar, ai = ar_ref[...], ai_ref[...]
    dt = jnp.exp(ldt_ref[...])
    mag = jnp.exp(ar * dt)
    lr = mag * jnp.cos(ai * dt)
    li = mag * jnp.sin(ai * dt)
    den = ar * ar + ai * ai
    zr = ((lr - 1.0) * ar + li * ai) / den
    zi = (li * ar - (lr - 1.0) * ai) / den
    lr_ref[...] = lr
    li_ref[...] = li
    br, bi = br_ref[...], bi_ref[...]
    zr3, zi3 = zr[:, None, :], zi[:, None, :]
    bbr_ref[...] = zr3 * br - zi3 * bi
    bbi_ref[...] = zr3 * bi + zi3 * br


def _s5_params(a_re, a_im, log_dt, bt_re, bt_im):
    G, N = a_re.shape
    C = bt_re.shape[1]
    return pl.pallas_call(
        _s5_param_kernel,
        out_shape=[jax.ShapeDtypeStruct((G, N), F32), jax.ShapeDtypeStruct((G, N), F32),
                   jax.ShapeDtypeStruct((G, C, N), F32), jax.ShapeDtypeStruct((G, C, N), F32)],
        name="s5_params",
    )(a_re, a_im, log_dt, bt_re, bt_im)


def _s5_kernel(u_ref, h0r_ref, h0i_ref, lr_ref, li_ref, wb_ref, wcr_ref, wci_ref, d_ref, wg_ref, bg_ref,
               o_ref, sr_ref, si_ref, bur, bui, y_scr, cr, ci, *, nseq, slen):
    c_id = pl.program_id(1)
    R = u_ref.shape[0]
    u = u_ref[...]
    ub = u.astype(BF16)
    rs = [_dot(ub[:, gc * S5_CHUNK_IN:(gc + 1) * S5_CHUNK_IN], wb_ref[gc]) for gc in range(S5_CHUNKS)]
    for lb in range(S5_LANE_BLOCKS):
        lo = lb * LANES
        xr = jnp.stack([r[:, lo:lo + LANES] for r in rs], axis=0)
        xi = jnp.stack([r[:, S5_CHUNK_STATE + lo:S5_CHUNK_STATE + lo + LANES] for r in rs], axis=0)
        bur[lb] = jnp.swapaxes(xr, 0, 1)
        bui[lb] = jnp.swapaxes(xi, 0, 1)

    @pl.when(c_id == 0)
    def _():
        cr[...] = h0r_ref[...]
        ci[...] = h0i_ref[...]

    lr = lr_ref[...]
    li = li_ref[...]

    def seq_body(q, _):
        base = q * slen

        def step(t, hc):
            hr, hi = hc
            br = jnp.concatenate([bur[lb, base + t] for lb in range(S5_LANE_BLOCKS)], axis=1)
            bi = jnp.concatenate([bui[lb, base + t] for lb in range(S5_LANE_BLOCKS)], axis=1)
            nr = lr * hr - li * hi + br
            ni = lr * hi + li * hr + bi
            for lb in range(S5_LANE_BLOCKS):
                bur[lb, base + t] = nr[:, lb * LANES:(lb + 1) * LANES]
                bui[lb, base + t] = ni[:, lb * LANES:(lb + 1) * LANES]
            return nr, ni

        hr, hi = lax.fori_loop(0, slen, step, (cr[q], ci[q]), unroll=4)
        cr[q] = hr
        ci[q] = hi
        return 0

    lax.fori_loop(0, nseq, seq_body, 0)
    sr_ref[...] = cr[...]
    si_ref[...] = ci[...]

    hr_t = [jnp.swapaxes(bur[lb], 0, 1) for lb in range(S5_LANE_BLOCKS)]
    hi_t = [jnp.swapaxes(bui[lb], 0, 1) for lb in range(S5_LANE_BLOCKS)]
    for gc in range(S5_CHUNKS):
        hrb = jnp.concatenate([x[gc] for x in hr_t], axis=1).astype(BF16)
        hib = jnp.concatenate([x[gc] for x in hi_t], axis=1).astype(BF16)
        y_scr[:, gc * S5_CHUNK_IN:(gc + 1) * S5_CHUNK_IN] = _dot(hrb, wcr_ref[gc]) - _dot(hib, wci_ref[gc])
    y = y_scr[...] + d_ref[...] * u
    z = jax.nn.gelu(y)
    o_ref[...] = (z * jax.nn.sigmoid(_dot(z.astype(BF16), wg_ref[...]) + bg_ref[...])).astype(o_ref.dtype)


def _s5(h, col_block, h0r, h0i, lam_r, lam_i, wb, wcr, wci, d_skip, w_glu, b_glu, *, nseq, slen, chunks):
    T = h.shape[0]
    W = S5_GROUPS * S5_GROUP
    R = nseq * slen
    nsb = T // (R * chunks)
    n_seq_total = h0r.shape[0]
    st_spec = pl.BlockSpec((nseq, S5_CHUNKS, S5_CHUNK_STATE), lambda sb, c: (sb, 0, 0))
    full3 = lambda sb, c: (0, 0, 0)
    full2 = lambda sb, c: (0, 0)
    kern = functools.partial(_s5_kernel, nseq=nseq, slen=slen)
    return pl.pallas_call(
        kern,
        grid=(nsb, chunks),
        in_specs=[pl.BlockSpec((R, W), lambda sb, c: (sb * chunks + c, col_block)),
                  st_spec, st_spec,
                  pl.BlockSpec((S5_CHUNKS, S5_CHUNK_STATE), full2),
                  pl.BlockSpec((S5_CHUNKS, S5_CHUNK_STATE), full2),
                  pl.BlockSpec((S5_CHUNKS, S5_CHUNK_IN, 2 * S5_CHUNK_STATE), full3),
                  pl.BlockSpec((S5_CHUNKS, S5_CHUNK_STATE, S5_CHUNK_IN), full3),
                  pl.BlockSpec((S5_CHUNKS, S5_CHUNK_STATE, S5_CHUNK_IN), full3),
                  pl.BlockSpec((1, W), full2),
                  pl.BlockSpec((W, W), full2),
                  pl.BlockSpec((1, W), full2)],
        out_specs=[pl.BlockSpec((R, W), lambda sb, c: (sb * chunks + c, 0)), st_spec, st_spec],
        out_shape=[jax.ShapeDtypeStruct((T, W), BF16),
                   jax.ShapeDtypeStruct((n_seq_total, S5_CHUNKS, S5_CHUNK_STATE), F32),
                   jax.ShapeDtypeStruct((n_seq_total, S5_CHUNKS, S5_CHUNK_STATE), F32)],
        scratch_shapes=[pltpu.VMEM((S5_LANE_BLOCKS, R, S5_CHUNKS, LANES), F32),
                        pltpu.VMEM((S5_LANE_BLOCKS, R, S5_CHUNKS, LANES), F32),
                        pltpu.VMEM((R, W), F32),
                        pltpu.VMEM((nseq, S5_CHUNKS, S5_CHUNK_STATE), F32),
                        pltpu.VMEM((nseq, S5_CHUNKS, S5_CHUNK_STATE), F32)],
        compiler_params=_cp(("arbitrary", "arbitrary")),
        name="s5",
    )(h, h0r, h0i, lam_r, lam_i, wb, wcr, wci, d_skip, w_glu, b_glu)


def _mix_gate_kernel(att_ref, ssm_ref, ga_ref, gb_ref, pa_ref, pb_ref, o_ref):
    m = (jax.nn.sigmoid(ga_ref[...]) * _dot(att_ref[...], pa_ref[...])
         + jax.nn.sigmoid(gb_ref[...]) * _dot(ssm_ref[...], pb_ref[...]))
    o_ref[...] = m.astype(o_ref.dtype)


def _mix_gate(att, ssm, h, p_a, p_b):
    T, W = att.shape
    D = p_a.shape[1]
    tm = min(256, T)
    return pl.pallas_call(
        _mix_gate_kernel,
        grid=(T // tm,),
        in_specs=[pl.BlockSpec((tm, W), lambda i: (i, 0)),
                  pl.BlockSpec((tm, W), lambda i: (i, 0)),
                  pl.BlockSpec((tm, D), lambda i: (i, 2)),
                  pl.BlockSpec((tm, D), lambda i: (i, 3)),
                  pl.BlockSpec((W, D), lambda i: (0, 0)),
                  pl.BlockSpec((W, D), lambda i: (0, 0))],
        out_specs=pl.BlockSpec((tm, D), lambda i: (i, 0)),
        out_shape=jax.ShapeDtypeStruct((T, D), BF16),
        compiler_params=_cp(("parallel",)),
        name="mix_gate",
    )(att, ssm, h, h, p_a, p_b)


def _mix_out_kernel(m_ref, x_ref, g1_ref, sh2_ref, sc2_ref, wo_ref, lg_ref, lb_ref, wr_ref, br_ref,
                    x1_ref, u2_ref, ri_ref, rw_ref, cnt_ref, cnt_scr):
    out = _dot(m_ref[...], wo_ref[...])
    x1 = _layernorm(DN_ALPHA * x_ref[...] + g1_ref[...] * out, lg_ref[...], lb_ref[...])
    x1_ref[...] = x1
    u2 = x1 * (1.0 + sc2_ref[...]) + sh2_ref[...]
    u2_ref[...] = u2
    u_hi = u2.astype(BF16)
    u_lo = (u2 - u_hi.astype(F32)).astype(BF16)
    wr = wr_ref[...]
    w_hi = wr.astype(BF16)
    w_lo = (wr - w_hi.astype(F32)).astype(BF16)
    logits = _dot(u_hi, w_hi) + _dot(u_hi, w_lo) + _dot(u_lo, w_hi) + br_ref[...]
    lane = lax.broadcasted_iota(jnp.int32, logits.shape, 1)
    cur = logits
    vals, idxs = [], []
    for _ in range(TOP_K):
        mk = jnp.max(cur, axis=1, keepdims=True)
        ik = jnp.min(jnp.where(cur == mk, lane, LANES), axis=1, keepdims=True)
        vals.append(mk)
        idxs.append(ik)
        cur = jnp.where(lane == ik, -jnp.inf, cur)
    es = [jnp.exp(v - vals[0]) for v in vals]
    den = es[0] + es[1] + es[2] + es[3]
    @pl.when(pl.program_id(0) == 0)
    def _():
        cnt_scr[...] = jnp.zeros_like(cnt_scr)

    tm = logits.shape[0]
    picked = jnp.zeros(logits.shape, F32)
    for k in range(TOP_K):
        picked = picked + (lane == idxs[k]).astype(F32)
    earlier = (lax.broadcasted_iota(jnp.int32, (tm, tm), 1)
               < lax.broadcasted_iota(jnp.int32, (tm, tm), 0)).astype(BF16)
    before = _dot(earlier, picked.astype(BF16)) + cnt_scr[...]
    ri = jnp.zeros(logits.shape, jnp.int32)
    rw = jnp.zeros(logits.shape, F32)
    for k in range(TOP_K):
        rank_k = jnp.sum(jnp.where(lane == idxs[k], before, 0.0), axis=1, keepdims=True)
        ri = jnp.where(lane == k, idxs[k], ri)
        ri = jnp.where(lane == TOP_K + k, rank_k.astype(jnp.int32), ri)
        rw = jnp.where(lane == k, es[k] / den, rw)
    ri_ref[...] = ri
    rw_ref[...] = rw
    cnt_scr[...] += jnp.sum(picked, axis=0, keepdims=True)
    cnt_ref[...] = jnp.broadcast_to(cnt_scr[...], cnt_ref.shape)


def _mix_out(m, x, g1, sh2, sc2, w_o, ln_g, ln_b, w_r, b_r, *, seq_len):
    T, D = x.shape
    tm = min(256, T)
    if seq_len > 0:
        seq_tiles = seq_len // tm
        mod_spec = pl.BlockSpec((None, 1, D), lambda i: (i // seq_tiles, 0, 0))
    else:
        mod_spec = pl.BlockSpec((tm, D), lambda i: (i, 0))
    row = pl.BlockSpec((tm, D), lambda i: (i, 0))
    vec = pl.BlockSpec((1, D), lambda i: (0, 0))
    small = pl.BlockSpec((tm, LANES), lambda i: (i, 0))
    return pl.pallas_call(
        _mix_out_kernel,
        grid=(T // tm,),
        in_specs=[row, row, mod_spec, mod_spec, mod_spec,
                  pl.BlockSpec((D, D), lambda i: (0, 0)), vec, vec,
                  pl.BlockSpec((D, LANES), lambda i: (0, 0)),
                  pl.BlockSpec((1, LANES), lambda i: (0, 0))],
        out_specs=[row, row, small, small, pl.BlockSpec((8, LANES), lambda i: (0, 0))],
        out_shape=[jax.ShapeDtypeStruct((T, D), F32), jax.ShapeDtypeStruct((T, D), F32),
                   jax.ShapeDtypeStruct((T, LANES), jnp.int32), jax.ShapeDtypeStruct((T, LANES), F32),
                   jax.ShapeDtypeStruct((8, LANES), F32)],
        scratch_shapes=[pltpu.VMEM((1, LANES), F32)],
        compiler_params=_cp(("arbitrary",)),
        name="mix_out",
    )(m, x, g1, sh2, sc2, w_o, ln_g, ln_b, w_r, b_r)


def _gather_kernel(cnt_ref, blk_ref, idx_ref, idxn_ref, x_hbm, o_ref, buf, sem):
    j = pl.program_id(0)
    nb = pl.num_programs(0)
    slot = j % 2
    jn = jnp.minimum(j + 1, nb - 1)

    def rows8(step):
        return pl.multiple_of((cnt_ref[step] + 7) // 8 * 8, 8)

    def issue_block(idx_r, n8, sl):
        def issue(g, _):
            for u in range(8):
                r = g * 8 + u
                tok = idx_r[0, 0, r]
                pltpu.make_async_copy(x_hbm.at[pl.ds(tok, 1), :], buf.at[sl, pl.ds(r, 1), :], sem.at[sl]).start()
            return 0

        lax.fori_loop(0, n8 // 8, issue, 0)

    @pl.when(j == 0)
    def _():
        buf[...] = jnp.zeros_like(buf)
        issue_block(idx_ref, rows8(0), 0)

    @pl.when((j + 1 < nb) & (cnt_ref[jn] > 0))
    def _():
        issue_block(idxn_ref, rows8(jn), 1 - slot)

    @pl.when(cnt_ref[j] > 0)
    def _():
        n8 = rows8(j)
        pltpu.make_async_copy(x_hbm.at[pl.ds(0, n8), :], buf.at[slot, pl.ds(0, n8), :], sem.at[slot]).wait()
        o_ref[...] = buf[slot].astype(o_ref.dtype)


def _moe_gather(x, idx, cnt, blk):
    T, D = x.shape
    P = idx.shape[0]
    gb = MOE_SUB
    nb = P // gb
    grid_spec = pltpu.PrefetchScalarGridSpec(
        num_scalar_prefetch=2,
        grid=(nb,),
        in_specs=[pl.BlockSpec((1, 1, gb), lambda j, cnt, blk: (blk[j], 0, 0), memory_space=pltpu.SMEM),
                  pl.BlockSpec((1, 1, gb), lambda j, cnt, blk: (blk[jnp.minimum(j + 1, nb - 1)], 0, 0),
                               memory_space=pltpu.SMEM),
                  pl.BlockSpec(memory_space=pl.ANY)],
        out_specs=pl.BlockSpec((gb, D), lambda j, cnt, blk: (blk[j], 0)),
        scratch_shapes=[pltpu.VMEM((2, gb, D), F32), pltpu.SemaphoreType.DMA((2,))],
    )
    idx3 = idx.reshape(nb, 1, gb)
    return pl.pallas_call(
        _gather_kernel,
        grid_spec=grid_spec,
        out_shape=jax.ShapeDtypeStruct((P, D), BF16),
        compiler_params=pltpu.CompilerParams(dimension_semantics=("arbitrary",), vmem_limit_bytes=VMEM_LIMIT,
                                             disable_bounds_checks=True),
        name="moe_gather",
    )(cnt, blk, idx3, idx3, x)


def _moe_kernel(be_ref, nu_ref, nv_ref, x_ref, wg_ref, wu_ref, wd_ref, bgu_ref, bdn_ref, y_ref,
                a_scr):
    i = pl.program_id(0)
    s = pl.program_id(1)
    nf = pl.num_programs(1) // 2
    valid = nv_ref[i]

    @pl.when((s < nf) & (valid > 0))
    def _():
        def build(base, m):
            rows = pl.ds(base, m)
            xs = x_ref[rows, :].astype(F32)
            gate = jnp.minimum(_dot(xs, wg_ref[...]) + bgu_ref[s], SWIGLU_LIMIT)
            up = jnp.clip(_dot(xs, wu_ref[...]) + bgu_ref[nf + s], -SWIGLU_LIMIT, SWIGLU_LIMIT)
            a_scr[s, rows, :] = ((up + 1.0) * (gate * jax.nn.sigmoid(SWIGLU_ALPHA * gate))).astype(BF16)

        _for_sub_blocks(valid, build, None)

    @pl.when((s >= nf) & (valid > 0))
    def _():
        def write(base, m):
            rows = pl.ds(base, m)
            a = jnp.concatenate([a_scr[f, rows, :] for f in range(a_scr.shape[0])], axis=1)
            y_ref[rows, :] = _dot(a.astype(F32), wd_ref[...]) + bdn_ref[s - nf]

        def clear(base, m):
            y_ref[pl.ds(base, m), :] = jnp.zeros((m, y_ref.shape[1]), F32)

        _for_sub_blocks(valid, write, clear)


def _for_sub_blocks(valid, work, rest):
    for sbk in range(MOE_BLOCK // MOE_SUB):
        base = sbk * MOE_SUB
        left = valid - base
        for m in range(MOE_TAIL, MOE_SUB + 1, MOE_TAIL):
            @pl.when((left > m - MOE_TAIL) & ((left <= m) if m < MOE_SUB else (left > m - MOE_TAIL)))
            def _(m=m):
                work(base, m)
                if rest is not None and m < MOE_SUB:
                    rest(base + m, MOE_SUB - m)

        if rest is not None:
            @pl.when(left <= 0)
            def _():
                rest(base, MOE_SUB)


def _moe_experts(xs, blk_e, nused, nvalid, w_gu, b_gu, w_dn, b_dn):
    P, D = xs.shape
    F = w_dn.shape[2]
    NB = P // MOE_BLOCK
    NF = F // MOE_FT
    ND = D // MOE_FT

    def eff(i, nu):
        return jnp.minimum(i, nu[0] - 1)

    def fb(i, s, nu):
        return jnp.where(i < nu[0], jnp.maximum(s - NF, 0), ND - 1)

    def nxt(i, nu):
        return jnp.minimum(i + 1, nu[0] - 1)

    def x_map(i, s, be, nu, nv):
        return (jnp.where((i < nu[0]) & (s < NF), eff(i, nu), nxt(i, nu)), 0)

    def gu_tile(i, s, be, nu, last):
        here = (i < nu[0]) & (s <= last)
        return (jnp.where(here, be[eff(i, nu)], be[nxt(i, nu)]), jnp.where(here, jnp.minimum(s, NF - 1), 0))

    def dn_tile(i, s, be, nu):
        used = i < nu[0]
        prev = jnp.maximum(eff(i, nu) - 1, 0)
        e = jnp.where(used & (s < NF), be[prev], be[eff(i, nu)])
        return e, jnp.where(used & (s >= NF), s - NF, ND - 1)

    def wg_map(i, s, be, nu, nv):
        e, f = gu_tile(i, s, be, nu, NF + ND - 2)
        return (0, e, 0, f)

    def wu_map(i, s, be, nu, nv):
        e, f = gu_tile(i, s, be, nu, NF + ND - 1)
        return (0, e, 0, NF + f)

    def wd_map(i, s, be, nu, nv):
        e, n = dn_tile(i, s, be, nu)
        return (0, e, 0, n)

    def bgu_map(i, s, be, nu, nv):
        return (gu_tile(i, s, be, nu, NF + ND - 1)[0], 0, 0, 0)

    def bdn_map(i, s, be, nu, nv):
        return (dn_tile(i, s, be, nu)[0], 0, 0, 0)

    grid_spec = pltpu.PrefetchScalarGridSpec(
        num_scalar_prefetch=3,
        grid=(NB, NF + ND),
        in_specs=[
            pl.BlockSpec((MOE_BLOCK, D), x_map),
            pl.BlockSpec((None, None, D, MOE_FT), wg_map),
            pl.BlockSpec((None, None, D, MOE_FT), wu_map),
            pl.BlockSpec((None, None, F, MOE_FT), wd_map),
            pl.BlockSpec((None, 2 * NF, 1, MOE_FT), bgu_map),
            pl.BlockSpec((None, ND, 1, MOE_FT), bdn_map),
        ],
        out_specs=pl.BlockSpec((MOE_BLOCK, MOE_FT), lambda i, s, be, nu, nv: (eff(i, nu), fb(i, s, nu))),
        scratch_shapes=[pltpu.VMEM((NF, MOE_BLOCK, MOE_FT), BF16)],
    )
    assert NF == ND
    return pl.pallas_call(
        _moe_kernel,
        grid_spec=grid_spec,
        out_shape=jax.ShapeDtypeStruct((P, D), F32),
        compiler_params=_cp(("arbitrary", "arbitrary"), MOE_VMEM_LIMIT),
        name="moe_experts",
    )(blk_e, nused, nvalid, xs, w_gu, w_gu, w_dn,
      b_gu.reshape(-1, 2 * NF, 1, MOE_FT), b_dn.reshape(-1, ND, 1, MOE_FT))


def _combine_kernel(idx_ref, idxn_ref, y_hbm, x1_ref, g2_ref, rw_ref, lg_ref, lb_ref, o_ref, buf, sem):
    i = pl.program_id(0)
    nt = pl.num_programs(0)
    slot = i % 2
    tm = x1_ref.shape[0]

    def issue_tile(idx_r, sl):
        def issue(r, _):
            for k in range(TOP_K):
                d = idx_r[0, 0, k * tm + r]
                pltpu.make_async_copy(y_hbm.at[pl.ds(d, 1), :], buf.at[sl, k, pl.ds(r, 1), :], sem.at[sl]).start()
            return 0

        lax.fori_loop(0, tm, issue, 0, unroll=2)

    @pl.when(i == 0)
    def _():
        issue_tile(idx_ref, 0)

    @pl.when(i + 1 < nt)
    def _():
        issue_tile(idxn_ref, 1 - slot)

    for k in range(TOP_K):
        pltpu.make_async_copy(y_hbm.at[pl.ds(0, tm), :], buf.at[slot, k], sem.at[slot]).wait()
    rw = rw_ref[...]
    moe = rw[:, 0:1] * buf[slot, 0]
    for k in range(1, TOP_K):
        moe = moe + rw[:, k:k + 1] * buf[slot, k]
    o_ref[...] = _layernorm(DN_ALPHA * x1_ref[...] + g2_ref[...] * moe, lg_ref[...], lb_ref[...])


def _moe_combine(y, dest, x1, g2, rw, ln_g, ln_b, *, seq_len):
    T, D = x1.shape
    tm = 128
    nt = T // tm
    idx = dest.reshape(nt, tm, TOP_K).transpose(0, 2, 1).reshape(nt, 1, TOP_K * tm)
    if seq_len > 0:
        seq_tiles = seq_len // tm
        mod_spec = pl.BlockSpec((None, 1, D), lambda i: (i // seq_tiles, 0, 0))
    else:
        mod_spec = pl.BlockSpec((tm, D), lambda i: (i, 0))
    row = pl.BlockSpec((tm, D), lambda i: (i, 0))
    vec = pl.BlockSpec((1, D), lambda i: (0, 0))
    return pl.pallas_call(
        _combine_kernel,
        grid=(nt,),
        in_specs=[pl.BlockSpec((1, 1, TOP_K * tm), lambda i: (i, 0, 0), memory_space=pltpu.SMEM),
                  pl.BlockSpec((1, 1, TOP_K * tm), lambda i: (jnp.minimum(i + 1, nt - 1), 0, 0),
                               memory_space=pltpu.SMEM),
                  pl.BlockSpec(memory_space=pl.ANY),
                  row, mod_spec, pl.BlockSpec((tm, LANES), lambda i: (i, 0)), vec, vec],
        out_specs=row,
        out_shape=jax.ShapeDtypeStruct((T, D), F32),
        scratch_shapes=[pltpu.VMEM((2, TOP_K, tm, D), F32), pltpu.SemaphoreType.DMA((2,))],
        compiler_params=pltpu.CompilerParams(dimension_semantics=("arbitrary",), vmem_limit_bytes=VMEM_LIMIT,
                                             disable_bounds_checks=True),
        name="moe_combine",
    )(idx, idx, y, x1, g2, rw, ln_g, ln_b)


def _lookup(table, idx):
    hit = idx[..., None] == jnp.arange(N_EXPERTS, dtype=jnp.int32)
    return jnp.sum(jnp.where(hit, table, 0), axis=-1)


def _route(experts, ranks, counts):
    T = experts.shape[0]
    A = T * TOP_K
    flat_e = experts.reshape(A)
    rank = ranks.reshape(A)
    nblk = (counts + MOE_BLOCK - 1) // MOE_BLOCK
    blk_end = jnp.cumsum(nblk)
    blk_start = blk_end - nblk
    dest = _lookup(blk_start, flat_e) * MOE_BLOCK + rank
    NB = -(-A // MOE_BLOCK) + N_EXPERTS
    idx = jnp.zeros((NB * MOE_BLOCK,), jnp.int32).at[dest].set(
        jnp.arange(A, dtype=jnp.int32) // TOP_K, unique_indices=True, mode='promise_in_bounds')
    blk_e = jnp.minimum(jnp.searchsorted(blk_end, jnp.arange(NB, dtype=jnp.int32), side='right'),
                        N_EXPERTS - 1).astype(jnp.int32)
    nused = blk_end[-1:].astype(jnp.int32)
    blk = jnp.arange(NB, dtype=jnp.int32)
    nvalid = jnp.clip(counts[blk_e] - (blk - blk_start[blk_e]) * MOE_BLOCK, 0, MOE_BLOCK)
    nvalid = jnp.where(blk < nused[0], nvalid, 0).astype(jnp.int32)
    per = MOE_BLOCK // MOE_SUB
    gblk = jnp.arange(NB * per, dtype=jnp.int32)
    gcnt = jnp.clip(jnp.repeat(nvalid, per) - (gblk % per) * MOE_SUB, 0, MOE_SUB).astype(jnp.int32)
    order = jnp.argsort(gcnt == 0, stable=True).astype(jnp.int32)
    nnz = jnp.sum(gcnt > 0)
    gvisit = jnp.where(gblk < nnz, order, order[jnp.maximum(nnz - 1, 0)])
    gcnt = jnp.where(gblk < nnz, gcnt[order], 0)
    return dest.reshape(T, TOP_K), idx, blk_e, nused, nvalid, gcnt, gvisit


def _block_diag_in(bb):
    gpc = S5_GROUPS // S5_CHUNKS
    x = bb.reshape(S5_CHUNKS, gpc, S5_GROUP, S5_STATE)
    eye = jnp.eye(gpc, dtype=bb.dtype)
    return jnp.einsum('agcn,gh->agchn', x, eye).reshape(S5_CHUNKS, S5_CHUNK_IN, S5_CHUNK_STATE)


def _block_diag_out(cc):
    gpc = S5_GROUPS // S5_CHUNKS
    x = cc.reshape(S5_CHUNKS, gpc, S5_GROUP, S5_STATE)
    eye = jnp.eye(gpc, dtype=cc.dtype)
    return jnp.einsum('agcn,gh->ahngc', x, eye).reshape(S5_CHUNKS, S5_CHUNK_STATE, S5_CHUNK_IN)


def kernel(x_prompt, x_sample, c_prompt, c_sample, cache_k, cache_v, cache_logf, state_s5_re, state_s5_im, page_table, w_ada, b_ada, w_in, fox_b_f, s5_a_re, s5_a_im, s5_log_dt, s5_b_re, s5_b_im, s5_c_re, s5_c_im, s5_d, w_glu, b_glu, p_a, p_b, w_o, ln1_g, ln1_b, w_router, b_router, w_gu, b_gu, w_dn, b_dn, ln2_g, ln2_b):
    B, L, D = x_prompt.shape
    Bd, Ld, _ = x_sample.shape
    Tp, Ts = B * L, Bd * Ld
    W = HEADS * HEAD_DIM
    n_phys = cache_k.shape[1]

    n_c = B + Bd
    c_all = jnp.concatenate([c_prompt, c_sample, jnp.zeros((-n_c % 8, D), F32)], axis=0)
    mod = _adaln(c_all, w_ada[0], b_ada)
    mp = [mod[:B, i * D:(i + 1) * D].reshape(B, 1, D) for i in range(6)]
    ms = [jnp.repeat(mod[B:n_c, i * D:(i + 1) * D], Ld, axis=0) for i in range(6)]

    wit = w_in[0].T
    f_lo = 3 * W
    wmt = jnp.concatenate([wit[:f_lo], wit[f_lo + HEADS:]], axis=0).astype(BF16)
    wft = jnp.pad(wit[f_lo:f_lo + HEADS], ((0, LANES - HEADS), (0, 0))).astype(BF16)
    bf = jnp.pad(fox_b_f, ((0, 0), (0, LANES - HEADS)))
    hp, logf_p, cT_p = _in_proj(x_prompt.reshape(Tp, D), mp[0], mp[1], wmt, wft.T, wft, bf, bf.T,
                                seq_len=L, seg=0)
    hs, logf_s, c_s = _in_proj(x_sample.reshape(Ts, D), ms[0], ms[1], wmt, wft.T, wft, bf, bf.T,
                               seq_len=0, seg=Ld)

    lam_r, lam_i, bb_r, bb_i = _s5_params(s5_a_re[0], s5_a_im[0], s5_log_dt[0][:, None],
                                          s5_b_re[0].transpose(0, 2, 1), s5_b_im[0].transpose(0, 2, 1))
    lam_r = lam_r.reshape(S5_CHUNKS, S5_CHUNK_STATE)
    lam_i = lam_i.reshape(S5_CHUNKS, S5_CHUNK_STATE)
    wb = jnp.concatenate([_block_diag_in(bb_r), _block_diag_in(bb_i)], axis=2).astype(BF16)
    wcr = _block_diag_out(s5_c_re[0]).astype(BF16)
    wci = _block_diag_out(s5_c_im[0]).astype(BF16)
    wgl = w_glu[0].astype(BF16)
    s5_args = (lam_r, lam_i, wb, wcr, wci, s5_d, wgl, b_glu)
    zeros_state = jnp.zeros((B, S5_CHUNKS, S5_CHUNK_STATE), F32)
    s5_t = 256
    ssm_p, sr_p, si_p = _s5(hp, 3, zeros_state, zeros_state, *s5_args, nseq=1, slen=s5_t, chunks=L // s5_t)
    ssm_s, sr_s, si_s = _s5(hs, 3, state_s5_re[0].reshape(Bd, S5_CHUNKS, S5_CHUNK_STATE),
                            state_s5_im[0].reshape(Bd, S5_CHUNKS, S5_CHUNK_STATE), *s5_args,
                            nseq=32, slen=Ld, chunks=1)

    att_p = _fox_prompt(hp, cT_p, B, L)
    scale = HEAD_DIM ** -0.5
    q_s = hs[:, :W].reshape(Bd, Ld, HEADS, HEAD_DIM) * scale
    k_s = hs[:, W:2 * W].reshape(Bd, Ld, HEADS, HEAD_DIM)
    v_s = hs[:, 2 * W:3 * W].reshape(Bd, Ld, HEADS, HEAD_DIM)
    qe = q_s.reshape(Bd, Ld * HEADS, HEAD_DIM)
    kn = jnp.broadcast_to(k_s[:, :, None], (Bd, Ld, Ld, HEADS, HEAD_DIM)).reshape(Bd, Ld, Ld * HEADS, HEAD_DIM)
    vn = jnp.broadcast_to(v_s[:, :, None], (Bd, Ld, Ld, HEADS, HEAD_DIM)).reshape(Bd, Ld, Ld * HEADS, HEAD_DIM)
    cn = c_s[:, :HEADS].reshape(Bd, Ld, HEADS)
    bn = jnp.broadcast_to(cn.transpose(0, 2, 1)[:, None], (Bd, Ld, HEADS, Ld)).reshape(Bd, Ld * HEADS, Ld)
    bn = jnp.pad(bn, ((0, 0), (0, 0), (0, LANES - Ld)))
    ckt = jnp.transpose(cache_k[0], (0, 2, 3, 1)).reshape(n_phys, W, PAGE)
    cvt = jnp.transpose(cache_v[0], (0, 2, 3, 1)).reshape(n_phys, W, PAGE)
    clft = jnp.transpose(cache_logf[0], (0, 2, 1))
    att_s = _fox_sample(page_table, ckt, cvt, clft, qe, kn, vn, bn)
    att_s = att_s.reshape(Ts, W).astype(BF16)

    pab, pbb, wob = p_a[0].astype(BF16), p_b[0].astype(BF16), w_o[0].astype(BF16)
    wr = jnp.pad(w_router[0], ((0, 0), (0, LANES - N_EXPERTS)))
    br = jnp.pad(b_router, ((0, 0), (0, LANES - N_EXPERTS)), constant_values=NEG)
    m_p = _mix_gate(att_p, ssm_p, hp, pab, pbb)
    m_s = _mix_gate(att_s, ssm_s, hs, pab, pbb)
    x1_p, u2_p, ri_p, rw_p, cnt_p = _mix_out(m_p, x_prompt.reshape(Tp, D), mp[2], mp[3], mp[4], wob, ln1_g, ln1_b,
                                             wr, br, seq_len=L)
    x1_s, u2_s, ri_s, rw_s, cnt_s = _mix_out(m_s, x_sample.reshape(Ts, D), ms[2], ms[3], ms[4], wob, ln1_g, ln1_b,
                                             wr, br, seq_len=0)

    u2 = jnp.concatenate([u2_p, u2_s], axis=0)
    cnt_p = cnt_p[0, :N_EXPERTS].astype(jnp.int32)
    cnt_s = cnt_s[0, :N_EXPERTS].astype(jnp.int32)
    ex_p, ex_s = ri_p[:, :TOP_K], ri_s[:, :TOP_K]
    experts = jnp.concatenate([ex_p, ex_s], axis=0)
    ranks = jnp.concatenate([ri_p[:, TOP_K:2 * TOP_K], ri_s[:, TOP_K:2 * TOP_K] + _lookup(cnt_p, ex_s)], axis=0)
    dest, idx, blk_e, nused, nvalid, gcnt, gvisit = _route(experts, ranks, cnt_p + cnt_s)
    xs = _moe_gather(u2, idx, gcnt, gvisit)
    y = _moe_experts(xs, blk_e, nused, nvalid, w_gu, b_gu[0], w_dn, b_dn[0])
    x2_p = _moe_combine(y, dest[:Tp], x1_p, mp[5], rw_p, ln2_g, ln2_b, seq_len=L)
    x2_s = _moe_combine(y, dest[Tp:], x1_s, ms[5], rw_s, ln2_g, ln2_b, seq_len=0)

    def kv_out(t, n, l):
        return t.reshape(n, HEADS, HEAD_DIM, l).transpose(0, 3, 1, 2)[None]

    def heads(t, n, l):
        return t.reshape(1, n, l, HEADS, HEAD_DIM)

    def state(s, n):
        return s.reshape(1, n, S5_GROUPS, S5_STATE)

    return (x2_p.reshape(B, L, D), x2_s.reshape(Bd, Ld, D),
            kv_out(_heads_minor_time(hp, 1, B, L), B, L), kv_out(_heads_minor_time(hp, 2, B, L), B, L),
            logf_p[:, :HEADS].reshape(1, B, L, HEADS), state(sr_p, B), state(si_p, B),
            heads(hs[:, W:2 * W], Bd, Ld), heads(hs[:, 2 * W:3 * W], Bd, Ld),
            logf_s[:, :HEADS].reshape(1, Bd, Ld, HEADS), state(sr_s, Bd), state(si_s, Bd))
```

```python
import functools
import math

import jax
import jax.numpy as jnp
from jax import lax
from jax.experimental import pallas as pl
from jax.experimental.pallas import tpu as pltpu

F32 = jnp.float32
BF16 = jnp.bfloat16

HEADS = 16
HEAD_DIM = 64
PAGE = 128
S5_GROUPS = 64
S5_GROUP = 16
S5_STATE = 64
N_EXPERTS = 32
TOP_K = 4
SWIGLU_LIMIT = 7.0
SWIGLU_ALPHA = 1.702
DN_ALPHA = 2.0 ** 0.25
LN_EPS = 1e-5
NEG = -1e30
LOG2E = math.log2(math.e)

LANES = 128
S5_CHUNKS = 8
S5_CHUNK_STATE = S5_GROUPS * S5_STATE // S5_CHUNKS
S5_CHUNK_IN = S5_GROUPS * S5_GROUP // S5_CHUNKS
S5_LANE_BLOCKS = S5_CHUNK_STATE // LANES
FOX_TQ = 1024
FOX_TK = 1024
MOE_BLOCK = 1536
MOE_SUB = 384
MOE_TAIL = 128
MOE_FT = 512
VMEM_LIMIT = 56 * 1024 * 1024
MOE_VMEM_LIMIT = 60 * 1024 * 1024


def _cp(sem, vmem=VMEM_LIMIT):
    return pltpu.CompilerParams(dimension_semantics=sem, vmem_limit_bytes=vmem)


def _dot(a, b):
    return jnp.dot(a, b, preferred_element_type=F32)


def _dot_nt(a, b):
    return lax.dot_general(a, b, (((1,), (1,)), ((), ())), preferred_element_type=F32)


def _log_sigmoid(x):
    return jnp.minimum(x, 0.0) - jnp.log1p(jnp.exp(-jnp.abs(x)))


def _split3(v):
    hi = v.astype(BF16)
    r = v - hi.astype(F32)
    mid = r.astype(BF16)
    lo = (r - mid.astype(F32)).astype(BF16)
    return hi, mid, lo


def _layernorm(v, g, b):
    mu = jnp.mean(v, axis=-1, keepdims=True)
    d = v - mu
    var = jnp.mean(d * d, axis=-1, keepdims=True)
    return d * lax.rsqrt(var + LN_EPS) * g + b


def _ada_kernel(c_ref, w_ref, b_ref, o_ref):
    c = c_ref[...]
    a = (c * jax.nn.sigmoid(c)).astype(BF16)
    o_ref[...] = _dot(a, w_ref[...].astype(BF16)) + b_ref[...]


def _adaln(c_all, w_ada, b_ada):
    R, D = c_all.shape
    N = w_ada.shape[1]
    tn = 1024
    return pl.pallas_call(
        _ada_kernel,
        grid=(N // tn,),
        in_specs=[pl.BlockSpec((R, D), lambda j: (0, 0)),
                  pl.BlockSpec((D, tn), lambda j: (0, j)),
                  pl.BlockSpec((1, tn), lambda j: (0, j))],
        out_specs=pl.BlockSpec((R, tn), lambda j: (0, j)),
        out_shape=jax.ShapeDtypeStruct((R, N), F32),
        compiler_params=_cp(("arbitrary",)),
        name="adaln",
    )(c_all, w_ada, b_ada)


def _in_kernel(x_ref, sh_ref, sc_ref, wm_ref, wf_ref, wft_ref, bf_ref, bft_ref,
               h_ref, logf_ref, cum_ref, u_scr, carry_scr, *, seq_tiles, seg):
    i = pl.program_id(0)
    j = pl.program_id(1)
    tm = x_ref.shape[0]

    @pl.when(j == 0)
    def _():
        u = x_ref[...] * (1.0 + sc_ref[...]) + sh_ref[...]
        ub = u.astype(BF16)
        u_scr[...] = ub
        lf = _log_sigmoid(_dot(ub, wf_ref[...]) + bf_ref[...])
        logf_ref[...] = lf
        r = lax.broadcasted_iota(jnp.int32, (tm, tm), 0)
        c = lax.broadcasted_iota(jnp.int32, (tm, tm), 1)
        if seq_tiles > 0:
            lft = _log_sigmoid(_dot_nt(wft_ref[...], ub) + bft_ref[...])
            tri = (r <= c).astype(BF16)
            hi, mid, lo = _split3(lft)
            cs = _dot(hi, tri) + _dot(mid, tri) + _dot(lo, tri)

            @pl.when(i % seq_tiles == 0)
            def _():
                carry_scr[...] = jnp.zeros_like(carry_scr)

            cs = cs + carry_scr[...]
            cum_ref[...] = cs
            carry_scr[...] = cs[:, tm - 1:tm]
        else:
            tri = ((c <= r) & (c // seg == r // seg)).astype(BF16)
            hi, mid, lo = _split3(lf)
            cum_ref[...] = _dot(tri, hi) + _dot(tri, mid) + _dot(tri, lo)

    h_ref[...] = _dot_nt(u_scr[...], wm_ref[...])


def _in_proj(x, shift, scale, wmt, wf, wft, bf, bft, *, seq_len, seg):
    T, D = x.shape
    N = wmt.shape[0]
    tm, tn = min(1024, T), 1024
    nt = T // tm
    if seq_len > 0:
        seq_tiles = seq_len // tm
        mod_spec = pl.BlockSpec((None, 1, D), lambda i, j: (i // seq_tiles, 0, 0))
        cum_shape, cum_spec = (LANES, T), pl.BlockSpec((LANES, tm), lambda i, j: (0, i))
    else:
        seq_tiles = 0
        mod_spec = pl.BlockSpec((tm, D), lambda i, j: (i, 0))
        cum_shape, cum_spec = (T, LANES), pl.BlockSpec((tm, LANES), lambda i, j: (i, 0))
    kern = functools.partial(_in_kernel, seq_tiles=seq_tiles, seg=seg)
    return pl.pallas_call(
        kern,
        grid=(nt, N // tn),
        in_specs=[pl.BlockSpec((tm, D), lambda i, j: (i, 0)), mod_spec, mod_spec,
                  pl.BlockSpec((tn, D), lambda i, j: (j, 0)),
                  pl.BlockSpec((D, LANES), lambda i, j: (0, 0)),
                  pl.BlockSpec((LANES, D), lambda i, j: (0, 0)),
                  pl.BlockSpec((1, LANES), lambda i, j: (0, 0)),
                  pl.BlockSpec((LANES, 1), lambda i, j: (0, 0))],
        out_specs=[pl.BlockSpec((tm, tn), lambda i, j: (i, j)),
                   pl.BlockSpec((tm, LANES), lambda i, j: (i, 0)),
                   cum_spec],
        out_shape=[jax.ShapeDtypeStruct((T, N), F32),
                   jax.ShapeDtypeStruct((T, LANES), F32),
                   jax.ShapeDtypeStruct(cum_shape, F32)],
        scratch_shapes=[pltpu.VMEM((tm, D), BF16), pltpu.VMEM((LANES, 1), F32)],
        compiler_params=_cp(("arbitrary", "arbitrary")),
        name="in_proj",
    )(x, shift, scale, wmt, wf, wft, bf, bft)


def _transpose_kernel(x_ref, o_ref):
    o_ref[...] = x_ref[...].T


def _heads_minor_time(h, col_block, B, L):
    W = HEADS * HEAD_DIM
    tm = min(512, L)
    nt = L // tm
    return pl.pallas_call(
        _transpose_kernel,
        grid=(B, nt),
        in_specs=[pl.BlockSpec((tm, W), lambda b, i: (b * nt + i, col_block))],
        out_specs=pl.BlockSpec((None, W, tm), lambda b, i: (b, 0, i)),
        out_shape=jax.ShapeDtypeStruct((B, W, L), F32),
        compiler_params=_cp(("parallel", "parallel")),
        name="heads_minor_time",
    )(h)


def _fox_prompt_kernel(qt_ref, kt_ref, q_ref, k_ref, v_ref, c_ref, o_ref, qs, m_scr, l_scr, acc, *, ratio):
    hp = pl.program_id(1)
    t = pl.program_id(2)
    qi = qt_ref[t]
    kv = kt_ref[t]
    tq, tk = q_ref.shape[0], k_ref.shape[0]
    lane = lax.broadcasted_iota(jnp.int32, (tq, LANES), 1)

    @pl.when(kv == 0)
    def _():
        q2 = q_ref[...] * (HEAD_DIM ** -0.5 * LOG2E)
        qs[0] = jnp.where(lane < HEAD_DIM, q2, 0.0).astype(BF16)
        qs[1] = jnp.where(lane >= HEAD_DIM, q2, 0.0).astype(BF16)
        m_scr[...] = jnp.full_like(m_scr, NEG)
        l_scr[...] = jnp.zeros_like(l_scr)
        acc[...] = jnp.zeros_like(acc)

    def block(diagonal):
        kb = k_ref[...].astype(BF16)
        vb = v_ref[...].astype(BF16)
        r0 = (2 * hp) % 8
        for a in range(2):
            crow = c_ref[pl.ds(r0 + a, 1), :] * LOG2E
            s = _dot_nt(qs[a], kb) - crow
            if diagonal:
                row = lax.broadcasted_iota(jnp.int32, (tq, tk), 0)
                col = lax.broadcasted_iota(jnp.int32, (tq, tk), 1) + (kv * tk - qi * tq)
                s = jnp.where(col <= row, s, NEG)
            cols = [s[:, c * LANES:(c + 1) * LANES] for c in range(tk // LANES)]
            mx = cols[0]
            for sc in cols[1:]:
                mx = jnp.maximum(mx, sc)
            m_prev = m_scr[a]
            m_new = jnp.maximum(m_prev, jnp.max(mx, axis=1, keepdims=True))
            alpha = jnp.exp2(m_prev - m_new)
            ps = [jnp.exp2(sc - m_new) for sc in cols]
            lsum = ps[0]
            for pc in ps[1:]:
                lsum = lsum + pc
            l_scr[a] = alpha * l_scr[a] + lsum
            p = jnp.concatenate(ps, axis=1).astype(BF16)
            acc[a] = alpha * acc[a] + _dot(p, vb)
            m_scr[a] = m_new

    @pl.when(kv < qi * ratio)
    def _():
        block(False)

    @pl.when(kv >= qi * ratio)
    def _():
        block(True)

    @pl.when(kv == (qi + 1) * ratio - 1)
    def _():
        l0 = jnp.sum(l_scr[0], axis=1, keepdims=True)
        l1 = jnp.sum(l_scr[1], axis=1, keepdims=True)
        o = jnp.where(lane < HEAD_DIM, acc[0] / l0, acc[1] / l1)
        o_ref[...] = o.astype(o_ref.dtype)


def _fox_prompt(h, cT, B, L):
    tq, tk = FOX_TQ, FOX_TK
    ratio = tq // tk
    nq, nk = L // tq, L // tk
    npairs = HEADS // 2
    steps = [(qi, kv) for qi in range(nq) for kv in range((qi + 1) * ratio)]
    qt = jnp.asarray([p[0] for p in steps], jnp.int32)
    kt = jnp.asarray([p[1] for p in steps], jnp.int32)
    grid_spec = pltpu.PrefetchScalarGridSpec(
        num_scalar_prefetch=2,
        grid=(B, npairs, len(steps)),
        in_specs=[
            pl.BlockSpec((tq, LANES), lambda b, hp, t, qt, kt: (b * nq + qt[t], hp)),
            pl.BlockSpec((tk, LANES), lambda b, hp, t, qt, kt: (b * nk + kt[t], npairs + hp)),
            pl.BlockSpec((tk, LANES), lambda b, hp, t, qt, kt: (b * nk + kt[t], 2 * npairs + hp)),
            pl.BlockSpec((8, tk), lambda b, hp, t, qt, kt: (hp // 4, b * nk + kt[t])),
        ],
        out_specs=pl.BlockSpec((tq, LANES), lambda b, hp, t, qt, kt: (b * nq + qt[t], hp)),
        scratch_shapes=[pltpu.VMEM((2, tq, LANES), BF16), pltpu.VMEM((2, tq, LANES), F32),
                        pltpu.VMEM((2, tq, LANES), F32), pltpu.VMEM((2, tq, LANES), F32)],
    )
    return pl.pallas_call(
        functools.partial(_fox_prompt_kernel, ratio=ratio),
        grid_spec=grid_spec,
        out_shape=jax.ShapeDtypeStruct((B * L, HEADS * HEAD_DIM), BF16),
        compiler_params=_cp(("parallel", "parallel", "arbitrary")),
        name="fox_prompt",
    )(qt, kt, h, h, h, cT)


def _fox_sample_kernel(pt_ref, *refs, n_pages):
    k_refs = refs[:n_pages]
    v_refs = refs[n_pages:2 * n_pages]
    lf_refs = refs[2 * n_pages:3 * n_pages]
    qe_ref, kn_ref, vn_ref, bn_ref, o_ref = refs[3 * n_pages:]
    nq = qe_ref.shape[0] // HEADS
    nrow = nq * HEADS
    width = HEADS * HEAD_DIM

    xs = jnp.concatenate([r[...] for r in lf_refs], axis=0)
    hi, mid, lo = _split3(xs)
    after = (lax.broadcasted_iota(jnp.int32, (PAGE, PAGE), 0)
             > lax.broadcasted_iota(jnp.int32, (PAGE, PAGE), 1)).astype(BF16)
    within = _dot(hi, after) + _dot(mid, after) + _dot(lo, after)
    n = n_pages * HEADS
    pr = lax.broadcasted_iota(jnp.int32, (n, n), 0)
    pc = lax.broadcasted_iota(jnp.int32, (n, n), 1)
    later = ((pc // HEADS > pr // HEADS) & (pc % HEADS == pr % HEADS)).astype(BF16)
    beyond = jnp.sum(_dot(later, hi) + _dot(later, mid) + _dot(later, lo), axis=1, keepdims=True)
    bias_rows = within + beyond
    bias = jnp.concatenate([bias_rows[j * HEADS:(j + 1) * HEADS] for j in range(n_pages)], axis=1)
    bias = jnp.concatenate([bias] * nq, axis=0)

    qe = qe_ref[...]
    rh = lax.broadcasted_iota(jnp.int32, (nrow, width), 0) % HEADS
    ch = lax.broadcasted_iota(jnp.int32, (nrow, width), 1) // HEAD_DIM
    tile = (lax.broadcasted_iota(jnp.int32, (HEAD_DIM, width), 0)
            == lax.broadcasted_iota(jnp.int32, (HEAD_DIM, width), 1) % HEAD_DIM).astype(BF16)
    qbd = jnp.where(rh == ch, _dot(qe.astype(BF16), tile), 0.0).astype(BF16)

    kt = jnp.concatenate([r[...].astype(BF16) for r in k_refs], axis=1)
    s = _dot(qbd, kt) + bias

    q_of_row = lax.broadcasted_iota(jnp.int32, (nrow, 1), 0) // HEADS
    bn = bn_ref[...]
    sn = []
    for j in range(nq):
        sj = jnp.sum(qe * kn_ref[j], axis=1, keepdims=True) - bn[:, j:j + 1]
        sn.append(jnp.where(q_of_row >= j, sj, NEG))
    m = jnp.max(s, axis=1, keepdims=True)
    for sj in sn:
        m = jnp.maximum(m, sj)
    p = jnp.exp(s - m)
    l = jnp.sum(p, axis=1, keepdims=True)

    vt = jnp.concatenate([r[...].astype(BF16) for r in v_refs], axis=1)
    o_all = _dot_nt(p.astype(BF16), vt)
    o_hi, o_mid, o_lo = _split3(jnp.where(rh == ch, o_all, 0.0))
    fold = (lax.broadcasted_iota(jnp.int32, (width, HEAD_DIM), 0) % HEAD_DIM
            == lax.broadcasted_iota(jnp.int32, (width, HEAD_DIM), 1)).astype(BF16)
    o = _dot(o_hi, fold) + _dot(o_mid, fold) + _dot(o_lo, fold)
    for j in range(nq):
        pj = jnp.exp(sn[j] - m)
        l = l + pj
        o = o + pj * vn_ref[j]
    o_ref[...] = o / l


def _fox_sample(page_table, ckt, cvt, clft, qe, kn, vn, bn):
    Bd, n_pages = page_table.shape
    nrow = qe.shape[1]
    nq = nrow // HEADS
    width = HEADS * HEAD_DIM

    def page_map(j):
        return lambda b, pt: (pt[b * n_pages + j], 0, 0)

    per_row3 = lambda b, pt: (b, 0, 0)
    per_row4 = lambda b, pt: (b, 0, 0, 0)
    grid_spec = pltpu.PrefetchScalarGridSpec(
        num_scalar_prefetch=1,
        grid=(Bd,),
        in_specs=([pl.BlockSpec((None, width, PAGE), page_map(j)) for j in range(n_pages)]
                  + [pl.BlockSpec((None, width, PAGE), page_map(j)) for j in range(n_pages)]
                  + [pl.BlockSpec((None, HEADS, PAGE), page_map(j)) for j in range(n_pages)]
                  + [pl.BlockSpec((None, nrow, HEAD_DIM), per_row3),
                     pl.BlockSpec((None, nq, nrow, HEAD_DIM), per_row4),
                     pl.BlockSpec((None, nq, nrow, HEAD_DIM), per_row4),
                     pl.BlockSpec((None, nrow, LANES), per_row3)]),
        out_specs=pl.BlockSpec((None, nrow, HEAD_DIM), per_row3),
    )
    return pl.pallas_call(
        functools.partial(_fox_sample_kernel, n_pages=n_pages),
        grid_spec=grid_spec,
        out_shape=jax.ShapeDtypeStruct((Bd, nrow, HEAD_DIM), F32),
        compiler_params=_cp(("parallel",)),
        name="fox_sample",
    )(page_table.reshape(-1), *([ckt] * n_pages), *([cvt] * n_pages), *([clft] * n_pages),
      qe, kn, vn, bn)


def _s5_param_kernel(ar_ref, ai_ref, ldt_ref, br_ref, bi_ref, lr_ref, li_ref, bbr_ref, bbi_ref):
    ar, ai = ar_ref[...], ai_ref[...]
    dt = jnp.exp(ldt_ref[...])
    mag = jnp.exp(ar * dt)
    lr = mag * jnp.cos(ai * dt)
    li = mag * jnp.sin(ai * dt)
    den = ar * ar + ai * ai
    zr = ((lr - 1.0) * ar + li * ai) / den
    zi = (li * ar - (lr - 1.0) * ai) / den
    lr_ref[...] = lr
    li_ref[...] = li
    br, bi = br_ref[...], bi_ref[...]
    zr3, zi3 = zr[:, None, :], zi[:, None, :]
    bbr_ref[...] = zr3 * br - zi3 * bi
    bbi_ref[...] = zr3 * bi + zi3 * br


def _s5_params(a_re, a_im, log_dt, bt_re, bt_im):
    G, N = a_re.shape
    C = bt_re.shape[1]
    return pl.pallas_call(
        _s5_param_kernel,
        out_shape=[jax.ShapeDtypeStruct((G, N), F32), jax.ShapeDtypeStruct((G, N), F32),
                   jax.ShapeDtypeStruct((G, C, N), F32), jax.ShapeDtypeStruct((G, C, N), F32)],
        name="s5_params",
    )(a_re, a_im, log_dt, bt_re, bt_im)


def _s5_kernel(u_ref, h0r_ref, h0i_ref, lr_ref, li_ref, wb_ref, wcr_ref, wci_ref, d_ref, wg_ref, bg_ref,
               o_ref, sr_ref, si_ref, bur, bui, y_scr, cr, ci, *, nseq, slen):
    c_id = pl.program_id(1)
    R = u_ref.shape[0]
    u = u_ref[...]
    ub = u.astype(BF16)
    rs = [_dot(ub[:, gc * S5_CHUNK_IN:(gc + 1) * S5_CHUNK_IN], wb_ref[gc]) for gc in range(S5_CHUNKS)]
    for lb in range(S5_LANE_BLOCKS):
        lo = lb * LANES
        xr = jnp.stack([r[:, lo:lo + LANES] for r in rs], axis=0)
        xi = jnp.stack([r[:, S5_CHUNK_STATE + lo:S5_CHUNK_STATE + lo + LANES] for r in rs], axis=0)
        bur[lb] = jnp.swapaxes(xr, 0, 1)
        bui[lb] = jnp.swapaxes(xi, 0, 1)

    @pl.when(c_id == 0)
    def _():
        cr[...] = h0r_ref[...]
        ci[...] = h0i_ref[...]

    lr = lr_ref[...]
    li = li_ref[...]

    def seq_body(q, _):
        base = q * slen

        def step(t, hc):
            hr, hi = hc
            br = jnp.concatenate([bur[lb, base + t] for lb in range(S5_LANE_BLOCKS)], axis=1)
            bi = jnp.concatenate([bui[lb, base + t] for lb in range(S5_LANE_BLOCKS)], axis=1)
            nr = lr * hr - li * hi + br
            ni = lr * hi + li * hr + bi
            for lb in range(S5_LANE_BLOCKS):
                bur[lb, base + t] = nr[:, lb * LANES:(lb + 1) * LANES]
                bui[lb, base + t] = ni[:, lb * LANES:(lb + 1) * LANES]
            return nr, ni

        hr, hi = lax.fori_loop(0, slen, step, (cr[q], ci[q]), unroll=4)
        cr[q] = hr
        ci[q] = hi
        return 0

    lax.fori_loop(0, nseq, seq_body, 0)
    sr_ref[...] = cr[...]
    si_ref[...] = ci[...]

    hr_t = [jnp.swapaxes(bur[lb], 0, 1) for lb in range(S5_LANE_BLOCKS)]
    hi_t = [jnp.swapaxes(bui[lb], 0, 1) for lb in range(S5_LANE_BLOCKS)]
    for gc in range(S5_CHUNKS):
        hrb = jnp.concatenate([x[gc] for x in hr_t], axis=1).astype(BF16)
        hib = jnp.concatenate([x[gc] for x in hi_t], axis=1).astype(BF16)
        y_scr[:, gc * S5_CHUNK_IN:(gc + 1) * S5_CHUNK_IN] = _dot(hrb, wcr_ref[gc]) - _dot(hib, wci_ref[gc])
    y = y_scr[...] + d_ref[...] * u
    z = jax.nn.gelu(y)
    o_ref[...] = (z * jax.nn.sigmoid(_dot(z.astype(BF16), wg_ref[...]) + bg_ref[...])).astype(o_ref.dtype)


def _s5(h, col_block, h0r, h0i, lam_r, lam_i, wb, wcr, wci, d_skip, w_glu, b_glu, *, nseq, slen, chunks):
    T = h.shape[0]
    W = S5_GROUPS * S5_GROUP
    R = nseq * slen
    nsb = T // (R * chunks)
    n_seq_total = h0r.shape[0]
    st_spec = pl.BlockSpec((nseq, S5_CHUNKS, S5_CHUNK_STATE), lambda sb, c: (sb, 0, 0))
    full3 = lambda sb, c: (0, 0, 0)
    full2 = lambda sb, c: (0, 0)
    kern = functools.partial(_s5_kernel, nseq=nseq, slen=slen)
    return pl.pallas_call(
        kern,
        grid=(nsb, chunks),
        in_specs=[pl.BlockSpec((R, W), lambda sb, c: (sb * chunks + c, col_block)),
                  st_spec, st_spec,
                  pl.BlockSpec((S5_CHUNKS, S5_CHUNK_STATE), full2),
                  pl.BlockSpec((S5_CHUNKS, S5_CHUNK_STATE), full2),
                  pl.BlockSpec((S5_CHUNKS, S5_CHUNK_IN, 2 * S5_CHUNK_STATE), full3),
                  pl.BlockSpec((S5_CHUNKS, S5_CHUNK_STATE, S5_CHUNK_IN), full3),
                  pl.BlockSpec((S5_CHUNKS, S5_CHUNK_STATE, S5_CHUNK_IN), full3),
                  pl.BlockSpec((1, W), full2),
                  pl.BlockSpec((W, W), full2),
                  pl.BlockSpec((1, W), full2)],
        out_specs=[pl.BlockSpec((R, W), lambda sb, c: (sb * chunks + c, 0)), st_spec, st_spec],
        out_shape=[jax.ShapeDtypeStruct((T, W), BF16),
                   jax.ShapeDtypeStruct((n_seq_total, S5_CHUNKS, S5_CHUNK_STATE), F32),
                   jax.ShapeDtypeStruct((n_seq_total, S5_CHUNKS, S5_CHUNK_STATE), F32)],
        scratch_shapes=[pltpu.VMEM((S5_LANE_BLOCKS, R, S5_CHUNKS, LANES), F32),
                        pltpu.VMEM((S5_LANE_BLOCKS, R, S5_CHUNKS, LANES), F32),
                        pltpu.VMEM((R, W), F32),
                        pltpu.VMEM((nseq, S5_CHUNKS, S5_CHUNK_STATE), F32),
                        pltpu.VMEM((nseq, S5_CHUNKS, S5_CHUNK_STATE), F32)],
        compiler_params=_cp(("arbitrary", "arbitrary")),
        name="s5",
    )(h, h0r, h0i, lam_r, lam_i, wb, wcr, wci, d_skip, w_glu, b_glu)


def _mix_gate_kernel(att_ref, ssm_ref, ga_ref, gb_ref, pa_ref, pb_ref, o_ref):
    m = (jax.nn.sigmoid(ga_ref[...]) * _dot(att_ref[...], pa_ref[...])
         + jax.nn.sigmoid(gb_ref[...]) * _dot(ssm_ref[...], pb_ref[...]))
    o_ref[...] = m.astype(o_ref.dtype)


def _mix_gate(att, ssm, h, p_a, p_b):
    T, W = att.shape
    D = p_a.shape[1]
    tm = min(256, T)
    return pl.pallas_call(
        _mix_gate_kernel,
        grid=(T // tm,),
        in_specs=[pl.BlockSpec((tm, W), lambda i: (i, 0)),
                  pl.BlockSpec((tm, W), lambda i: (i, 0)),
                  pl.BlockSpec((tm, D), lambda i: (i, 2)),
                  pl.BlockSpec((tm, D), lambda i: (i, 3)),
                  pl.BlockSpec((W, D), lambda i: (0, 0)),
                  pl.BlockSpec((W, D), lambda i: (0, 0))],
        out_specs=pl.BlockSpec((tm, D), lambda i: (i, 0)),
        out_shape=jax.ShapeDtypeStruct((T, D), BF16),
        compiler_params=_cp(("parallel",)),
        name="mix_gate",
    )(att, ssm, h, h, p_a, p_b)


def _mix_out_kernel(m_ref, x_ref, g1_ref, sh2_ref, sc2_ref, wo_ref, lg_ref, lb_ref, wr_ref, br_ref,
                    x1_ref, u2_ref, ri_ref, rw_ref, cnt_ref, cnt_scr):
    out = _dot(m_ref[...], wo_ref[...])
    x1 = _layernorm(DN_ALPHA * x_ref[...] + g1_ref[...] * out, lg_ref[...], lb_ref[...])
    x1_ref[...] = x1
    u2 = x1 * (1.0 + sc2_ref[...]) + sh2_ref[...]
    u2_ref[...] = u2
    u_hi = u2.astype(BF16)
    u_lo = (u2 - u_hi.astype(F32)).astype(BF16)
    wr = wr_ref[...]
    w_hi = wr.astype(BF16)
    w_lo = (wr - w_hi.astype(F32)).astype(BF16)
    logits = _dot(u_hi, w_hi) + _dot(u_hi, w_lo) + _dot(u_lo, w_hi) + br_ref[...]
    lane = lax.broadcasted_iota(jnp.int32, logits.shape, 1)
    cur = logits
    vals, idxs = [], []
    for _ in range(TOP_K):
        mk = jnp.max(cur, axis=1, keepdims=True)
        ik = jnp.min(jnp.where(cur == mk, lane, LANES), axis=1, keepdims=True)
        vals.append(mk)
        idxs.append(ik)
        cur = jnp.where(lane == ik, -jnp.inf, cur)
    es = [jnp.exp(v - vals[0]) for v in vals]
    den = es[0] + es[1] + es[2] + es[3]
    @pl.when(pl.program_id(0) == 0)
    def _():
        cnt_scr[...] = jnp.zeros_like(cnt_scr)

    tm = logits.shape[0]
    picked = jnp.zeros(logits.shape, F32)
    for k in range(TOP_K):
        picked = picked + (lane == idxs[k]).astype(F32)
    earlier = (lax.broadcasted_iota(jnp.int32, (tm, tm), 1)
               < lax.broadcasted_iota(jnp.int32, (tm, tm), 0)).astype(BF16)
    before = _dot(earlier, picked.astype(BF16)) + cnt_scr[...]
    ri = jnp.zeros(logits.shape, jnp.int32)
    rw = jnp.zeros(logits.shape, F32)
    for k in range(TOP_K):
        rank_k = jnp.sum(jnp.where(lane == idxs[k], before, 0.0), axis=1, keepdims=True)
        ri = jnp.where(lane == k, idxs[k], ri)
        ri = jnp.where(lane == TOP_K + k, rank_k.astype(jnp.int32), ri)
        rw = jnp.where(lane == k, es[k] / den, rw)
    ri_ref[...] = ri
    rw_ref[...] = rw
    cnt_scr[...] += jnp.sum(picked, axis=0, keepdims=True)
    cnt_ref[...] = jnp.broadcast_to(cnt_scr[...], cnt_ref.shape)


def _mix_out(m, x, g1, sh2, sc2, w_o, ln_g, ln_b, w_r, b_r, *, seq_len):
    T, D = x.shape
    tm = min(256, T)
    if seq_len > 0:
        seq_tiles = seq_len // tm
        mod_spec = pl.BlockSpec((None, 1, D), lambda i: (i // seq_tiles, 0, 0))
    else:
        mod_spec = pl.BlockSpec((tm, D), lambda i: (i, 0))
    row = pl.BlockSpec((tm, D), lambda i: (i, 0))
    vec = pl.BlockSpec((1, D), lambda i: (0, 0))
    small = pl.BlockSpec((tm, LANES), lambda i: (i, 0))
    return pl.pallas_call(
        _mix_out_kernel,
        grid=(T // tm,),
        in_specs=[row, row, mod_spec, mod_spec, mod_spec,
                  pl.BlockSpec((D, D), lambda i: (0, 0)), vec, vec,
                  pl.BlockSpec((D, LANES), lambda i: (0, 0)),
                  pl.BlockSpec((1, LANES), lambda i: (0, 0))],
        out_specs=[row, row, small, small, pl.BlockSpec((8, LANES), lambda i: (0, 0))],
        out_shape=[jax.ShapeDtypeStruct((T, D), F32), jax.ShapeDtypeStruct((T, D), F32),
                   jax.ShapeDtypeStruct((T, LANES), jnp.int32), jax.ShapeDtypeStruct((T, LANES), F32),
                   jax.ShapeDtypeStruct((8, LANES), F32)],
        scratch_shapes=[pltpu.VMEM((1, LANES), F32)],
        compiler_params=_cp(("arbitrary",)),
        name="mix_out",
    )(m, x, g1, sh2, sc2, w_o, ln_g, ln_b, w_r, b_r)


def _gather_kernel(cnt_ref, blk_ref, idx_ref, idxn_ref, x_hbm, o_ref, buf, sem):
    j = pl.program_id(0)
    nb = pl.num_programs(0)
    slot = j % 2
    jn = jnp.minimum(j + 1, nb - 1)

    def rows8(step):
        return pl.multiple_of((cnt_ref[step] + 7) // 8 * 8, 8)

    def issue_block(idx_r, n8, sl):
        def issue(g, _):
            for u in range(8):
                r = g * 8 + u
                tok = idx_r[0, 0, r]
                pltpu.make_async_copy(x_hbm.at[pl.ds(tok, 1), :], buf.at[sl, pl.ds(r, 1), :],
                                      sem.at[sl]).start(priority=u % 2)
            return 0

        lax.fori_loop(0, n8 // 8, issue, 0)

    @pl.when(j == 0)
    def _():
        buf[...] = jnp.zeros_like(buf)
        issue_block(idx_ref, rows8(0), 0)

    @pl.when((j + 1 < nb) & (cnt_ref[jn] > 0))
    def _():
        issue_block(idxn_ref, rows8(jn), 1 - slot)

    @pl.when(cnt_ref[j] > 0)
    def _():
        n8 = rows8(j)
        pltpu.make_async_copy(x_hbm.at[pl.ds(0, n8), :], buf.at[slot, pl.ds(0, n8), :], sem.at[slot]).wait()
        o_ref[...] = buf[slot].astype(o_ref.dtype)


def _moe_gather(x, idx, cnt, blk):
    T, D = x.shape
    P = idx.shape[0]
    gb = MOE_SUB
    nb = P // gb
    grid_spec = pltpu.PrefetchScalarGridSpec(
        num_scalar_prefetch=2,
        grid=(nb,),
        in_specs=[pl.BlockSpec((1, 1, gb), lambda j, cnt, blk: (blk[j], 0, 0), memory_space=pltpu.SMEM),
                  pl.BlockSpec((1, 1, gb), lambda j, cnt, blk: (blk[jnp.minimum(j + 1, nb - 1)], 0, 0),
                               memory_space=pltpu.SMEM),
                  pl.BlockSpec(memory_space=pl.ANY)],
        out_specs=pl.BlockSpec((gb, D), lambda j, cnt, blk: (blk[j], 0)),
        scratch_shapes=[pltpu.VMEM((2, gb, D), F32), pltpu.SemaphoreType.DMA((2,))],
    )
    idx3 = idx.reshape(nb, 1, gb)
    return pl.pallas_call(
        _gather_kernel,
        grid_spec=grid_spec,
        out_shape=jax.ShapeDtypeStruct((P, D), BF16),
        compiler_params=pltpu.CompilerParams(dimension_semantics=("arbitrary",), vmem_limit_bytes=VMEM_LIMIT,
                                             disable_bounds_checks=True),
        name="moe_gather",
    )(cnt, blk, idx3, idx3, x)


def _moe_kernel(be_ref, nu_ref, nv_ref, x_ref, wg_ref, wu_ref, wd_ref, bgu_ref, bdn_ref, y_ref,
                a_scr):
    i = pl.program_id(0)
    s = pl.program_id(1)
    nf = pl.num_programs(1) // 2
    valid = nv_ref[i]

    @pl.when((s < nf) & (valid > 0))
    def _():
        def build(base, m):
            rows = pl.ds(base, m)
            xs = x_ref[rows, :].astype(F32)
            gate = jnp.minimum(_dot(xs, wg_ref[...]) + bgu_ref[s], SWIGLU_LIMIT)
            up = jnp.clip(_dot(xs, wu_ref[...]) + bgu_ref[nf + s], -SWIGLU_LIMIT, SWIGLU_LIMIT)
            a_scr[s, rows, :] = ((up + 1.0) * (gate * jax.nn.sigmoid(SWIGLU_ALPHA * gate))).astype(BF16)

        _for_sub_blocks(valid, build, None)

    @pl.when((s >= nf) & (valid > 0))
    def _():
        def write(base, m):
            rows = pl.ds(base, m)
            a = jnp.concatenate([a_scr[f, rows, :] for f in range(a_scr.shape[0])], axis=1)
            y_ref[rows, :] = _dot(a.astype(F32), wd_ref[...]) + bdn_ref[s - nf]

        def clear(base, m):
            y_ref[pl.ds(base, m), :] = jnp.zeros((m, y_ref.shape[1]), F32)

        _for_sub_blocks(valid, write, clear)


def _for_sub_blocks(valid, work, rest):
    for sbk in range(MOE_BLOCK // MOE_SUB):
        base = sbk * MOE_SUB
        left = valid - base
        for m in range(MOE_TAIL, MOE_SUB + 1, MOE_TAIL):
            @pl.when((left > m - MOE_TAIL) & ((left <= m) if m < MOE_SUB else (left > m - MOE_TAIL)))
            def _(m=m):
                work(base, m)
                if rest is not None and m < MOE_SUB:
                    rest(base + m, MOE_SUB - m)

        if rest is not None:
            @pl.when(left <= 0)
            def _():
                rest(base, MOE_SUB)


def _moe_experts(xs, blk_e, nused, nvalid, w_gu, b_gu, w_dn, b_dn):
    P, D = xs.shape
    F = w_dn.shape[2]
    NB = P // MOE_BLOCK
    NF = F // MOE_FT
    ND = D // MOE_FT

    def eff(i, nu):
        return jnp.minimum(i, nu[0] - 1)

    def fb(i, s, nu):
        return jnp.where(i < nu[0], jnp.maximum(s - NF, 0), ND - 1)

    def nxt(i, nu):
        return jnp.minimum(i + 1, nu[0] - 1)

    def x_map(i, s, be, nu, nv):
        return (jnp.where((i < nu[0]) & (s < NF), eff(i, nu), nxt(i, nu)), 0)

    def gu_tile(i, s, be, nu, last):
        here = (i < nu[0]) & (s <= last)
        return (jnp.where(here, be[eff(i, nu)], be[nxt(i, nu)]), jnp.where(here, jnp.minimum(s, NF - 1), 0))

    def dn_tile(i, s, be, nu):
        used = i < nu[0]
        prev = jnp.maximum(eff(i, nu) - 1, 0)
        e = jnp.where(used & (s < NF), be[prev], be[eff(i, nu)])
        return e, jnp.where(used & (s >= NF), s - NF, ND - 1)

    def wg_map(i, s, be, nu, nv):
        e, f = gu_tile(i, s, be, nu, NF + ND - 2)
        return (0, e, 0, f)

    def wu_map(i, s, be, nu, nv):
        e, f = gu_tile(i, s, be, nu, NF + ND - 1)
        return (0, e, 0, NF + f)

    def wd_map(i, s, be, nu, nv):
        e, n = dn_tile(i, s, be, nu)
        return (0, e, 0, n)

    def bgu_map(i, s, be, nu, nv):
        return (gu_tile(i, s, be, nu, NF + ND - 1)[0], 0, 0, 0)

    def bdn_map(i, s, be, nu, nv):
        return (dn_tile(i, s, be, nu)[0], 0, 0, 0)

    grid_spec = pltpu.PrefetchScalarGridSpec(
        num_scalar_prefetch=3,
        grid=(NB, NF + ND),
        in_specs=[
            pl.BlockSpec((MOE_BLOCK, D), x_map),
            pl.BlockSpec((None, None, D, MOE_FT), wg_map),
            pl.BlockSpec((None, None, D, MOE_FT), wu_map),
            pl.BlockSpec((None, None, F, MOE_FT), wd_map),
            pl.BlockSpec((None, 2 * NF, 1, MOE_FT), bgu_map),
            pl.BlockSpec((None, ND, 1, MOE_FT), bdn_map),
        ],
        out_specs=pl.BlockSpec((MOE_BLOCK, MOE_FT), lambda i, s, be, nu, nv: (eff(i, nu), fb(i, s, nu))),
        scratch_shapes=[pltpu.VMEM((NF, MOE_BLOCK, MOE_FT), BF16)],
    )
    assert NF == ND
    return pl.pallas_call(
        _moe_kernel,
        grid_spec=grid_spec,
        out_shape=jax.ShapeDtypeStruct((P, D), F32),
        compiler_params=_cp(("arbitrary", "arbitrary"), MOE_VMEM_LIMIT),
        name="moe_experts",
    )(blk_e, nused, nvalid, xs, w_gu, w_gu, w_dn,
      b_gu.reshape(-1, 2 * NF, 1, MOE_FT), b_dn.reshape(-1, ND, 1, MOE_FT))


def _combine_kernel(idx_ref, idxn_ref, y_hbm, x1_ref, g2_ref, rw_ref, lg_ref, lb_ref, o_ref, buf, sem):
    i = pl.program_id(0)
    nt = pl.num_programs(0)
    slot = i % 2
    tm = x1_ref.shape[0]

    def issue_tile(idx_r, sl):
        def issue(r, _):
            for k in range(TOP_K):
                d = idx_r[0, 0, k * tm + r]
                pltpu.make_async_copy(y_hbm.at[pl.ds(d, 1), :], buf.at[sl, k, pl.ds(r, 1), :],
                                      sem.at[sl]).start(priority=k % 2)
            return 0

        lax.fori_loop(0, tm, issue, 0, unroll=2)

    @pl.when(i == 0)
    def _():
        issue_tile(idx_ref, 0)

    @pl.when(i + 1 < nt)
    def _():
        issue_tile(idxn_ref, 1 - slot)

    for k in range(TOP_K):
        pltpu.make_async_copy(y_hbm.at[pl.ds(0, tm), :], buf.at[slot, k], sem.at[slot]).wait()
    rw = rw_ref[...]
    moe = rw[:, 0:1] * buf[slot, 0]
    for k in range(1, TOP_K):
        moe = moe + rw[:, k:k + 1] * buf[slot, k]
    o_ref[...] = _layernorm(DN_ALPHA * x1_ref[...] + g2_ref[...] * moe, lg_ref[...], lb_ref[...])


def _moe_combine(y, dest, x1, g2, rw, ln_g, ln_b, *, seq_len):
    T, D = x1.shape
    tm = 128
    nt = T // tm
    idx = dest.reshape(nt, tm, TOP_K).transpose(0, 2, 1).reshape(nt, 1, TOP_K * tm)
    if seq_len > 0:
        seq_tiles = seq_len // tm
        mod_spec = pl.BlockSpec((None, 1, D), lambda i: (i // seq_tiles, 0, 0))
    else:
        mod_spec = pl.BlockSpec((tm, D), lambda i: (i, 0))
    row = pl.BlockSpec((tm, D), lambda i: (i, 0))
    vec = pl.BlockSpec((1, D), lambda i: (0, 0))
    return pl.pallas_call(
        _combine_kernel,
        grid=(nt,),
        in_specs=[pl.BlockSpec((1, 1, TOP_K * tm), lambda i: (i, 0, 0), memory_space=pltpu.SMEM),
                  pl.BlockSpec((1, 1, TOP_K * tm), lambda i: (jnp.minimum(i + 1, nt - 1), 0, 0),
                               memory_space=pltpu.SMEM),
                  pl.BlockSpec(memory_space=pl.ANY),
                  row, mod_spec, pl.BlockSpec((tm, LANES), lambda i: (i, 0)), vec, vec],
        out_specs=row,
        out_shape=jax.ShapeDtypeStruct((T, D), F32),
        scratch_shapes=[pltpu.VMEM((2, TOP_K, tm, D), F32), pltpu.SemaphoreType.DMA((2,))],
        compiler_params=pltpu.CompilerParams(dimension_semantics=("arbitrary",), vmem_limit_bytes=VMEM_LIMIT,
                                             disable_bounds_checks=True),
        name="moe_combine",
    )(idx, idx, y, x1, g2, rw, ln_g, ln_b)


def _lookup(table, idx):
    hit = idx[..., None] == jnp.arange(N_EXPERTS, dtype=jnp.int32)
    return jnp.sum(jnp.where(hit, table, 0), axis=-1)


def _route(experts, ranks, counts):
    T = experts.shape[0]
    A = T * TOP_K
    flat_e = experts.reshape(A)
    rank = ranks.reshape(A)
    nblk = (counts + MOE_BLOCK - 1) // MOE_BLOCK
    blk_end = jnp.cumsum(nblk)
    blk_start = blk_end - nblk
    dest = _lookup(blk_start, flat_e) * MOE_BLOCK + rank
    NB = -(-A // MOE_BLOCK) + N_EXPERTS
    idx = jnp.zeros((NB * MOE_BLOCK,), jnp.int32).at[dest].set(
        jnp.arange(A, dtype=jnp.int32) // TOP_K, unique_indices=True, mode='promise_in_bounds')
    blk_e = jnp.minimum(jnp.searchsorted(blk_end, jnp.arange(NB, dtype=jnp.int32), side='right'),
                        N_EXPERTS - 1).astype(jnp.int32)
    nused = blk_end[-1:].astype(jnp.int32)
    blk = jnp.arange(NB, dtype=jnp.int32)
    nvalid = jnp.clip(counts[blk_e] - (blk - blk_start[blk_e]) * MOE_BLOCK, 0, MOE_BLOCK)
    nvalid = jnp.where(blk < nused[0], nvalid, 0).astype(jnp.int32)
    per = MOE_BLOCK // MOE_SUB
    gblk = jnp.arange(NB * per, dtype=jnp.int32)
    gcnt = jnp.clip(jnp.repeat(nvalid, per) - (gblk % per) * MOE_SUB, 0, MOE_SUB).astype(jnp.int32)
    order = jnp.argsort(gcnt == 0, stable=True).astype(jnp.int32)
    nnz = jnp.sum(gcnt > 0)
    gvisit = jnp.where(gblk < nnz, order, order[jnp.maximum(nnz - 1, 0)])
    gcnt = jnp.where(gblk < nnz, gcnt[order], 0)
    return dest.reshape(T, TOP_K), idx, blk_e, nused, nvalid, gcnt, gvisit


def _block_diag_in(bb):
    gpc = S5_GROUPS // S5_CHUNKS
    x = bb.reshape(S5_CHUNKS, gpc, S5_GROUP, S5_STATE)
    eye = jnp.eye(gpc, dtype=bb.dtype)
    return jnp.einsum('agcn,gh->agchn', x, eye).reshape(S5_CHUNKS, S5_CHUNK_IN, S5_CHUNK_STATE)


def _block_diag_out(cc):
    gpc = S5_GROUPS // S5_CHUNKS
    x = cc.reshape(S5_CHUNKS, gpc, S5_GROUP, S5_STATE)
    eye = jnp.eye(gpc, dtype=cc.dtype)
    return jnp.einsum('agcn,gh->ahngc', x, eye).reshape(S5_CHUNKS, S5_CHUNK_STATE, S5_CHUNK_IN)


def kernel(x_prompt, x_sample, c_prompt, c_sample, cache_k, cache_v, cache_logf, state_s5_re, state_s5_im, page_table, w_ada, b_ada, w_in, fox_b_f, s5_a_re, s5_a_im, s5_log_dt, s5_b_re, s5_b_im, s5_c_re, s5_c_im, s5_d, w_glu, b_glu, p_a, p_b, w_o, ln1_g, ln1_b, w_router, b_router, w_gu, b_gu, w_dn, b_dn, ln2_g, ln2_b):
    B, L, D = x_prompt.shape
    Bd, Ld, _ = x_sample.shape
    Tp, Ts = B * L, Bd * Ld
    W = HEADS * HEAD_DIM
    n_phys = cache_k.shape[1]

    n_c = B + Bd
    c_all = jnp.concatenate([c_prompt, c_sample, jnp.zeros((-n_c % 8, D), F32)], axis=0)
    mod = _adaln(c_all, w_ada[0], b_ada)
    mp = [mod[:B, i * D:(i + 1) * D].reshape(B, 1, D) for i in range(6)]
    ms = [jnp.repeat(mod[B:n_c, i * D:(i + 1) * D], Ld, axis=0) for i in range(6)]

    wit = w_in[0].T
    f_lo = 3 * W
    wmt = jnp.concatenate([wit[:f_lo], wit[f_lo + HEADS:]], axis=0).astype(BF16)
    wft = jnp.pad(wit[f_lo:f_lo + HEADS], ((0, LANES - HEADS), (0, 0))).astype(BF16)
    bf = jnp.pad(fox_b_f, ((0, 0), (0, LANES - HEADS)))
    hp, logf_p, cT_p = _in_proj(x_prompt.reshape(Tp, D), mp[0], mp[1], wmt, wft.T, wft, bf, bf.T,
                                seq_len=L, seg=0)
    hs, logf_s, c_s = _in_proj(x_sample.reshape(Ts, D), ms[0], ms[1], wmt, wft.T, wft, bf, bf.T,
                               seq_len=0, seg=Ld)

    lam_r, lam_i, bb_r, bb_i = _s5_params(s5_a_re[0], s5_a_im[0], s5_log_dt[0][:, None],
                                          s5_b_re[0].transpose(0, 2, 1), s5_b_im[0].transpose(0, 2, 1))
    lam_r = lam_r.reshape(S5_CHUNKS, S5_CHUNK_STATE)
    lam_i = lam_i.reshape(S5_CHUNKS, S5_CHUNK_STATE)
    wb = jnp.concatenate([_block_diag_in(bb_r), _block_diag_in(bb_i)], axis=2).astype(BF16)
    wcr = _block_diag_out(s5_c_re[0]).astype(BF16)
    wci = _block_diag_out(s5_c_im[0]).astype(BF16)
    wgl = w_glu[0].astype(BF16)
    s5_args = (lam_r, lam_i, wb, wcr, wci, s5_d, wgl, b_glu)
    zeros_state = jnp.zeros((B, S5_CHUNKS, S5_CHUNK_STATE), F32)
    s5_t = 256
    ssm_p, sr_p, si_p = _s5(hp, 3, zeros_state, zeros_state, *s5_args, nseq=1, slen=s5_t, chunks=L // s5_t)
    ssm_s, sr_s, si_s = _s5(hs, 3, state_s5_re[0].reshape(Bd, S5_CHUNKS, S5_CHUNK_STATE),
                            state_s5_im[0].reshape(Bd, S5_CHUNKS, S5_CHUNK_STATE), *s5_args,
                            nseq=32, slen=Ld, chunks=1)

    att_p = _fox_prompt(hp, cT_p, B, L)
    scale = HEAD_DIM ** -0.5
    q_s = hs[:, :W].reshape(Bd, Ld, HEADS, HEAD_DIM) * scale
    k_s = hs[:, W:2 * W].reshape(Bd, Ld, HEADS, HEAD_DIM)
    v_s = hs[:, 2 * W:3 * W].reshape(Bd, Ld, HEADS, HEAD_DIM)
    qe = q_s.reshape(Bd, Ld * HEADS, HEAD_DIM)
    kn = jnp.broadcast_to(k_s[:, :, None], (Bd, Ld, Ld, HEADS, HEAD_DIM)).reshape(Bd, Ld, Ld * HEADS, HEAD_DIM)
    vn = jnp.broadcast_to(v_s[:, :, None], (Bd, Ld, Ld, HEADS, HEAD_DIM)).reshape(Bd, Ld, Ld * HEADS, HEAD_DIM)
    cn = c_s[:, :HEADS].reshape(Bd, Ld, HEADS)
    bn = jnp.broadcast_to(cn.transpose(0, 2, 1)[:, None], (Bd, Ld, HEADS, Ld)).reshape(Bd, Ld * HEADS, Ld)
    bn = jnp.pad(bn, ((0, 0), (0, 0), (0, LANES - Ld)))
    ckt = jnp.transpose(cache_k[0], (0, 2, 3, 1)).reshape(n_phys, W, PAGE)
    cvt = jnp.transpose(cache_v[0], (0, 2, 3, 1)).reshape(n_phys, W, PAGE)
    clft = jnp.transpose(cache_logf[0], (0, 2, 1))
    att_s = _fox_sample(page_table, ckt, cvt, clft, qe, kn, vn, bn)
    att_s = att_s.reshape(Ts, W).astype(BF16)

    pab, pbb, wob = p_a[0].astype(BF16), p_b[0].astype(BF16), w_o[0].astype(BF16)
    wr = jnp.pad(w_router[0], ((0, 0), (0, LANES - N_EXPERTS)))
    br = jnp.pad(b_router, ((0, 0), (0, LANES - N_EXPERTS)), constant_values=NEG)
    m_p = _mix_gate(att_p, ssm_p, hp, pab, pbb)
    m_s = _mix_gate(att_s, ssm_s, hs, pab, pbb)
    x1_p, u2_p, ri_p, rw_p, cnt_p = _mix_out(m_p, x_prompt.reshape(Tp, D), mp[2], mp[3], mp[4], wob, ln1_g, ln1_b,
                                             wr, br, seq_len=L)
    x1_s, u2_s, ri_s, rw_s, cnt_s = _mix_out(m_s, x_sample.reshape(Ts, D), ms[2], ms[3], ms[4], wob, ln1_g, ln1_b,
                                             wr, br, seq_len=0)

    u2 = jnp.concatenate([u2_p, u2_s], axis=0)
    cnt_p = cnt_p[0, :N_EXPERTS].astype(jnp.int32)
    cnt_s = cnt_s[0, :N_EXPERTS].astype(jnp.int32)
    ex_p, ex_s = ri_p[:, :TOP_K], ri_s[:, :TOP_K]
    experts = jnp.concatenate([ex_p, ex_s], axis=0)
    ranks = jnp.concatenate([ri_p[:, TOP_K:2 * TOP_K], ri_s[:, TOP_K:2 * TOP_K] + _lookup(cnt_p, ex_s)], axis=0)
    dest, idx, blk_e, nused, nvalid, gcnt, gvisit = _route(experts, ranks, cnt_p + cnt_s)
    xs = _moe_gather(u2, idx, gcnt, gvisit)
    y = _moe_experts(xs, blk_e, nused, nvalid, w_gu, b_gu[0], w_dn, b_dn[0])
    x2_p = _moe_combine(y, dest[:Tp], x1_p, mp[5], rw_p, ln2_g, ln2_b, seq_len=L)
    x2_s = _moe_combine(y, dest[Tp:], x1_s, ms[5], rw_s, ln2_g, ln2_b, seq_len=0)

    def kv_out(t, n, l):
        return t.reshape(n, HEADS, HEAD_DIM, l).transpose(0, 3, 1, 2)[None]

    def heads(t, n, l):
        return t.reshape(1, n, l, HEADS, HEAD_DIM)

    def state(s, n):
        return s.reshape(1, n, S5_GROUPS, S5_STATE)

    return (x2_p.reshape(B, L, D), x2_s.reshape(Bd, Ld, D),
            kv_out(_heads_minor_time(hp, 1, B, L), B, L), kv_out(_heads_minor_time(hp, 2, B, L), B, L),
            logf_p[:, :HEADS].reshape(1, B, L, HEADS), state(sr_p, B), state(si_p, B),
            heads(hs[:, W:2 * W], Bd, Ld), heads(hs[:, 2 * W:3 * W], Bd, Ld),
            logf_s[:, :HEADS].reshape(1, Bd, Ld, HEADS), state(sr_s, Bd), state(si_s, Bd))
```
